```python
import math
import jax
import jax.numpy as jnp
from jax import lax
import numpy as np

D_MODEL = 1024
BATCH = 8
SEQ = 2048
DEPTH = 1
DEC_BATCH = 128
DEC_SEQ = 1
PAST_LEN = 16384
PAGE_SIZE = 128

H_A = 8
HD_A = 64
D_A = H_A * HD_A
A_PATTERNS = ((128, 1), (512, 4), (2048, 16))
W_A = 2048
A_BLOCK = 128
H_B = 8
KV_B = 2
HD_B = 64
D_B = H_B * HD_B
W_B = 128
H_C = 4
HD_C = 128
D_C = H_C * HD_C
N_MEM = 256
NUM_BUCKETS = 32
MAX_DISTANCE = 2048
EPS = 1e-6
IN_SIZES = (D_A, D_A, D_A, D_A,
            D_B, KV_B * HD_B, KV_B * HD_B, D_B,
            D_C, D_C,
            D_MODEL, D_MODEL, D_MODEL)
IN_WIDTH = sum(IN_SIZES)

kernel_name = "hybrid_dilated_swa_memory_decoder_step"


def rms_norm(x, g):
    xf = x.astype(jnp.float32)
    y = xf * lax.rsqrt(jnp.mean(xf * xf, axis=-1, keepdims=True) + EPS)
    return (y * g.astype(jnp.float32)).astype(x.dtype)


def t5_bucket(dist):
    n = jnp.maximum(dist, 0)
    max_exact = NUM_BUCKETS // 2
    nf = jnp.maximum(n, 1).astype(jnp.float32)
    large = max_exact + (jnp.log(nf / max_exact) / math.log(MAX_DISTANCE / max_exact)
                         * (NUM_BUCKETS - max_exact)).astype(jnp.int32)
    return jnp.where(n < max_exact, n, jnp.minimum(large, NUM_BUCKETS - 1))


def heads(t, n, hd, g=None):
    t = t.reshape(t.shape[:-1] + (n, hd))
    return t if g is None else rms_norm(t, g)


def masked_softmax_stats(logits, mask):
    logits = jnp.where(mask, logits, -jnp.inf)
    m = jnp.max(logits, axis=-1, keepdims=True)
    p = jnp.exp(logits - m)
    s = jnp.sum(p, axis=-1, keepdims=True)
    return p / s, (m + jnp.log(s))[..., 0]


def sink_softmax(logits, mask, sink):
    logits = jnp.where(mask, logits, -jnp.inf)
    sink = sink.astype(jnp.float32)
    m = jnp.maximum(jnp.max(logits, axis=-1, keepdims=True), sink)
    p = jnp.exp(logits - m)
    return p / (jnp.sum(p, axis=-1, keepdims=True) + jnp.exp(sink - m))


def combine_by_denominator(outs, lses):
    wts = jax.nn.softmax(jnp.stack(lses, 0), axis=0)
    return jnp.einsum('gbsh,gbshe->bshe', wts, jnp.stack(outs, 0))


def dilated_attn_prompt(q, k, v, table_a):
    b, s, h, e = q.shape
    scale = HD_A ** -0.5
    outs, lses = [], []
    for w, d in A_PATTERNS:
        nb = w // d
        L = s // d
        c = math.gcd(L, A_BLOCK)
        nblk = L // c

        def strided(t):
            return t.reshape(b, L, d, h, e).transpose(0, 2, 1, 3, 4)

        qs = strided(q).reshape(b, d, nblk, c, h, e)
        pad = ((0, 0), (0, 0), (nb, 0), (0, 0), (0, 0))
        kp = jnp.pad(strided(k), pad)
        vp = jnp.pad(strided(v), pad)
        idx = jnp.arange(nblk)[:, None] * c + jnp.arange(c + nb)[None, :]
        kb = kp[:, :, idx]
        vb = vp[:, :, idx]
        logits = jnp.einsum('bdnqhe,bdnshe->bdnhqs', qs, kb,
                            preferred_element_type=jnp.float32) * scale
        rel = jnp.arange(c)[:, None] + nb - jnp.arange(c + nb)[None, :]
        bias = table_a[t5_bucket(rel * d)].astype(jnp.float32).transpose(2, 0, 1)
        mask = ((rel >= 0) & (rel <= nb))[None] & (idx >= nb)[:, None, :]
        p, lse = masked_softmax_stats(logits + bias, mask[None, None, :, None])
        o = jnp.einsum('bdnhqs,bdnshe->bdnqhe', p, vb.astype(jnp.float32))
        outs.append(o.reshape(b, d, L, h, e).transpose(0, 2, 1, 3, 4).reshape(b, s, h, e))
        lses.append(lse.transpose(0, 1, 2, 4, 3).reshape(b, d, L, h)
                    .transpose(0, 2, 1, 3).reshape(b, s, h))
    return combine_by_denominator(outs, lses)


def dilated_attn_sample(q, k_all, v_all, table_a):
    b, t, h, e = q.shape
    scale = HD_A ** -0.5
    wb = k_all.shape[1] - t
    qidx = wb + jnp.arange(t)
    outs, lses = [], []
    for w, d in A_PATTERNS:
        nb = w // d
        steps = jnp.arange(nb + 1)
        idx = qidx[:, None] - steps[None, :] * d
        safe = jnp.maximum(idx, 0)
        kg = k_all[:, safe]
        vg = v_all[:, safe]
        logits = jnp.einsum('bqhe,bqshe->bhqs', q, kg,
                            preferred_element_type=jnp.float32) * scale
        bias = table_a[t5_bucket(steps * d)].astype(jnp.float32).T[:, None, :]
        p, lse = masked_softmax_stats(logits + bias, (idx >= 0)[None, None])
        outs.append(jnp.einsum('bhqs,bqshe->bqhe', p, vg.astype(jnp.float32)))
        lses.append(lse.transpose(0, 2, 1))
    return combine_by_denominator(outs, lses)


def swa_sink_prompt(q, k, v, table_b, sinks):
    b, s, h, e = q.shape
    g = h // KV_B
    c = W_B
    nblk = s // c
    qb = q.reshape(b, nblk, c, KV_B, g, e)

    def band(t):
        tp = jnp.pad(t, ((0, 0), (c, 0), (0, 0), (0, 0))).reshape(b, nblk + 1, c, KV_B, e)
        return jnp.concatenate([tp[:, :-1], tp[:, 1:]], axis=2)

    kb, vb = band(k), band(v)
    logits = jnp.einsum('bnqhgd,bnshd->bnhgqs', qb, kb,
                        preferred_element_type=jnp.float32) * (HD_B ** -0.5)
    rel = jnp.arange(c)[:, None] + c - jnp.arange(2 * c)[None, :]
    bias = table_b[t5_bucket(rel)].astype(jnp.float32).transpose(2, 0, 1).reshape(KV_B, g, c, 2 * c)
    kpos_ok = (jnp.arange(nblk)[:, None] * c + jnp.arange(2 * c)[None, :]) >= c
    mask = ((rel >= 0) & (rel < W_B))[None] & kpos_ok[:, None, :]
    p = sink_softmax(logits + bias, mask[None, :, None, None], sinks.reshape(KV_B, g)[:, :, None, None])
    o = jnp.einsum('bnhgqs,bnshd->bnqhgd', p, vb.astype(jnp.float32))
    return o.reshape(b, s, h, e)


def swa_sink_sample(q, k_all, v_all, table_b, sinks):
    b, t, h, e = q.shape
    g = h // KV_B
    wb = k_all.shape[1] - t
    rel = (wb + jnp.arange(t))[:, None] - jnp.arange(wb + t)[None, :]
    mask = (rel >= 0) & (rel < W_B)
    qg = q.reshape(b, t, KV_B, g, e)
    logits = jnp.einsum('bqhgd,bshd->bhgqs', qg, k_all,
                        preferred_element_type=jnp.float32) * (HD_B ** -0.5)
    bias = table_b[t5_bucket(rel)].astype(jnp.float32).transpose(2, 0, 1).reshape(KV_B, g, t, wb + t)
    p = sink_softmax(logits + bias, mask, sinks.reshape(KV_B, g)[:, :, None, None])
    o = jnp.einsum('bhgqs,bshd->bqhgd', p, v_all.astype(jnp.float32))
    return o.reshape(b, t, h, e)


def mem_attn(q, mk, mv):
    logits = jnp.einsum('bqhd,bmhd->bhqm', q, mk,
                        preferred_element_type=jnp.float32) * (HD_C ** -0.5)
    p = jax.nn.softmax(logits, axis=-1)
    return jnp.einsum('bhqm,bmhd->bqhd', p, mv.astype(jnp.float32))


def mem_kv(mem, mem_ln_g, w_mem_kv, gk_c):
    mk, mv = jnp.split(rms_norm(mem, mem_ln_g) @ w_mem_kv, 2, axis=-1)
    return heads(mk, H_C, HD_C, gk_c), heads(mv, H_C, HD_C)


def layer_in(x, ln_g, w_in, gq_a, gk_a, gq_b, gk_b, gq_c):
    h = rms_norm(x, ln_g)
    pts = np.cumsum(IN_SIZES)[:-1].tolist()
    qa, ka, va, za, qb, kb, vb, zb, qc, zc, ga, gb, gc = jnp.split(h @ w_in, pts, axis=-1)
    qa = heads(qa, H_A, HD_A, gq_a)
    ka = heads(ka, H_A, HD_A, gk_a)
    va = heads(va, H_A, HD_A)
    qb = heads(qb, H_B, HD_B, gq_b)
    kb = heads(kb, KV_B, HD_B, gk_b)
    vb = heads(vb, KV_B, HD_B)
    qc = heads(qc, H_C, HD_C, gq_c)
    return qa, ka, va, za, qb, kb, vb, zb, qc, zc, ga, gb, gc


def layer_out(x, oa, ob, oc, za, zb, zc, ga, gb, gc, w_br_a, w_br_b, w_br_c, w_out):
    def branch(o, z, w):
        y = o.reshape(z.shape).astype(jnp.float32) * jax.nn.silu(z.astype(jnp.float32))
        return y.astype(x.dtype) @ w
    m = (jax.nn.sigmoid(ga) * branch(oa, za, w_br_a)
         + jax.nn.sigmoid(gb) * branch(ob, zb, w_br_b)
         + jax.nn.sigmoid(gc) * branch(oc, zc, w_br_c))
    return x + m @ w_out


def setup_inputs(seed: int = 0) -> dict:
    key = jax.random.key(seed)
    ks = jax.random.split(key, 32)
    f32 = jnp.float32

    def nrm(k, shape, scale=1.0):
        return jax.random.normal(k, shape, f32) * scale

    wa_buf = min(W_A, PAST_LEN)
    wb_buf = min(W_B, PAST_LEN)
    return {
        "x_prompt": nrm(ks[0], (BATCH, SEQ, D_MODEL)),
        "x_sample": nrm(ks[1], (DEC_BATCH, DEC_SEQ, D_MODEL)),
        "mem_prompt": nrm(ks[2], (BATCH, N_MEM, D_MODEL)),
        "cache_a_k": nrm(ks[3], (DEPTH, DEC_BATCH, wa_buf, H_A, HD_A)),
        "cache_a_v": nrm(ks[4], (DEPTH, DEC_BATCH, wa_buf, H_A, HD_A)),
        "cache_b_k": nrm(ks[5], (DEPTH, DEC_BATCH, wb_buf, KV_B, HD_B)),
        "cache_b_v": nrm(ks[6], (DEPTH, DEC_BATCH, wb_buf, KV_B, HD_B)),
        "cache_mem_k": nrm(ks[7], (DEPTH, DEC_BATCH, N_MEM, H_C, HD_C)),
        "cache_mem_v": nrm(ks[8], (DEPTH, DEC_BATCH, N_MEM, H_C, HD_C)),
        "rel_bias": nrm(ks[9], (NUM_BUCKETS, H_A + H_B), 0.5),
        "ln_g": 1.0 + nrm(ks[10], (DEPTH, D_MODEL), 0.05),
        "w_in": nrm(ks[11], (DEPTH, D_MODEL, IN_WIDTH), D_MODEL ** -0.5),
        "gq_a": 1.0 + nrm(ks[12], (DEPTH, HD_A), 0.05),
        "gk_a": 1.0 + nrm(ks[13], (DEPTH, HD_A), 0.05),
        "gq_b": 1.0 + nrm(ks[14], (DEPTH, HD_B), 0.05),
        "gk_b": 1.0 + nrm(ks[15], (DEPTH, HD_B), 0.05),
        "gq_c": 1.0 + nrm(ks[16], (DEPTH, HD_C), 0.05),
        "gk_c": 1.0 + nrm(ks[17], (DEPTH, HD_C), 0.05),
        "sinks_b": nrm(ks[18], (DEPTH, H_B), 0.5),
        "mem_ln_g": 1.0 + nrm(ks[19], (DEPTH, D_MODEL), 0.05),
        "w_mem_kv": nrm(ks[20], (DEPTH, D_MODEL, 2 * D_C), D_MODEL ** -0.5),
        "w_br_a": nrm(ks[21], (DEPTH, D_A, D_MODEL), D_A ** -0.5),
        "w_br_b": nrm(ks[22], (DEPTH, D_B, D_MODEL), D_B ** -0.5),
        "w_br_c": nrm(ks[23], (DEPTH, D_C, D_MODEL), D_C ** -0.5),
        "w_out": nrm(ks[24], (DEPTH, D_MODEL, D_MODEL), D_MODEL ** -0.5),
    }


def reference(x_prompt, x_sample, mem_prompt, cache_a_k, cache_a_v, cache_b_k, cache_b_v,
              cache_mem_k, cache_mem_v, rel_bias, ln_g, w_in, gq_a, gk_a, gq_b, gk_b, gq_c, gk_c,
              sinks_b, mem_ln_g, w_mem_kv, w_br_a, w_br_b, w_br_c, w_out):
    table_a = rel_bias[:, :H_A]
    table_b = rel_bias[:, H_A:]
    xp, xs = x_prompt, x_sample
    pa_k, pa_v, pb_k, pb_v, pm_k, pm_v = [], [], [], [], [], []
    sa_k, sa_v, sb_k, sb_v = [], [], [], []
    for l in range(DEPTH):
        qa, ka, va, za, qb, kb, vb, zb, qc, zc, ga, gb, gc = layer_in(
            xp, ln_g[l], w_in[l], gq_a[l], gk_a[l], gq_b[l], gk_b[l], gq_c[l])
        mk, mv = mem_kv(mem_prompt, mem_ln_g[l], w_mem_kv[l], gk_c[l])
        oa = dilated_attn_prompt(qa, ka, va, table_a)
        ob = swa_sink_prompt(qb, kb, vb, table_b, sinks_b[l])
        oc = mem_attn(qc, mk, mv)
        xp = layer_out(xp, oa, ob, oc, za, zb, zc, ga, gb, gc,
                       w_br_a[l], w_br_b[l], w_br_c[l], w_out[l])
        pa_k.append(ka[:, -W_A:])
        pa_v.append(va[:, -W_A:])
        pb_k.append(kb[:, -W_B:])
        pb_v.append(vb[:, -W_B:])
        pm_k.append(mk)
        pm_v.append(mv)
        qa, ka, va, za, qb, kb, vb, zb, qc, zc, ga, gb, gc = layer_in(
            xs, ln_g[l], w_in[l], gq_a[l], gk_a[l], gq_b[l], gk_b[l], gq_c[l])
        ka_all = jnp.concatenate([cache_a_k[l], ka], axis=1)
        va_all = jnp.concatenate([cache_a_v[l], va], axis=1)
        kb_all = jnp.concatenate([cache_b_k[l], kb], axis=1)
        vb_all = jnp.concatenate([cache_b_v[l], vb], axis=1)
        oa = dilated_attn_sample(qa, ka_all, va_all, table_a)
        ob = swa_sink_sample(qb, kb_all, vb_all, table_b, sinks_b[l])
        oc = mem_attn(qc, cache_mem_k[l], cache_mem_v[l])
        xs = layer_out(xs, oa, ob, oc, za, zb, zc, ga, gb, gc,
                       w_br_a[l], w_br_b[l], w_br_c[l], w_out[l])
        sa_k.append(ka)
        sa_v.append(va)
        sb_k.append(kb)
        sb_v.append(vb)
    return (xp, xs,
            jnp.stack(pa_k), jnp.stack(pa_v), jnp.stack(pb_k), jnp.stack(pb_v),
            jnp.stack(pm_k), jnp.stack(pm_v),
            jnp.stack(sa_k), jnp.stack(sa_v), jnp.stack(sb_k), jnp.stack(sb_v))
```

```python
import functools
import math

import numpy as np
import jax
import jax.numpy as jnp
from jax import lax
from jax.experimental import pallas as pl
from jax.experimental.pallas import tpu as pltpu

F32 = jnp.float32
BF16 = jnp.bfloat16

D_MODEL = 1024
H_A, HD_A = 8, 64
A_DILATIONS = (1, 4, 16)
A_STEPS = 128
H_B, KV_B, HD_B, W_B = 8, 2, 64, 128
H_C, HD_C, N_MEM = 4, 128, 256
NUM_BUCKETS, MAX_DISTANCE = 32, 2048
EPS = 1e-6
NEG = -1e30
BLK = 128

C_QA, C_KA, C_VA, C_ZA = 0, 512, 1024, 1536
C_QB, C_KB, C_VB, C_ZB = 2048, 2560, 2688, 2816
C_QC, C_ZC = 3328, 3840
C_GA, C_GB, C_GC, C_END = 4352, 5376, 6400, 7424

VMEM_LIMIT = 56 * 1024 * 1024


def _cparams(sem):
    return pltpu.CompilerParams(dimension_semantics=sem, vmem_limit_bytes=VMEM_LIMIT)


def _const_spec(shape):
    nd = len(shape)
    return pl.BlockSpec(shape, lambda *_: (0,) * nd, pipeline_mode=pl.Buffered(1))


def _t5_bucket_np(dist):
    n = np.maximum(dist, 0)
    max_exact = NUM_BUCKETS // 2
    nf = np.maximum(n, 1).astype(np.float32)
    large = max_exact + (np.log(nf / np.float32(max_exact))
                         / np.float32(math.log(MAX_DISTANCE / max_exact))
                         * np.float32(NUM_BUCKETS - max_exact)).astype(np.int32)
    return np.where(n < max_exact, n, np.minimum(large, NUM_BUCKETS - 1)).astype(np.int32)


def _bucket_tables():
    i = np.arange(BLK)[:, None]
    j = np.arange(2 * BLK)[None, :]
    rel = i + BLK - j
    bkt_a = np.stack([np.where((rel >= 0) & (rel <= A_STEPS), _t5_bucket_np(rel * d), -1)
                      for d in A_DILATIONS]).astype(np.int32)
    bkt_b = np.where((rel >= 0) & (rel < W_B), _t5_bucket_np(rel), -1).astype(np.int32)
    m = np.arange(BLK)[:, None]
    sb_a = np.stack([np.broadcast_to(_t5_bucket_np((A_STEPS - m) * d), (BLK, 128))
                     for d in A_DILATIONS]).astype(np.int32)
    relb = W_B - m
    sb_b = np.broadcast_to(np.where(relb < W_B, _t5_bucket_np(relb), -1), (BLK, 128)).astype(np.int32)
    return bkt_a, bkt_b, sb_a, sb_b


def _seg_matrices(width, seg):
    c = np.arange(width)
    e = np.zeros((width, 128), np.float32)
    e[c, c // seg] = 1.0
    return jnp.asarray(e, BF16), jnp.asarray(e.T.copy(), BF16)


def _rms_rows(x, g):
    return x * lax.rsqrt(jnp.mean(x * x, axis=-1, keepdims=True) + EPS) * g


def _dot(a, b):
    return jnp.dot(a, b, preferred_element_type=F32)


def _dot2(x, w):
    hi = x.astype(BF16)
    lo = (x - hi.astype(F32)).astype(BF16)
    return _dot(lo, w) + _dot(hi, w)


def _headnorm64(t, g, pmat):
    sq = t * t
    hi = sq.astype(BF16)
    lo = (sq - hi.astype(F32)).astype(BF16)
    width = t.shape[1]
    step = min(width, 256)
    pp = pmat[:step, :step]
    parts = [_dot(lo[:, c:c + step], pp) + _dot(hi[:, c:c + step], pp) for c in range(0, width, step)]
    ss = parts[0] if len(parts) == 1 else jnp.concatenate(parts, axis=1)
    return t * lax.rsqrt(ss * (1.0 / 64.0) + EPS) * g


def _headnorm128(t, g):
    parts = []
    for c in range(0, t.shape[1], 128):
        seg = t[:, c:c + 128]
        parts.append(seg * lax.rsqrt(jnp.mean(seg * seg, axis=-1, keepdims=True) + EPS))
    return jnp.concatenate(parts, axis=1) * g


def _bias_kernel(tbl_ref, sink_ref, bkta_ref, bktb_ref, sba_ref, sbb_ref,
                 ba_ref, bb_ref, oa_ref, ob_ref, s0a_ref, s0b_ref, sk_ref, *, present):
    pa, pb, psa, psb = present
    lane = lax.broadcasted_iota(jnp.int32, (BLK, 128), 1)
    lane8 = lax.broadcasted_iota(jnp.int32, (8, 128), 1)
    for g in range(3):
        bk = bkta_ref[g]
        for h in range(H_A):
            ba_ref[g, h] = jnp.full((BLK, 2 * BLK), NEG, F32)
        for b in pa[g]:
            hit = bk == b
            for h in range(H_A):
                ba_ref[g, h] = jnp.where(hit, tbl_ref[b, h], ba_ref[g, h])
    bk = bktb_ref[...]
    for h in range(H_B):
        bb_ref[h] = jnp.full((BLK, 2 * BLK), NEG, F32)
    for b in pb:
        hit = bk == b
        for h in range(H_B):
            bb_ref[h] = jnp.where(hit, tbl_ref[b, H_A + h], bb_ref[h])
    for g in range(3):
        bk = sba_ref[g]
        acc = jnp.zeros((BLK, 128), F32)
        for b in psa[g]:
            hit = bk == b
            for h in range(H_A):
                acc = jnp.where(hit & (lane == h), tbl_ref[b, h], acc)
        oa_ref[g] = acc
    bk = sbb_ref[...]
    for gi in range(4):
        acc = jnp.zeros((BLK, 128), F32)
        for b in psb:
            hit = bk == b
            for ln, h in ((0, gi), (1, 4 + gi)):
                acc = jnp.where(hit & (lane == ln), tbl_ref[b, H_A + h], acc)
        ob_ref[gi] = jnp.where((bk < 0) & (lane < 2), NEG, acc)
    z8 = jnp.zeros((8, 128), F32)
    acc = z8
    for h in range(H_A):
        acc = jnp.where(lane8 == h, tbl_ref[0, h], acc)
    s0a_ref[...] = acc
    for gi in range(4):
        acc = z8
        sk = jnp.full((8, 128), NEG, F32)
        for ln, h in ((0, gi), (1, 4 + gi)):
            acc = jnp.where(lane8 == ln, tbl_ref[0, H_A + h], acc)
            sk = jnp.where(lane8 == ln, sink_ref[h], sk)
        s0b_ref[gi] = acc
        sk_ref[gi] = sk


def _bias_call(rel_bias, sinks):
    bkt_a, bkt_b, sb_a, sb_b = _bucket_tables()
    present = (tuple(tuple(int(b) for b in np.unique(bkt_a[g]) if b >= 0) for g in range(3)),
               tuple(int(b) for b in np.unique(bkt_b) if b >= 0),
               tuple(tuple(int(b) for b in np.unique(sb_a[g]) if b >= 0) for g in range(3)),
               tuple(int(b) for b in np.unique(sb_b) if b >= 0))
    smem = pl.BlockSpec(memory_space=pltpu.SMEM)
    vmem = pl.BlockSpec(memory_space=pltpu.VMEM)
    return pl.pallas_call(
        functools.partial(_bias_kernel, present=present),
        in_specs=[smem, smem, vmem, vmem, vmem, vmem],
        out_specs=[vmem] * 7,
        out_shape=[jax.ShapeDtypeStruct((3, H_A, BLK, 2 * BLK), F32),
                   jax.ShapeDtypeStruct((H_B, BLK, 2 * BLK), F32),
                   jax.ShapeDtypeStruct((3, BLK, 128), F32),
                   jax.ShapeDtypeStruct((4, BLK, 128), F32),
                   jax.ShapeDtypeStruct((8, 128), F32),
                   jax.ShapeDtypeStruct((4, 8, 128), F32),
                   jax.ShapeDtypeStruct((4, 8, 128), F32)],
        compiler_params=pltpu.CompilerParams(vmem_limit_bytes=VMEM_LIMIT),
        name="bias_expand",
    )(rel_bias, sinks, jnp.asarray(bkt_a), jnp.asarray(bkt_b), jnp.asarray(sb_a), jnp.asarray(sb_b))


def _qkv_kernel(x_ref, lng_ref, w_ref, g_ref, p_ref,
                qa_ref, ka_ref, va_ref, qb_ref, kb_ref, vb_ref, qc_ref):
    h = _rms_rows(x_ref[...], lng_ref[...]).astype(BF16)
    pmat = p_ref[...]

    def proj(c0, c1):
        return _dot(h, w_ref[:, c0:c1])

    qa_ref[...] = _headnorm64(proj(C_QA, C_KA), g_ref[0:1, :], pmat)
    ka_ref[...] = _headnorm64(proj(C_KA, C_VA), g_ref[1:2, :], pmat)
    va_ref[...] = proj(C_VA, C_ZA)
    qb_ref[...] = _headnorm64(proj(C_QB, C_KB), g_ref[2:3, :], pmat)
    kb_ref[...] = _headnorm64(proj(C_KB, C_VB), g_ref[3:4, 0:128], pmat)
    vb_ref[...] = proj(C_VB, C_ZB)
    qc_ref[...] = _headnorm128(proj(C_QC, C_ZC), g_ref[4:5, :])


def _qkv_call(x2, lng, w_bf, gains, pmat):
    n = x2.shape[0]
    tm = min(512, n)
    row = lambda w: pl.BlockSpec((tm, w), lambda i: (i, 0))
    widths = (512, 512, 512, 512, 128, 128, 512)
    return pl.pallas_call(
        _qkv_kernel,
        grid=(n // tm,),
        in_specs=[row(D_MODEL), _const_spec((1, D_MODEL)), _const_spec((D_MODEL, C_END)),
                  _const_spec((5, 512)), _const_spec((256, 256))],
        out_specs=[row(w) for w in widths],
        out_shape=[jax.ShapeDtypeStruct((n, w), F32) for w in widths],
        compiler_params=_cparams(("arbitrary",)),
        name="qkv_proj",
    )(x2, lng, w_bf, gains, pmat)


def _memkv_kernel(m_ref, g_ref, w_ref, gk_ref, mk_ref, mv_ref):
    h = _rms_rows(m_ref[...], g_ref[...]).astype(BF16)
    mk_ref[...] = _headnorm128(_dot(h, w_ref[:, 0:512]), gk_ref[...])
    mv_ref[...] = _dot(h, w_ref[:, 512:1024])


def _memkv_call(mem2, g, w_bf, gk):
    n = mem2.shape[0]
    tm = 512
    row = lambda w: pl.BlockSpec((tm, w), lambda i: (i, 0))
    return pl.pallas_call(
        _memkv_kernel,
        grid=(n // tm,),
        in_specs=[row(D_MODEL), _const_spec((1, D_MODEL)), _const_spec((D_MODEL, 2 * 512)),
                  _const_spec((1, 512))],
        out_specs=[row(512), row(512)],
        out_shape=[jax.ShapeDtypeStruct((n, 512), F32)] * 2,
        compiler_params=_cparams(("arbitrary",)),
        name="mem_kv_proj",
    )(mem2, g, w_bf, gk)


def _out_kernel(x_ref, oa_ref, ob_ref, oc_ref, lng_ref, w_ref, wbr_ref, wout_ref, y_ref):
    x = x_ref[...]
    h = _rms_rows(x, lng_ref[...]).astype(BF16)
    acc = None
    for br, (o_ref, cz, cg) in enumerate(((oa_ref, C_ZA, C_GA), (ob_ref, C_ZB, C_GB), (oc_ref, C_ZC, C_GC))):
        z = _dot(h, w_ref[:, cz:cz + 512])
        y = (o_ref[...] * (z * jax.nn.sigmoid(z))).astype(BF16)
        yb = _dot(y, wbr_ref[br])
        gate = jax.nn.sigmoid(_dot(h, w_ref[:, cg:cg + D_MODEL]))
        acc = gate * yb if acc is None else acc + gate * yb
    y_ref[...] = x + _dot(acc.astype(BF16), wout_ref[...])


def _out_call(x2, oa, ob, oc, lng, w_bf, wbr_bf, wout_bf):
    n = x2.shape[0]
    tm = min(256, n)
    row = lambda w: pl.BlockSpec((tm, w), lambda i: (i, 0))
    return pl.pallas_call(
        _out_kernel,
        grid=(n // tm,),
        in_specs=[row(D_MODEL), row(512), row(512), row(512), _const_spec((1, D_MODEL)),
                  _const_spec((D_MODEL, C_END)), _const_spec((3, 512, D_MODEL)),
                  _const_spec((D_MODEL, D_MODEL))],
        out_specs=row(D_MODEL),
        out_shape=jax.ShapeDtypeStruct((n, D_MODEL), F32),
        compiler_params=_cparams(("arbitrary",)),
        name="gated_out",
    )(x2, oa, ob, oc, lng, w_bf, wbr_bf, wout_bf)


def _pair_tile(qf, kf, vf, bias_of, scale, sinks=None):
    low = lax.broadcasted_iota(jnp.int32, (BLK, 128), 1) < 64
    kb = kf.astype(BF16)
    vb = vf.astype(BF16)
    us, ms, ls = [], [], []
    for hh in range(2):
        sel = low if hh == 0 else jnp.logical_not(low)
        qm = jnp.where(sel, qf, 0.0).astype(BF16)
        s = lax.dot_general(qm, kb, (((1,), (1,)), ((), ())), preferred_element_type=F32) * scale
        s = s + bias_of(hh)
        m = jnp.max(s, axis=-1, keepdims=True)
        if sinks is not None:
            m = jnp.maximum(m, sinks[hh])
        p = jnp.exp(s - m)
        l = jnp.sum(p, axis=-1, keepdims=True)
        if sinks is not None:
            l = l + jnp.exp(sinks[hh] - m)
        us.append(_dot(p.astype(BF16), vb))
        ms.append(m)
        ls.append(l)
    return (jnp.where(low, us[0], us[1]), jnp.where(low, ms[0], ms[1]), jnp.where(low, ls[0], ls[1]))


def _dil_kernel(q_ref, k_ref, v_ref, bias_ref, o_ref, au_ref, am_ref, al_ref):
    scale = HD_A ** -0.5
    seq = q_ref.shape[0]

    def put(rows, u, m, l, first):
        if first:
            au_ref[rows, :] = u
            am_ref[rows, :] = m
            al_ref[rows, :] = l
        else:
            mo = am_ref[rows, :]
            mn = jnp.maximum(mo, m)
            a = jnp.exp(mo - mn)
            b = jnp.exp(m - mn)
            au_ref[rows, :] = a * au_ref[rows, :] + b * u
            al_ref[rows, :] = a * al_ref[rows, :] + b * l
            am_ref[rows, :] = mn

    def first_block(g, rows, first):
        u, m, l = _pair_tile(q_ref[rows, :], k_ref[rows, :], v_ref[rows, :],
                             lambda hh: bias_ref[g, hh, :, BLK:2 * BLK], scale)
        put(rows, u, m, l, first)

    def later_block(g, qrows, krows, first):
        u, m, l = _pair_tile(q_ref[qrows, :], k_ref[krows, :], v_ref[krows, :],
                             lambda hh: bias_ref[g, hh], scale)
        put(qrows, u, m, l, first)

    first_block(0, pl.ds(0, BLK), True)

    def p0_body(n, c):
        r0 = pl.multiple_of(n * BLK, BLK)
        later_block(0, pl.ds(r0, BLK), pl.ds(r0 - BLK, 2 * BLK), True)
        return c

    lax.fori_loop(1, seq // BLK, p0_body, 0)

    for g in (1, 2):
        d = A_DILATIONS[g]
        nblk = seq // (d * BLK)

        def class_body(r, c, g=g, d=d, nblk=nblk):
            first_block(g, pl.ds(r, BLK, stride=d), False)

            def blk_body(n, cc):
                later_block(g, pl.ds(r + n * (d * BLK), BLK, stride=d),
                            pl.ds(r + (n - 1) * (d * BLK), 2 * BLK, stride=d), False)
                return cc

            if nblk > 1:
                lax.fori_loop(1, nblk, blk_body, 0)
            return c

        lax.fori_loop(0, d, class_body, 0)

    def fin_body(i, c):
        rows = pl.ds(pl.multiple_of(i * 256, 256), 256)
        o_ref[rows, :] = au_ref[rows, :] / al_ref[rows, :]
        return c

    lax.fori_loop(0, seq // 256, fin_body, 0)


def _dil_call(qa, ka, va, bias_a, batch, seq):
    blk = pl.BlockSpec((None, seq, 128), lambda b, p: (b, 0, p))
    return pl.pallas_call(
        _dil_kernel,
        grid=(batch, H_A // 2),
        in_specs=[blk, blk, blk, pl.BlockSpec((3, 2, BLK, 2 * BLK), lambda b, p: (0, p, 0, 0))],
        out_specs=blk,
        out_shape=jax.ShapeDtypeStruct((batch, seq, 512), F32),
        scratch_shapes=[pltpu.VMEM((seq, 128), F32)] * 3,
        compiler_params=_cparams(("arbitrary", "arbitrary")),
        name="dilated_attn",
    )(qa.reshape(batch, seq, 512), ka.reshape(batch, seq, 512), va.reshape(batch, seq, 512), bias_a)


def _swa_kernel(sink_ref, q_ref, k_ref, v_ref, bias_ref, o_ref):
    scale = HD_B ** -0.5
    seq = q_ref.shape[0]
    p = pl.program_id(1)
    kv = p // 2
    sinks = (sink_ref[2 * p], sink_ref[2 * p + 1])

    def dup(xf):
        half = (lax.broadcasted_iota(jnp.int32, xf.shape, 1) >= 64).astype(jnp.int32)
        return jnp.where(half == kv, xf, pltpu.roll(xf, 64, 1))

    def tile(qrows, krows, bias_of):
        u, m, l = _pair_tile(q_ref[qrows, :], dup(k_ref[krows, :]), dup(v_ref[krows, :]),
                             bias_of, scale, sinks)
        o_ref[qrows, :] = u / l

    tile(pl.ds(0, BLK), pl.ds(0, BLK), lambda hh: bias_ref[hh, :, BLK:2 * BLK])

    def body(n, c):
        r0 = pl.multiple_of(n * BLK, BLK)
        tile(pl.ds(r0, BLK), pl.ds(r0 - BLK, 2 * BLK), lambda hh: bias_ref[hh])
        return c

    lax.fori_loop(1, seq // BLK, body, 0)


def _swa_call(sinks, qb, kb, vb, bias_b, batch, seq):
    qblk = pl.BlockSpec((None, seq, 128), lambda b, p: (b, 0, p))
    kblk = pl.BlockSpec((None, seq, 128), lambda b, p: (b, 0, 0))
    return pl.pallas_call(
        _swa_kernel,
        grid=(batch, H_B // 2),
        in_specs=[pl.BlockSpec(memory_space=pltpu.SMEM), qblk, kblk, kblk,
                  pl.BlockSpec((2, BLK, 2 * BLK), lambda b, p: (p, 0, 0))],
        out_specs=qblk,
        out_shape=jax.ShapeDtypeStruct((batch, seq, 512), F32),
        compiler_params=_cparams(("arbitrary", "arbitrary")),
        name="swa_attn",
    )(sinks, qb.reshape(batch, seq, 512), kb.reshape(batch, seq, 128), vb.reshape(batch, seq, 128), bias_b)


def _mem_attn_kernel(q_ref, mk_ref, mv_ref, o_ref):
    scale = HD_C ** -0.5
    for h in range(H_C):
        cs = slice(h * HD_C, (h + 1) * HD_C)
        q = q_ref[:, cs].astype(BF16)
        s = lax.dot_general(q, mk_ref[:, cs].astype(BF16), (((1,), (1,)), ((), ())),
                            preferred_element_type=F32) * scale
        m = jnp.max(s, axis=-1, keepdims=True)
        p = jnp.exp(s - m)
        l = jnp.sum(p, axis=-1, keepdims=True)
        o_ref[:, cs] = _dot(p.astype(BF16), mv_ref[:, cs].astype(BF16)) / l


def _mem_attn_call(qc, mk, mv, batch, seq):
    tq = 512
    return pl.pallas_call(
        _mem_attn_kernel,
        grid=(batch, seq // tq),
        in_specs=[pl.BlockSpec((None, tq, 512), lambda b, i: (b, i, 0)),
                  pl.BlockSpec((None, N_MEM, 512), lambda b, i: (b, 0, 0)),
                  pl.BlockSpec((None, N_MEM, 512), lambda b, i: (b, 0, 0))],
        out_specs=pl.BlockSpec((None, tq, 512), lambda b, i: (b, i, 0)),
        out_shape=jax.ShapeDtypeStruct((batch, seq, 512), F32),
        compiler_params=_cparams(("arbitrary", "arbitrary")),
        name="mem_attn",
    )(qc.reshape(batch, seq, 512), mk.reshape(batch, N_MEM, 512), mv.reshape(batch, N_MEM, 512))


def _row8(x):
    return jnp.broadcast_to(x, (8, x.shape[1]))


def _samp_a_kernel(q_ref, kn_ref, vn_ref, k1_ref, k2_ref, k3_ref, v1_ref, v2_ref, v3_ref,
                   sb_ref, s0_ref, e_ref, et_ref, o_ref):
    scale = HD_A ** -0.5
    e = e_ref[...]
    et = et_ref[...]
    krefs = (k1_ref, k2_ref, k3_ref)
    vrefs = (v1_ref, v2_ref, v3_ref)

    def body(b, c):
        q = q_ref[pl.ds(b, 1), :]
        kn = kn_ref[pl.ds(b, 1), :]
        vn = vn_ref[pl.ds(b, 1), :]
        qrows = jnp.broadcast_to(q, (BLK, 512))
        lgs = [_dot2(krefs[g][b] * qrows, e) * scale + sb_ref[g] for g in range(3)]
        ln = _dot2(_row8(kn * q), e)[0:1, :] * scale + s0_ref[0:1, :]
        m = ln
        for lg in lgs:
            m = jnp.maximum(m, jnp.max(lg, axis=0, keepdims=True))
        pn = jnp.exp(ln - m)
        den = 3.0 * pn
        num = 3.0 * _dot2(_row8(pn), et)[0:1, :] * vn
        for g in range(3):
            p = jnp.exp(lgs[g] - m)
            den = den + jnp.sum(p, axis=0, keepdims=True)
            num = num + jnp.sum(_dot2(p, et) * vrefs[g][b], axis=0, keepdims=True)
        o_ref[pl.ds(b, 1), :] = num / _dot2(_row8(den), et)[0:1, :]
        return c

    lax.fori_loop(0, q_ref.shape[0], body, 0)


def _samp_a_call(qa, kn, vn, cache_k, cache_v, sb_a, s0_a):
    nb = qa.shape[0]
    bb = 8
    e, et = _seg_matrices(512, 64)
    views = []
    for c in (cache_k, cache_v):
        views += [c.reshape(nb, 2048, 512), c.reshape(nb, 512, 2048), c.reshape(nb, 128, 8192)]
    row = pl.BlockSpec((bb, 512), lambda i: (i, 0))
    kspecs = [pl.BlockSpec((bb, BLK, 512), lambda i: (i, 15, 0)),
              pl.BlockSpec((bb, BLK, 512), lambda i: (i, 3, 0)),
              pl.BlockSpec((bb, BLK, 512), lambda i: (i, 0, 0))]
    return pl.pallas_call(
        _samp_a_kernel,
        grid=(nb // bb,),
        in_specs=[row, row, row] + kspecs + kspecs
                 + [_const_spec((3, BLK, 128)), _const_spec((8, 128)),
                    _const_spec((512, 128)), _const_spec((128, 512))],
        out_specs=row,
        out_shape=jax.ShapeDtypeStruct((nb, 512), F32),
        compiler_params=_cparams(("arbitrary",)),
        name="sample_dilated_attn",
    )(qa, kn, vn, *views, sb_a, s0_a, e, et)


def _samp_b_kernel(q_ref, kn_ref, vn_ref, kc_ref, vc_ref, sb_ref, s0_ref, sk_ref, e_ref, et_ref, o_ref):
    scale = HD_B ** -0.5
    e = e_ref[...]
    et = et_ref[...]

    def body(b, c):
        q = q_ref[pl.ds(b, 1), :]
        kn = kn_ref[pl.ds(b, 1), :]
        vn = vn_ref[pl.ds(b, 1), :]
        kc = kc_ref[b]
        vc = vc_ref[b]
        outs = []
        for gi in range(4):
            qg = jnp.concatenate([q[:, 64 * gi:64 * gi + 64], q[:, 256 + 64 * gi:320 + 64 * gi]], axis=1)
            lg = _dot2(kc * jnp.broadcast_to(qg, (BLK, 128)), e) * scale + sb_ref[gi]
            ln = _dot2(_row8(kn * qg), e)[0:1, :] * scale + s0_ref[gi, 0:1, :]
            sk = sk_ref[gi, 0:1, :]
            m = jnp.maximum(jnp.maximum(jnp.max(lg, axis=0, keepdims=True), ln), sk)
            p = jnp.exp(lg - m)
            pn = jnp.exp(ln - m)
            den = jnp.sum(p, axis=0, keepdims=True) + pn + jnp.exp(sk - m)
            num = (jnp.sum(_dot2(p, et) * vc, axis=0, keepdims=True)
                   + _dot2(_row8(pn), et)[0:1, :] * vn)
            outs.append(num / _dot2(_row8(den), et)[0:1, :])
        o_ref[pl.ds(b, 1), :] = jnp.concatenate([o[:, :64] for o in outs] + [o[:, 64:] for o in outs], axis=1)
        return c

    lax.fori_loop(0, q_ref.shape[0], body, 0)


def _samp_b_call(qb, kn, vn, cache_k, cache_v, sb_b, s0_b, sk_b):
    nb = qb.shape[0]
    bb = 8
    e, et = _seg_matrices(128, 64)
    row = lambda w: pl.BlockSpec((bb, w), lambda i: (i, 0))
    cblk = pl.BlockSpec((bb, W_B, 128), lambda i: (i, 0, 0))
    return pl.pallas_call(
        _samp_b_kernel,
        grid=(nb // bb,),
        in_specs=[row(512), row(128), row(128), cblk, cblk, _const_spec((4, BLK, 128)),
                  _const_spec((4, 8, 128)), _const_spec((4, 8, 128)),
                  _const_spec((128, 128)), _const_spec((128, 128))],
        out_specs=row(512),
        out_shape=jax.ShapeDtypeStruct((nb, 512), F32),
        compiler_params=_cparams(("arbitrary",)),
        name="sample_swa_attn",
    )(qb, kn, vn, cache_k.reshape(nb, W_B, 128), cache_v.reshape(nb, W_B, 128), sb_b, s0_b, sk_b, e, et)


def _samp_c_kernel(q_ref, mk_ref, mv_ref, e_ref, et_ref, o_ref):
    scale = HD_C ** -0.5
    e = e_ref[...]
    et = et_ref[...]

    def body(b, c):
        q = q_ref[pl.ds(b, 1), :]
        lg = _dot2(mk_ref[b] * jnp.broadcast_to(q, (N_MEM, 512)), e) * scale
        m = jnp.max(lg, axis=0, keepdims=True)
        p = jnp.exp(lg - m)
        den = jnp.sum(p, axis=0, keepdims=True)
        num = jnp.sum(_dot2(p, et) * mv_ref[b], axis=0, keepdims=True)
        o_ref[pl.ds(b, 1), :] = num / _dot2(_row8(den), et)[0:1, :]
        return c

    lax.fori_loop(0, q_ref.shape[0], body, 0)


def _samp_c_call(qc, cache_k, cache_v):
    nb = qc.shape[0]
    bb = 8
    e, et = _seg_matrices(512, 128)
    row = pl.BlockSpec((bb, 512), lambda i: (i, 0))
    cblk = pl.BlockSpec((bb, N_MEM, 512), lambda i: (i, 0, 0))
    return pl.pallas_call(
        _samp_c_kernel,
        grid=(nb // bb,),
        in_specs=[row, cblk, cblk, _const_spec((512, 128)), _const_spec((128, 512))],
        out_specs=row,
        out_shape=jax.ShapeDtypeStruct((nb, 512), F32),
        compiler_params=_cparams(("arbitrary",)),
        name="sample_mem_attn",
    )(qc, cache_k.reshape(nb, N_MEM, 512), cache_v.reshape(nb, N_MEM, 512), e, et)


def kernel(x_prompt, x_sample, mem_prompt, cache_a_k, cache_a_v, cache_b_k, cache_b_v, cache_mem_k,
           cache_mem_v, rel_bias, ln_g, w_in, gq_a, gk_a, gq_b, gk_b, gq_c, gk_c, sinks_b, mem_ln_g,
           w_mem_kv, w_br_a, w_br_b, w_br_c, w_out):
    batch, seq, _ = x_prompt.shape
    nsamp = x_sample.shape[0]
    assert ln_g.shape[0] == 1 and x_sample.shape[1] == 1
    assert (batch, seq, nsamp) == (8, 2048, 128) and w_in.shape == (1, D_MODEL, C_END)
    assert cache_a_k.shape == (1, nsamp, 2048, H_A, HD_A) and cache_b_k.shape == (1, nsamp, W_B, KV_B, HD_B)

    w_bf = w_in[0].astype(BF16)
    wmem_bf = w_mem_kv[0].astype(BF16)
    wbr_bf = jnp.stack([w_br_a[0], w_br_b[0], w_br_c[0]]).astype(BF16)
    wout_bf = w_out[0].astype(BF16)
    lng = ln_g.reshape(1, D_MODEL)
    gains = jnp.stack([jnp.tile(gq_a[0], 8), jnp.tile(gk_a[0], 8), jnp.tile(gq_b[0], 8),
                       jnp.tile(gk_b[0], 8), jnp.tile(gq_c[0], 4)])
    gk_c4 = jnp.tile(gk_c[0], 4).reshape(1, 512)
    blockdiag = np.kron(np.eye(4, dtype=np.float32), np.ones((64, 64), np.float32))
    pmat = jnp.asarray(blockdiag, BF16)

    bias_a, bias_b, sb_a, sb_b, s0_a, s0_b, sk_b = _bias_call(rel_bias, sinks_b.reshape(H_B))

    xp2 = x_prompt.reshape(batch * seq, D_MODEL)
    qa, ka, va, qb, kb, vb, qc = _qkv_call(xp2, lng, w_bf, gains, pmat)
    mk, mv = _memkv_call(mem_prompt.reshape(batch * N_MEM, D_MODEL), mem_ln_g.reshape(1, D_MODEL),
                         wmem_bf, gk_c4)
    oa = _dil_call(qa, ka, va, bias_a, batch, seq)
    ob = _swa_call(sinks_b.reshape(H_B), qb, kb, vb, bias_b, batch, seq)
    oc = _mem_attn_call(qc, mk, mv, batch, seq)
    yp = _out_call(xp2, oa.reshape(-1, 512), ob.reshape(-1, 512), oc.reshape(-1, 512),
                   lng, w_bf, wbr_bf, wout_bf)

    xs2 = x_sample.reshape(nsamp, D_MODEL)
    sqa, ska, sva, sqb, skb, svb, sqc = _qkv_call(xs2, lng, w_bf, gains, pmat)
    soa = _samp_a_call(sqa, ska, sva, cache_a_k[0], cache_a_v[0], sb_a, s0_a)
    sob = _samp_b_call(sqb, skb, svb, cache_b_k[0], cache_b_v[0], sb_b, s0_b, sk_b)
    soc = _samp_c_call(sqc, cache_mem_k[0], cache_mem_v[0])
    ys = _out_call(xs2, soa, sob, soc, lng, w_bf, wbr_bf, wout_bf)

    return (yp.reshape(batch, seq, D_MODEL), ys.reshape(nsamp, 1, D_MODEL),
            ka.reshape(1, batch, seq, H_A, HD_A), va.reshape(1, batch, seq, H_A, HD_A),
            kb.reshape(batch, seq, KV_B, HD_B)[None, :, seq - W_B:],
            vb.reshape(batch, seq, KV_B, HD_B)[None, :, seq - W_B:],
            mk.reshape(1, batch, N_MEM, H_C, HD_C), mv.reshape(1, batch, N_MEM, H_C, HD_C),
            ska.reshape(1, nsamp, 1, H_A, HD_A), sva.reshape(1, nsamp, 1, H_A, HD_A),
            skb.reshape(1, nsamp, 1, KV_B, HD_B), svb.reshape(1, nsamp, 1, KV_B, HD_B))
```

```python
import functools
import math

import numpy as np
import jax
import jax.numpy as jnp
from jax import lax
from jax.experimental import pallas as pl
from jax.experimental.pallas import tpu as pltpu

F32 = jnp.float32
BF16 = jnp.bfloat16

D_MODEL = 1024
H_A, HD_A = 8, 64
A_DILATIONS = (1, 4, 16)
A_STEPS = 128
W_A = 2048
H_B, KV_B, HD_B, W_B = 8, 2, 64, 128
H_C, HD_C, N_MEM = 4, 128, 256
NUM_BUCKETS, MAX_DISTANCE = 32, 2048
EPS = 1e-6
NEG = -1e30
BLK = 128
UNROLL = 4

C_QA, C_KA, C_VA, C_ZA = 0, 512, 1024, 1536
C_QB, C_KB, C_VB, C_ZB = 2048, 2560, 2688, 2816
C_QC, C_ZC = 3328, 3840
C_GA, C_GB, C_GC, C_END = 4352, 5376, 6400, 7424

VMEM_LIMIT = 56 * 1024 * 1024


def _cparams(sem):
    return pltpu.CompilerParams(dimension_semantics=sem, vmem_limit_bytes=VMEM_LIMIT)


def _const_spec(shape):
    nd = len(shape)
    return pl.BlockSpec(shape, lambda *_: (0,) * nd, pipeline_mode=pl.Buffered(1))


def _t5_bucket_np(dist):
    n = np.maximum(dist, 0)
    max_exact = NUM_BUCKETS // 2
    nf = np.maximum(n, 1).astype(np.float32)
    large = max_exact + (np.log(nf / np.float32(max_exact))
                         / np.float32(math.log(MAX_DISTANCE / max_exact))
                         * np.float32(NUM_BUCKETS - max_exact)).astype(np.int32)
    return np.where(n < max_exact, n, np.minimum(large, NUM_BUCKETS - 1)).astype(np.int32)


def _a_multiplicity(dist):
    return sum(((dist % d == 0) & (dist <= A_STEPS * d)).astype(np.int32) for d in A_DILATIONS)


def _bucket_tables():
    i = np.arange(BLK)[:, None]
    j = np.arange(2 * BLK)[None, :]
    rel = i + BLK - j
    bkt_a = np.stack([np.where((rel >= 0) & (rel <= A_STEPS), _t5_bucket_np(rel * d), -1)
                      for d in A_DILATIONS]).astype(np.int32)
    bkt_b = np.where((rel >= 0) & (rel < W_B), _t5_bucket_np(rel), -1).astype(np.int32)
    dist_a = W_A - np.arange(W_A)
    mult_a = _a_multiplicity(dist_a)
    sb_a = np.broadcast_to(np.where(mult_a > 0, _t5_bucket_np(dist_a), -1), (8, W_A)).astype(np.int32)
    dist_b = W_B - np.arange(W_B)
    sb_b = np.broadcast_to(np.where(dist_b < W_B, _t5_bucket_np(dist_b), -1), (8, W_B)).astype(np.int32)
    return bkt_a, bkt_b, sb_a, sb_b, mult_a.astype(np.float32).reshape(1, W_A)


def _rms_rows(x, g):
    return x * lax.rsqrt(jnp.mean(x * x, axis=-1, keepdims=True) + EPS) * g


def _dot(a, b):
    return jnp.dot(a, b, preferred_element_type=F32)


def _dot_nt(a, b):
    return lax.dot_general(a, b, (((1,), (1,)), ((), ())), preferred_element_type=F32)


def _headnorm64(t, g, pmat):
    sq = t * t
    hi = sq.astype(BF16)
    lo = (sq - hi.astype(F32)).astype(BF16)
    width = t.shape[1]
    step = min(width, 256)
    pp = pmat[:step, :step]
    parts = [_dot(lo[:, c:c + step], pp) + _dot(hi[:, c:c + step], pp) for c in range(0, width, step)]
    ss = parts[0] if len(parts) == 1 else jnp.concatenate(parts, axis=1)
    return t * lax.rsqrt(ss * (1.0 / 64.0) + EPS) * g


def _headnorm128(t, g):
    parts = []
    for c in range(0, t.shape[1], 128):
        seg = t[:, c:c + 128]
        parts.append(seg * lax.rsqrt(jnp.mean(seg * seg, axis=-1, keepdims=True) + EPS))
    return jnp.concatenate(parts, axis=1) * g


def _bias_kernel(tbl_ref, sink_ref, bkta_ref, bktb_ref, sba_ref, sbb_ref,
                 ba_ref, bb_ref, oa_ref, ob_ref, s0a_ref, s0b_ref, sk_ref, *, present):
    pa, pb, psa, psb = present
    for g in range(3):
        bk = bkta_ref[g]
        for h in range(H_A):
            ba_ref[g, h] = jnp.full((BLK, 2 * BLK), NEG, F32)
        for b in pa[g]:
            hit = bk == b
            for h in range(H_A):
                ba_ref[g, h] = jnp.where(hit, tbl_ref[b, h], ba_ref[g, h])
    bk = bktb_ref[...]
    for h in range(H_B):
        bb_ref[h] = jnp.full((BLK, 2 * BLK), NEG, F32)
    for b in pb:
        hit = bk == b
        for h in range(H_B):
            bb_ref[h] = jnp.where(hit, tbl_ref[b, H_A + h], bb_ref[h])
    for (src, dst, buckets, col0) in ((sba_ref, oa_ref, psa, 0), (sbb_ref, ob_ref, psb, H_A)):
        bk = src[...]
        row = lax.broadcasted_iota(jnp.int32, bk.shape, 0)
        acc = jnp.full(bk.shape, NEG, F32)
        for b in buckets:
            hit = bk == b
            for h in range(8):
                acc = jnp.where(hit & (row == h), tbl_ref[b, col0 + h], acc)
        dst[...] = acc
    row8 = lax.broadcasted_iota(jnp.int32, (8, 128), 0)
    a0 = jnp.zeros((8, 128), F32)
    b0 = jnp.zeros((8, 128), F32)
    sk = jnp.zeros((8, 128), F32)
    for h in range(8):
        a0 = jnp.where(row8 == h, tbl_ref[0, h], a0)
        b0 = jnp.where(row8 == h, tbl_ref[0, H_A + h], b0)
        sk = jnp.where(row8 == h, sink_ref[h], sk)
    s0a_ref[...] = a0
    s0b_ref[...] = b0
    sk_ref[...] = sk


def _bias_call(rel_bias, sinks, tables):
    bkt_a, bkt_b, sb_a, sb_b = tables
    uniq = lambda a: tuple(int(b) for b in np.unique(a) if b >= 0)
    present = (tuple(uniq(bkt_a[g]) for g in range(3)), uniq(bkt_b), uniq(sb_a), uniq(sb_b))
    smem = pl.BlockSpec(memory_space=pltpu.SMEM)
    vmem = pl.BlockSpec(memory_space=pltpu.VMEM)
    return pl.pallas_call(
        functools.partial(_bias_kernel, present=present),
        in_specs=[smem, smem, vmem, vmem, vmem, vmem],
        out_specs=[vmem] * 7,
        out_shape=[jax.ShapeDtypeStruct((3, H_A, BLK, 2 * BLK), F32),
                   jax.ShapeDtypeStruct((H_B, BLK, 2 * BLK), F32),
                   jax.ShapeDtypeStruct((8, W_A), F32),
                   jax.ShapeDtypeStruct((8, W_B), F32),
                   jax.ShapeDtypeStruct((8, 128), F32),
                   jax.ShapeDtypeStruct((8, 128), F32),
                   jax.ShapeDtypeStruct((8, 128), F32)],
        compiler_params=pltpu.CompilerParams(vmem_limit_bytes=VMEM_LIMIT),
        name="bias_expand",
    )(rel_bias, sinks, jnp.asarray(bkt_a), jnp.asarray(bkt_b), jnp.asarray(sb_a), jnp.asarray(sb_b))


def _qkv_kernel(x_ref, lng_ref, w_ref, g_ref, p_ref,
                qa_ref, ka_ref, va_ref, qb_ref, kb_ref, vb_ref, qc_ref):
    h = _rms_rows(x_ref[...], lng_ref[...]).astype(BF16)
    pmat = p_ref[...]

    def proj(c0, c1):
        return _dot(h, w_ref[:, c0:c1])

    qa_ref[...] = _headnorm64(proj(C_QA, C_KA), g_ref[0:1, :], pmat)
    ka_ref[...] = _headnorm64(proj(C_KA, C_VA), g_ref[1:2, :], pmat)
    va_ref[...] = proj(C_VA, C_ZA)
    qb_ref[...] = _headnorm64(proj(C_QB, C_KB), g_ref[2:3, :], pmat)
    kb_ref[...] = _headnorm64(proj(C_KB, C_VB), g_ref[3:4, 0:128], pmat)
    vb_ref[...] = proj(C_VB, C_ZB)
    qc_ref[...] = _headnorm128(proj(C_QC, C_ZC), g_ref[4:5, :])


def _qkv_call(x2, lng, w_bf, gains, pmat):
    n = x2.shape[0]
    tm = min(512, n)
    row = lambda w: pl.BlockSpec((tm, w), lambda i: (i, 0))
    widths = (512, 512, 512, 512, 128, 128, 512)
    return pl.pallas_call(
        _qkv_kernel,
        grid=(n // tm,),
        in_specs=[row(D_MODEL), _const_spec((1, D_MODEL)), _const_spec((D_MODEL, C_END)),
                  _const_spec((5, 512)), _const_spec((256, 256))],
        out_specs=[row(w) for w in widths],
        out_shape=[jax.ShapeDtypeStruct((n, w), F32) for w in widths],
        compiler_params=_cparams(("arbitrary",)),
        name="qkv_proj",
    )(x2, lng, w_bf, gains, pmat)


def _memkv_kernel(m_ref, g_ref, w_ref, gk_ref, mk_ref, mv_ref):
    h = _rms_rows(m_ref[...], g_ref[...]).astype(BF16)
    mk_ref[...] = _headnorm128(_dot(h, w_ref[:, 0:512]), gk_ref[...])
    mv_ref[...] = _dot(h, w_ref[:, 512:1024])


def _memkv_call(mem2, g, w_bf, gk):
    n = mem2.shape[0]
    tm = 512
    row = lambda w: pl.BlockSpec((tm, w), lambda i: (i, 0))
    return pl.pallas_call(
        _memkv_kernel,
        grid=(n // tm,),
        in_specs=[row(D_MODEL), _const_spec((1, D_MODEL)), _const_spec((D_MODEL, 2 * 512)),
                  _const_spec((1, 512))],
        out_specs=[row(512), row(512)],
        out_shape=[jax.ShapeDtypeStruct((n, 512), F32)] * 2,
        compiler_params=_cparams(("arbitrary",)),
        name="mem_kv_proj",
    )(mem2, g, w_bf, gk)


def _out_kernel(x_ref, oa_ref, ob_ref, oc_ref, lng_ref, w_ref, wbr_ref, wout_ref, y_ref):
    x = x_ref[...]
    h = _rms_rows(x, lng_ref[...]).astype(BF16)
    acc = None
    for br, (o_ref, cz, cg) in enumerate(((oa_ref, C_ZA, C_GA), (ob_ref, C_ZB, C_GB), (oc_ref, C_ZC, C_GC))):
        z = _dot(h, w_ref[:, cz:cz + 512])
        y = (o_ref[...] * (z * jax.nn.sigmoid(z))).astype(BF16)
        yb = _dot(y, wbr_ref[br])
        gate = jax.nn.sigmoid(_dot(h, w_ref[:, cg:cg + D_MODEL]))
        acc = gate * yb if acc is None else acc + gate * yb
    y_ref[...] = x + _dot(acc.astype(BF16), wout_ref[...])


def _out_call(x2, oa, ob, oc, lng, w_bf, wbr_bf, wout_bf):
    n = x2.shape[0]
    tm = min(256, n)
    row = lambda w: pl.BlockSpec((tm, w), lambda i: (i, 0))
    return pl.pallas_call(
        _out_kernel,
        grid=(n // tm,),
        in_specs=[row(D_MODEL), row(512), row(512), row(512), _const_spec((1, D_MODEL)),
                  _const_spec((D_MODEL, C_END)), _const_spec((3, 512, D_MODEL)),
                  _const_spec((D_MODEL, D_MODEL))],
        out_specs=row(D_MODEL),
        out_shape=jax.ShapeDtypeStruct((n, D_MODEL), F32),
        compiler_params=_cparams(("arbitrary",)),
        name="gated_out",
    )(x2, oa, ob, oc, lng, w_bf, wbr_bf, wout_bf)


def _pair_tile(qf, kf, vf, bias_of, scale, sinks=None):
    low = lax.broadcasted_iota(jnp.int32, (BLK, 128), 1) < 64
    kb = kf.astype(BF16)
    vb = vf.astype(BF16)
    us, ms, ls = [], [], []
    for hh in range(2):
        sel = low if hh == 0 else jnp.logical_not(low)
        qm = jnp.where(sel, qf, 0.0).astype(BF16)
        s = _dot_nt(qm, kb) * scale + bias_of(hh)
        m = jnp.max(s, axis=-1, keepdims=True)
        if sinks is not None:
            m = jnp.maximum(m, sinks[hh])
        p = jnp.exp(s - m)
        l = jnp.sum(p, axis=-1, keepdims=True)
        if sinks is not None:
            l = l + jnp.exp(sinks[hh] - m)
        us.append(_dot(p.astype(BF16), vb))
        ms.append(m)
        ls.append(l)
    return (jnp.where(low, us[0], us[1]), jnp.where(low, ms[0], ms[1]), jnp.where(low, ls[0], ls[1]))


def _dil_kernel(q_ref, k_ref, v_ref, bias_ref, o_ref, au_ref, am_ref, al_ref):
    scale = HD_A ** -0.5
    seq = q_ref.shape[0]

    def tile(g, n, qrows, krows):
        if n == 0:
            bias_of = lambda hh: bias_ref[g, hh, :, BLK:2 * BLK]
        else:
            bias_of = lambda hh: bias_ref[g, hh]
        return _pair_tile(q_ref[qrows, :], k_ref[krows, :], v_ref[krows, :], bias_of, scale)

    def assign(rows, uml):
        au_ref[rows, :], am_ref[rows, :], al_ref[rows, :] = uml

    def merged(rows, uml):
        u, m, l = uml
        mo = am_ref[rows, :]
        mn = jnp.maximum(mo, m)
        a = jnp.exp(mo - mn)
        b = jnp.exp(m - mn)
        return a * au_ref[rows, :] + b * u, mn, a * al_ref[rows, :] + b * l

    first = [tile(0, n, pl.ds(n * BLK, BLK), pl.ds(max(n - 1, 0) * BLK, BLK if n == 0 else 2 * BLK))
             for n in range(UNROLL)]
    for n in range(UNROLL):
        assign(pl.ds(n * BLK, BLK), first[n])

    def p0_body(i, c):
        base = pl.multiple_of(i * (UNROLL * BLK), UNROLL * BLK)
        res = [tile(0, 1, pl.ds(base + j * BLK, BLK), pl.ds(base + (j - 1) * BLK, 2 * BLK))
               for j in range(UNROLL)]
        for j in range(UNROLL):
            assign(pl.ds(base + j * BLK, BLK), res[j])
        return c

    lax.fori_loop(1, seq // (UNROLL * BLK), p0_body, 0)

    d1 = A_DILATIONS[1]
    nblk1 = seq // (d1 * BLK)

    def p1_body(r, c):
        rows = [pl.ds(r + n * (d1 * BLK), BLK, stride=d1) for n in range(nblk1)]
        res = [tile(1, n, rows[n],
                    rows[0] if n == 0 else pl.ds(r + (n - 1) * (d1 * BLK), 2 * BLK, stride=d1))
               for n in range(nblk1)]
        new = [merged(rows[n], res[n]) for n in range(nblk1)]
        for n in range(nblk1):
            assign(rows[n], new[n])
        return c

    lax.fori_loop(0, d1, p1_body, 0)

    d2 = A_DILATIONS[2]

    def p2_body(i, c):
        rows = [pl.ds(i * UNROLL + j, BLK, stride=d2) for j in range(UNROLL)]
        res = [tile(2, 0, rows[j], rows[j]) for j in range(UNROLL)]
        new = [merged(rows[j], res[j]) for j in range(UNROLL)]
        for j in range(UNROLL):
            o_ref[rows[j], :] = new[j][0] / new[j][2]
        return c

    lax.fori_loop(0, d2 // UNROLL, p2_body, 0)


def _dil_call(qa, ka, va, bias_a, batch, seq):
    assert seq == A_DILATIONS[2] * BLK
    blk = pl.BlockSpec((None, seq, 128), lambda b, p: (b, 0, p))
    return pl.pallas_call(
        _dil_kernel,
        grid=(batch, H_A // 2),
        in_specs=[blk, blk, blk, pl.BlockSpec((3, 2, BLK, 2 * BLK), lambda b, p: (0, p, 0, 0))],
        out_specs=blk,
        out_shape=jax.ShapeDtypeStruct((batch, seq, 512), F32),
        scratch_shapes=[pltpu.VMEM((seq, 128), F32)] * 3,
        compiler_params=_cparams(("arbitrary", "arbitrary")),
        name="dilated_attn",
    )(qa.reshape(batch, seq, 512), ka.reshape(batch, seq, 512), va.reshape(batch, seq, 512), bias_a)


def _swa_kernel(sink_ref, q_ref, k_ref, v_ref, bias_ref, o_ref):
    scale = HD_B ** -0.5
    seq = q_ref.shape[0]
    p = pl.program_id(1)
    kv = p // 2
    sinks = (sink_ref[2 * p], sink_ref[2 * p + 1])

    def dup(xf):
        half = (lax.broadcasted_iota(jnp.int32, xf.shape, 1) >= 64).astype(jnp.int32)
        return jnp.where(half == kv, xf, pltpu.roll(xf, 64, 1))

    def tile(first, qrows, krows):
        bias_of = (lambda hh: bias_ref[hh, :, BLK:2 * BLK]) if first else (lambda hh: bias_ref[hh])
        u, m, l = _pair_tile(q_ref[qrows, :], dup(k_ref[krows, :]), dup(v_ref[krows, :]),
                             bias_of, scale, sinks)
        return u / l

    first = [tile(n == 0, pl.ds(n * BLK, BLK), pl.ds(max(n - 1, 0) * BLK, BLK if n == 0 else 2 * BLK))
             for n in range(UNROLL)]
    for n in range(UNROLL):
        o_ref[pl.ds(n * BLK, BLK), :] = first[n]

    def body(i, c):
        base = pl.multiple_of(i * (UNROLL * BLK), UNROLL * BLK)
        res = [tile(False, pl.ds(base + j * BLK, BLK), pl.ds(base + (j - 1) * BLK, 2 * BLK))
               for j in range(UNROLL)]
        for j in range(UNROLL):
            o_ref[pl.ds(base + j * BLK, BLK), :] = res[j]
        return c

    lax.fori_loop(1, seq // (UNROLL * BLK), body, 0)


def _swa_call(sinks, qb, kb, vb, bias_b, batch, seq):
    qblk = pl.BlockSpec((None, seq, 128), lambda b, p: (b, 0, p))
    kblk = pl.BlockSpec((None, seq, 128), lambda b, p: (b, 0, 0))
    return pl.pallas_call(
        _swa_kernel,
        grid=(batch, H_B // 2),
        in_specs=[pl.BlockSpec(memory_space=pltpu.SMEM), qblk, kblk, kblk,
                  pl.BlockSpec((2, BLK, 2 * BLK), lambda b, p: (p, 0, 0))],
        out_specs=qblk,
        out_shape=jax.ShapeDtypeStruct((batch, seq, 512), F32),
        compiler_params=_cparams(("arbitrary", "arbitrary")),
        name="swa_attn",
    )(sinks, qb.reshape(batch, seq, 512), kb.reshape(batch, seq, 128), vb.reshape(batch, seq, 128), bias_b)


def _mem_attn_kernel(q_ref, mk_ref, mv_ref, o_ref):
    scale = HD_C ** -0.5
    for h in range(H_C):
        cs = slice(h * HD_C, (h + 1) * HD_C)
        s = _dot_nt(q_ref[:, cs].astype(BF16), mk_ref[:, cs].astype(BF16)) * scale
        m = jnp.max(s, axis=-1, keepdims=True)
        p = jnp.exp(s - m)
        l = jnp.sum(p, axis=-1, keepdims=True)
        o_ref[:, cs] = _dot(p.astype(BF16), mv_ref[:, cs].astype(BF16)) / l


def _mem_attn_call(qc, mk, mv, batch, seq):
    tq = 512
    return pl.pallas_call(
        _mem_attn_kernel,
        grid=(batch, seq // tq),
        in_specs=[pl.BlockSpec((None, tq, 512), lambda b, i: (b, i, 0)),
                  pl.BlockSpec((None, N_MEM, 512), lambda b, i: (b, 0, 0)),
                  pl.BlockSpec((None, N_MEM, 512), lambda b, i: (b, 0, 0))],
        out_specs=pl.BlockSpec((None, tq, 512), lambda b, i: (b, i, 0)),
        out_shape=jax.ShapeDtypeStruct((batch, seq, 512), F32),
        compiler_params=_cparams(("arbitrary", "arbitrary")),
        name="mem_attn",
    )(qc.reshape(batch, seq, 512), mk.reshape(batch, N_MEM, 512), mv.reshape(batch, N_MEM, 512))


def _head_rows(row, width, head_dim):
    rid = lax.broadcasted_iota(jnp.int32, (8, width), 0)
    hid = lax.broadcasted_iota(jnp.int32, (8, width), 1) // head_dim
    return rid == hid, jnp.where(rid == hid, jnp.broadcast_to(row, (8, width)), 0.0)


def _samp_a_kernel(q_ref, kn_ref, vn_ref, kt_ref, vt_ref, bias_ref, mult_ref, b0_ref, o_ref):
    scale = HD_A ** -0.5
    q = q_ref[...]
    own, qh = _head_rows(q, 512, HD_A)
    s = _dot(qh.astype(BF16), kt_ref[...].astype(BF16)) * scale + bias_ref[...]
    ln = jnp.sum(qh * kn_ref[...], axis=-1, keepdims=True) * scale + b0_ref[:, 0:1]
    m = jnp.maximum(jnp.max(s, axis=-1, keepdims=True), ln)
    p = jnp.exp(s - m) * mult_ref[...]
    pn = len(A_DILATIONS) * jnp.exp(ln - m)
    den = jnp.sum(p, axis=-1, keepdims=True) + pn
    r = _dot_nt(p.astype(BF16), vt_ref[...].astype(BF16)) + pn * vn_ref[...]
    o_ref[...] = jnp.sum(jnp.where(own, r / den, 0.0), axis=0, keepdims=True)


def _samp_a_call(qa, kn, vn, cache_k, cache_v, bias_sa, mult, s0a):
    nb = qa.shape[0]
    kt = jnp.transpose(cache_k, (0, 2, 3, 1)).reshape(nb, 512, W_A)
    vt = jnp.transpose(cache_v, (0, 2, 3, 1)).reshape(nb, 512, W_A)
    row = pl.BlockSpec((None, 1, 512), lambda i: (i, 0, 0))
    cblk = pl.BlockSpec((None, 512, W_A), lambda i: (i, 0, 0))
    out = pl.pallas_call(
        _samp_a_kernel,
        grid=(nb,),
        in_specs=[row, row, row, cblk, cblk, _const_spec((8, W_A)), _const_spec((1, W_A)),
                  _const_spec((8, 128))],
        out_specs=row,
        out_shape=jax.ShapeDtypeStruct((nb, 1, 512), F32),
        compiler_params=_cparams(("arbitrary",)),
        name="sample_dilated_attn",
    )(qa.reshape(nb, 1, 512), kn.reshape(nb, 1, 512), vn.reshape(nb, 1, 512), kt, vt, bias_sa, mult, s0a)
    return out.reshape(nb, 512)


def _samp_b_kernel(q_ref, kn_ref, vn_ref, kt_ref, vt_ref, bias_ref, b0_ref, sk_ref, o_ref):
    scale = HD_B ** -0.5
    low = lax.broadcasted_iota(jnp.int32, (1, 128), 1) < 64
    rid = lax.broadcasted_iota(jnp.int32, (8, 128), 0)
    kv_half = (lax.broadcasted_iota(jnp.int32, (8, 128), 1) >= 64) == (rid >= 4)

    def body(b, c):
        q = q_ref[pl.ds(b, 1), :]
        kn = kn_ref[pl.ds(b, 1), :]
        vn = vn_ref[pl.ds(b, 1), :]
        rows = []
        for h in range(H_B):
            chunk = q[:, 128 * (h // 2):128 * (h // 2) + 128]
            if (h % 2) != (h // 4):
                chunk = pltpu.roll(chunk, 64, 1)
            rows.append(chunk)
        qh = jnp.where(kv_half, jnp.concatenate(rows, axis=0), 0.0)
        s = _dot(qh.astype(BF16), kt_ref[b].astype(BF16)) * scale + bias_ref[...]
        ln = jnp.sum(qh * kn, axis=-1, keepdims=True) * scale + b0_ref[:, 0:1]
        sk = sk_ref[:, 0:1]
        m = jnp.maximum(jnp.maximum(jnp.max(s, axis=-1, keepdims=True), ln), sk)
        p = jnp.exp(s - m)
        pn = jnp.exp(ln - m)
        den = jnp.sum(p, axis=-1, keepdims=True) + pn + jnp.exp(sk - m)
        r = (_dot_nt(p.astype(BF16), vt_ref[b].astype(BF16)) + pn * vn) / den
        r = jnp.where(kv_half, r, 0.0)
        chunks = []
        for c2 in range(4):
            pair = []
            for h in (2 * c2, 2 * c2 + 1):
                piece = r[h:h + 1, :]
                if (h % 2) != (h // 4):
                    piece = pltpu.roll(piece, 64, 1)
                pair.append(piece)
            chunks.append(jnp.where(low, pair[0], pair[1]))
        o_ref[pl.ds(b, 1), :] = jnp.concatenate(chunks, axis=1)
        return c

    lax.fori_loop(0, q_ref.shape[0], body, 0)


def _samp_b_call(qb, kn, vn, cache_k, cache_v, bias_sb, s0b, skb):
    nb = qb.shape[0]
    bb = 8
    kt = jnp.transpose(cache_k, (0, 2, 3, 1)).reshape(nb, 128, W_B)
    vt = jnp.transpose(cache_v, (0, 2, 3, 1)).reshape(nb, 128, W_B)
    row = lambda w: pl.BlockSpec((bb, w), lambda i: (i, 0))
    cblk = pl.BlockSpec((bb, 128, W_B), lambda i: (i, 0, 0))
    return pl.pallas_call(
        _samp_b_kernel,
        grid=(nb // bb,),
        in_specs=[row(512), row(128), row(128), cblk, cblk, _const_spec((8, W_B)),
                  _const_spec((8, 128)), _const_spec((8, 128))],
        out_specs=row(512),
        out_shape=jax.ShapeDtypeStruct((nb, 512), F32),
        compiler_params=_cparams(("arbitrary",)),
        name="sample_swa_attn",
    )(qb, kn, vn, kt, vt, bias_sb, s0b, skb)


def _samp_c_kernel(q_ref, mk_ref, mv_ref, mask_ref, o_ref):
    scale = HD_C ** -0.5
    zeros = jnp.zeros((4, HD_C), F32)

    def body(b, c):
        q8 = jnp.concatenate([q_ref[b], zeros], axis=0).astype(BF16)
        s = _dot_nt(q8, mk_ref[b].astype(BF16)) * scale + mask_ref[...]
        m = jnp.max(s, axis=-1, keepdims=True)
        p = jnp.exp(s - m)
        den = jnp.sum(p, axis=-1, keepdims=True)
        r = _dot(p.astype(BF16), mv_ref[b].astype(BF16)) / den
        o_ref[b] = r[0:4, :]
        return c

    lax.fori_loop(0, q_ref.shape[0], body, 0)


def _samp_c_call(qc, cache_k, cache_v):
    nb = qc.shape[0]
    bb = 8
    rows = N_MEM * H_C
    head_of_row = np.arange(rows)[None, :] % H_C
    mask = np.where(head_of_row == np.arange(8)[:, None], 0.0, NEG).astype(np.float32)
    qblk = pl.BlockSpec((bb, H_C, HD_C), lambda i: (i, 0, 0))
    cblk = pl.BlockSpec((bb, rows, HD_C), lambda i: (i, 0, 0))
    out = pl.pallas_call(
        _samp_c_kernel,
        grid=(nb // bb,),
        in_specs=[qblk, cblk, cblk, _const_spec((8, rows))],
        out_specs=qblk,
        out_shape=jax.ShapeDtypeStruct((nb, H_C, HD_C), F32),
        compiler_params=_cparams(("arbitrary",)),
        name="sample_mem_attn",
    )(qc.reshape(nb, H_C, HD_C), cache_k.reshape(nb, rows, HD_C), cache_v.reshape(nb, rows, HD_C),
      jnp.asarray(mask))
    return out.reshape(nb, 512)


def kernel(x_prompt, x_sample, mem_prompt, cache_a_k, cache_a_v, cache_b_k, cache_b_v, cache_mem_k,
           cache_mem_v, rel_bias, ln_g, w_in, gq_a, gk_a, gq_b, gk_b, gq_c, gk_c, sinks_b, mem_ln_g,
           w_mem_kv, w_br_a, w_br_b, w_br_c, w_out):
    batch, seq, _ = x_prompt.shape
    nsamp = x_sample.shape[0]
    assert ln_g.shape[0] == 1 and x_sample.shape[1] == 1
    assert (batch, seq, nsamp) == (8, 2048, 128) and w_in.shape == (1, D_MODEL, C_END)
    assert cache_a_k.shape == (1, nsamp, W_A, H_A, HD_A) and cache_b_k.shape == (1, nsamp, W_B, KV_B, HD_B)

    w_bf = w_in[0].astype(BF16)
    wmem_bf = w_mem_kv[0].astype(BF16)
    wbr_bf = jnp.stack([w_br_a[0], w_br_b[0], w_br_c[0]]).astype(BF16)
    wout_bf = w_out[0].astype(BF16)
    lng = ln_g.reshape(1, D_MODEL)
    gains = jnp.stack([jnp.tile(gq_a[0], 8), jnp.tile(gk_a[0], 8), jnp.tile(gq_b[0], 8),
                       jnp.tile(gk_b[0], 8), jnp.tile(gq_c[0], 4)])
    gk_c4 = jnp.tile(gk_c[0], 4).reshape(1, 512)
    blockdiag = np.kron(np.eye(4, dtype=np.float32), np.ones((64, 64), np.float32))
    pmat = jnp.asarray(blockdiag, BF16)

    *tables, mult_a = _bucket_tables()
    bias_a, bias_b, bias_sa, bias_sb, s0a, s0b, skb = _bias_call(rel_bias, sinks_b.reshape(H_B), tables)

    xp2 = x_prompt.reshape(batch * seq, D_MODEL)
    qa, ka, va, qb, kb, vb, qc = _qkv_call(xp2, lng, w_bf, gains, pmat)
    mk, mv = _memkv_call(mem_prompt.reshape(batch * N_MEM, D_MODEL), mem_ln_g.reshape(1, D_MODEL),
                         wmem_bf, gk_c4)
    oa = _dil_call(qa, ka, va, bias_a, batch, seq)
    ob = _swa_call(sinks_b.reshape(H_B), qb, kb, vb, bias_b, batch, seq)
    oc = _mem_attn_call(qc, mk, mv, batch, seq)
    yp = _out_call(xp2, oa.reshape(-1, 512), ob.reshape(-1, 512), oc.reshape(-1, 512),
                   lng, w_bf, wbr_bf, wout_bf)

    xs2 = x_sample.reshape(nsamp, D_MODEL)
    sqa, ska, sva, sqb, skb_new, svb, sqc = _qkv_call(xs2, lng, w_bf, gains, pmat)
    soa = _samp_a_call(sqa, ska, sva, cache_a_k[0], cache_a_v[0], bias_sa, jnp.asarray(mult_a), s0a)
    sob = _samp_b_call(sqb, skb_new, svb, cache_b_k[0], cache_b_v[0], bias_sb, s0b, skb)
    soc = _samp_c_call(sqc, cache_mem_k[0], cache_mem_v[0])
    ys = _out_call(xs2, soa, sob, soc, lng, w_bf, wbr_bf, wout_bf)

    return (yp.reshape(batch, seq, D_MODEL), ys.reshape(nsamp, 1, D_MODEL),
            ka.reshape(1, batch, seq, H_A, HD_A), va.reshape(1, batch, seq, H_A, HD_A),
            kb.reshape(batch, seq, KV_B, HD_B)[None, :, seq - W_B:],
            vb.reshape(batch, seq, KV_B, HD_B)[None, :, seq - W_B:],
            mk.reshape(1, batch, N_MEM, H_C, HD_C), mv.reshape(1, batch, N_MEM, H_C, HD_C),
            ska.reshape(1, nsamp, 1, H_A, HD_A), sva.reshape(1, nsamp, 1, H_A, HD_A),
            skb_new.reshape(1, nsamp, 1, KV_B, HD_B), svb.reshape(1, nsamp, 1, KV_B, HD_B))
```

```python
import functools
import math

import numpy as np
import jax
import jax.numpy as jnp
from jax import lax
from jax.experimental import pallas as pl
from jax.experimental.pallas import tpu as pltpu

F32 = jnp.float32
BF16 = jnp.bfloat16

D_MODEL = 1024
H_A, HD_A = 8, 64
A_DILATIONS = (1, 4, 16)
A_STEPS = 128
W_A = 2048
H_B, KV_B, HD_B, W_B = 8, 2, 64, 128
H_C, HD_C, N_MEM = 4, 128, 256
NUM_BUCKETS, MAX_DISTANCE = 32, 2048
EPS = 1e-6
NEG = -1e30
BLK = 128
UNROLL = 8

C_QA, C_KA, C_VA, C_ZA = 0, 512, 1024, 1536
C_QB, C_KB, C_VB, C_ZB = 2048, 2560, 2688, 2816
C_QC, C_ZC = 3328, 3840
C_GA, C_GB, C_GC, C_END = 4352, 5376, 6400, 7424

VMEM_LIMIT = 56 * 1024 * 1024


def _cparams(sem):
    return pltpu.CompilerParams(dimension_semantics=sem, vmem_limit_bytes=VMEM_LIMIT)


def _const_spec(shape):
    nd = len(shape)
    return pl.BlockSpec(shape, lambda *_: (0,) * nd, pipeline_mode=pl.Buffered(1))


def _t5_bucket_np(dist):
    n = np.maximum(dist, 0)
    max_exact = NUM_BUCKETS // 2
    nf = np.maximum(n, 1).astype(np.float32)
    large = max_exact + (np.log(nf / np.float32(max_exact))
                         / np.float32(math.log(MAX_DISTANCE / max_exact))
                         * np.float32(NUM_BUCKETS - max_exact)).astype(np.int32)
    return np.where(n < max_exact, n, np.minimum(large, NUM_BUCKETS - 1)).astype(np.int32)


def _a_multiplicity(dist):
    return sum(((dist % d == 0) & (dist <= A_STEPS * d)).astype(np.int32) for d in A_DILATIONS)


def _bucket_tables():
    i = np.arange(BLK)[:, None]
    j = np.arange(2 * BLK)[None, :]
    rel = i + BLK - j
    bkt_a = np.stack([np.where((rel >= 0) & (rel <= A_STEPS), _t5_bucket_np(rel * d), -1)
                      for d in A_DILATIONS]).astype(np.int32)
    bkt_b = np.where((rel >= 0) & (rel < W_B), _t5_bucket_np(rel), -1).astype(np.int32)
    dist_a = W_A - np.arange(W_A)
    mult_a = _a_multiplicity(dist_a)
    sb_a = np.broadcast_to(np.where(mult_a > 0, _t5_bucket_np(dist_a), -1), (8, W_A)).astype(np.int32)
    dist_b = W_B - np.arange(W_B)
    sb_b = np.broadcast_to(np.where(dist_b < W_B, _t5_bucket_np(dist_b), -1), (8, W_B)).astype(np.int32)
    return bkt_a, bkt_b, sb_a, sb_b, mult_a.astype(np.float32).reshape(1, W_A)


def _rms_rows(x, g):
    return x * lax.rsqrt(jnp.mean(x * x, axis=-1, keepdims=True) + EPS) * g


def _dot(a, b):
    return jnp.dot(a, b, preferred_element_type=F32)


def _dot_nt(a, b):
    return lax.dot_general(a, b, (((1,), (1,)), ((), ())), preferred_element_type=F32)


def _headnorm64(t, g, pmat):
    sq = (t * t).astype(BF16)
    width = t.shape[1]
    step = min(width, 256)
    pp = pmat[:step, :step]
    parts = [_dot(sq[:, c:c + step], pp) for c in range(0, width, step)]
    ss = parts[0] if len(parts) == 1 else jnp.concatenate(parts, axis=1)
    return t * lax.rsqrt(ss * (1.0 / 64.0) + EPS) * g


def _headnorm128(t, g):
    parts = []
    for c in range(0, t.shape[1], 128):
        seg = t[:, c:c + 128]
        parts.append(seg * lax.rsqrt(jnp.mean(seg * seg, axis=-1, keepdims=True) + EPS))
    return jnp.concatenate(parts, axis=1) * g


def _bias_kernel(tbl_ref, sink_ref, bkta_ref, bktb_ref, sba_ref, sbb_ref,
                 ba_ref, bb_ref, oa_ref, ob_ref, s0a_ref, s0b_ref, sk_ref, *, present):
    pa, pb, psa, psb = present
    for g in range(3):
        bk = bkta_ref[g]
        for h in range(H_A):
            ba_ref[g, h] = jnp.full((BLK, 2 * BLK), NEG, F32)
        for b in pa[g]:
            hit = bk == b
            for h in range(H_A):
                ba_ref[g, h] = jnp.where(hit, tbl_ref[b, h], ba_ref[g, h])
    bk = bktb_ref[...]
    for h in range(H_B):
        bb_ref[h] = jnp.full((BLK, 2 * BLK), NEG, F32)
    for b in pb:
        hit = bk == b
        for h in range(H_B):
            bb_ref[h] = jnp.where(hit, tbl_ref[b, H_A + h], bb_ref[h])
    for (src, dst, buckets, col0) in ((sba_ref, oa_ref, psa, 0), (sbb_ref, ob_ref, psb, H_A)):
        bk = src[...]
        row = lax.broadcasted_iota(jnp.int32, bk.shape, 0)
        acc = jnp.full(bk.shape, NEG, F32)
        for b in buckets:
            hit = bk == b
            for h in range(8):
                acc = jnp.where(hit & (row == h), tbl_ref[b, col0 + h], acc)
        dst[...] = acc
    row8 = lax.broadcasted_iota(jnp.int32, (8, 128), 0)
    a0 = jnp.zeros((8, 128), F32)
    b0 = jnp.zeros((8, 128), F32)
    sk = jnp.zeros((8, 128), F32)
    for h in range(8):
        a0 = jnp.where(row8 == h, tbl_ref[0, h], a0)
        b0 = jnp.where(row8 == h, tbl_ref[0, H_A + h], b0)
        sk = jnp.where(row8 == h, sink_ref[h], sk)
    s0a_ref[...] = a0
    s0b_ref[...] = b0
    sk_ref[...] = sk


def _bias_call(rel_bias, sinks, tables):
    bkt_a, bkt_b, sb_a, sb_b = tables
    uniq = lambda a: tuple(int(b) for b in np.unique(a) if b >= 0)
    present = (tuple(uniq(bkt_a[g]) for g in range(3)), uniq(bkt_b), uniq(sb_a), uniq(sb_b))
    smem = pl.BlockSpec(memory_space=pltpu.SMEM)
    vmem = pl.BlockSpec(memory_space=pltpu.VMEM)
    return pl.pallas_call(
        functools.partial(_bias_kernel, present=present),
        in_specs=[smem, smem, vmem, vmem, vmem, vmem],
        out_specs=[vmem] * 7,
        out_shape=[jax.ShapeDtypeStruct((3, H_A, BLK, 2 * BLK), F32),
                   jax.ShapeDtypeStruct((H_B, BLK, 2 * BLK), F32),
                   jax.ShapeDtypeStruct((8, W_A), F32),
                   jax.ShapeDtypeStruct((8, W_B), F32),
                   jax.ShapeDtypeStruct((8, 128), F32),
                   jax.ShapeDtypeStruct((8, 128), F32),
                   jax.ShapeDtypeStruct((8, 128), F32)],
        compiler_params=pltpu.CompilerParams(vmem_limit_bytes=VMEM_LIMIT),
        name="bias_expand",
    )(rel_bias, sinks, jnp.asarray(bkt_a), jnp.asarray(bkt_b), jnp.asarray(sb_a), jnp.asarray(sb_b))


def _qkv_kernel(x_ref, lng_ref, w_ref, g_ref, p_ref, *refs, rider=False):
    if rider:
        _ride(refs[:N_RIDER_IN], refs[-1])
        refs = refs[N_RIDER_IN:-1]
    qa_ref, ka_ref, va_ref, qb_ref, kb_ref, vb_ref, qc_ref, *cache_refs = refs
    h = _rms_rows(x_ref[...], lng_ref[...]).astype(BF16)
    pmat = p_ref[...]

    def proj(c0, c1):
        return _dot(h, w_ref[:, c0:c1])

    qa_ref[...] = _headnorm64(proj(C_QA, C_KA), g_ref[0:1, :], pmat)
    ka = _headnorm64(proj(C_KA, C_VA), g_ref[1:2, :], pmat)
    va = proj(C_VA, C_ZA)
    ka_ref[...] = ka
    va_ref[...] = va
    if cache_refs:
        kt_ref, vt_ref = cache_refs
        kt_ref[...] = ka.T
        vt_ref[...] = va.T
    qb_ref[...] = _headnorm64(proj(C_QB, C_KB), g_ref[2:3, :], pmat)
    kb_ref[...] = _headnorm64(proj(C_KB, C_VB), g_ref[3:4, 0:128], pmat)
    vb_ref[...] = proj(C_VB, C_ZB)
    qc_ref[...] = _headnorm128(proj(C_QC, C_ZC), g_ref[4:5, :])


def _qkv_call(x2, lng, w_bf, gains, pmat, cache_layout_for=None, rider=None):
    n = x2.shape[0]
    tm = min(512, n)
    steps = n // tm
    row = lambda w: pl.BlockSpec((tm, w), lambda i: (i, 0))
    widths = (512, 512, 512, 512, 128, 128, 512)
    in_specs = [row(D_MODEL), _const_spec((1, D_MODEL)), _const_spec((D_MODEL, C_END)),
                _const_spec((5, 512)), _const_spec((256, 256))]
    args = [x2, lng, w_bf, gains, pmat]
    out_specs = [row(w) for w in widths]
    out_shape = [jax.ShapeDtypeStruct((n, w), F32) for w in widths]
    if cache_layout_for is not None:
        batch, seq = cache_layout_for
        per = seq // tm
        out_specs += [pl.BlockSpec((None, 512, tm), lambda i: (i // per, 0, i % per))] * 2
        out_shape += [jax.ShapeDtypeStruct((batch, 512, seq), F32)] * 2
    if rider is not None:
        r_in, r_out = _rider_specs(rider[1], lambda i: i)
        in_specs += r_in
        args += list(rider[0])
        out_specs.append(r_out)
        out_shape.append(jax.ShapeDtypeStruct((steps, 1, 512), F32))
    return pl.pallas_call(
        functools.partial(_qkv_kernel, rider=rider is not None),
        grid=(steps,),
        in_specs=in_specs,
        out_specs=out_specs,
        out_shape=out_shape,
        compiler_params=_cparams(("arbitrary",)),
        name="qkv_proj",
    )(*args)


def _memkv_kernel(m_ref, g_ref, w_ref, gk_ref, mk_ref, mv_ref):
    h = _rms_rows(m_ref[...], g_ref[...]).astype(BF16)
    mk_ref[...] = _headnorm128(_dot(h, w_ref[:, 0:512]), gk_ref[...])
    mv_ref[...] = _dot(h, w_ref[:, 512:1024])


def _memkv_call(mem2, g, w_bf, gk):
    n = mem2.shape[0]
    tm = 512
    row = lambda w: pl.BlockSpec((tm, w), lambda i: (i, 0))
    return pl.pallas_call(
        _memkv_kernel,
        grid=(n // tm,),
        in_specs=[row(D_MODEL), _const_spec((1, D_MODEL)), _const_spec((D_MODEL, 2 * 512)),
                  _const_spec((1, 512))],
        out_specs=[row(512), row(512)],
        out_shape=[jax.ShapeDtypeStruct((n, 512), F32)] * 2,
        compiler_params=_cparams(("arbitrary",)),
        name="mem_kv_proj",
    )(mem2, g, w_bf, gk)


def _out_kernel(x_ref, oa_ref, ob_ref, oc_ref, lng_ref, w_ref, wbr_ref, wout_ref, *refs):
    if len(refs) > 1:
        _ride(refs[:N_RIDER_IN], refs[-1])
    y_ref = refs[-2] if len(refs) > 1 else refs[0]
    x = x_ref[...]
    h = _rms_rows(x, lng_ref[...]).astype(BF16)
    acc = None
    for br, (o_ref, cz, cg) in enumerate(((oa_ref, C_ZA, C_GA), (ob_ref, C_ZB, C_GB), (oc_ref, C_ZC, C_GC))):
        z = _dot(h, w_ref[:, cz:cz + 512])
        y = (o_ref[...] * (z * jax.nn.sigmoid(z))).astype(BF16)
        yb = _dot(y, wbr_ref[br])
        gate = jax.nn.sigmoid(_dot(h, w_ref[:, cg:cg + D_MODEL]))
        acc = gate * yb if acc is None else acc + gate * yb
    y_ref[...] = x + _dot(acc.astype(BF16), wout_ref[...])


def _out_call(x2, oa, ob, oc, lng, w_bf, wbr_bf, wout_bf, tm=512, rider=None):
    n = x2.shape[0]
    tm = min(tm, n)
    steps = n // tm
    row = lambda w: pl.BlockSpec((tm, w), lambda i: (i, 0))
    in_specs = [row(D_MODEL), row(512), row(512), row(512), _const_spec((1, D_MODEL)),
                _const_spec((D_MODEL, C_END)), _const_spec((3, 512, D_MODEL)), _const_spec((D_MODEL, D_MODEL))]
    args = [x2, oa, ob, oc, lng, w_bf, wbr_bf, wout_bf]
    out_specs = [row(D_MODEL)]
    out_shape = [jax.ShapeDtypeStruct((n, D_MODEL), F32)]
    if rider is not None:
        r_in, r_out = _rider_specs(rider[1], lambda i: i)
        in_specs += r_in
        args += list(rider[0])
        out_specs.append(r_out)
        out_shape.append(jax.ShapeDtypeStruct((steps, 1, 512), F32))
    return pl.pallas_call(
        _out_kernel,
        grid=(steps,),
        in_specs=in_specs,
        out_specs=out_specs,
        out_shape=out_shape,
        compiler_params=_cparams(("arbitrary",)),
        name="gated_out",
    )(*args)


PAD = BLK


def _low_lanes(rows):
    return lax.broadcasted_iota(jnp.int32, (rows, 128), 1) < 64


def _aligned(start):
    return start if isinstance(start, int) else pl.multiple_of(start, BLK)


def _pair_tile(q0, q1, kblk, v0, v1, b0, b1, sinks=None):
    low = _low_lanes(BLK)
    ups, ms = [], []
    for hh, (q, v, bias) in enumerate(((q0, v0, b0), (q1, v1, b1))):
        s = _dot_nt(q, kblk) + bias
        m = jnp.max(s, axis=-1, keepdims=True)
        if sinks is not None:
            m = jnp.maximum(m, sinks[hh])
        ups.append(_dot(jnp.exp(s - m).astype(BF16), v))
        ms.append(m)
    u = jnp.where(low, ups[0], ups[1])
    l = pltpu.roll(jnp.where(low, ups[1], ups[0]), 64, 1)
    return u, jnp.where(low, ms[0], ms[1]), l


def _first_bias(bias, also=None):
    gone = lax.broadcasted_iota(jnp.int32, bias.shape, 1) < BLK
    if also is not None:
        gone = gone & also
    return jnp.where(gone, NEG, bias)


def _dil_kernel(q_ref, k_ref, v_ref, bias_ref, *refs, rider=False):
    if rider:
        _ride(refs[:N_RIDER_IN], refs[N_RIDER_IN + 1])
        refs = refs[N_RIDER_IN:N_RIDER_IN + 1] + refs[N_RIDER_IN + 2:]
    o_ref, qs_ref, kc_ref, vc_ref, ru_ref, rm_ref, rl_ref = refs
    scale = HD_A ** -0.5
    seq = q_ref.shape[0]
    ntile = seq // BLK
    low = _low_lanes(BLK)

    def src_rows(g, t):
        d = A_DILATIONS[g]
        if d == 1:
            return pl.ds(_aligned(t * BLK), BLK)
        nblk = ntile // d
        r, n = t >> (nblk.bit_length() - 1), t & (nblk - 1)
        return pl.ds(r + n * (d * BLK), BLK, stride=d)

    zero = jnp.zeros((PAD, 128), BF16)
    for g in range(3):
        kc_ref[g, 0:PAD, :] = zero
        vc_ref[g, 0, 0:PAD, :] = zero
        vc_ref[g, 1, 0:PAD, :] = zero

        def stage(t, c, g=g):
            src = src_rows(g, t)
            dst = pl.ds(pl.multiple_of(t * BLK, BLK), BLK)
            dstp = pl.ds(pl.multiple_of(PAD + t * BLK, BLK), BLK)
            q = q_ref[src, :] * scale
            qs_ref[g, 0, dst, :] = jnp.where(low, q, 0.0).astype(BF16)
            qs_ref[g, 1, dst, :] = jnp.where(low, 0.0, q).astype(BF16)
            kc_ref[g, dstp, :] = k_ref[src, :].astype(BF16)
            v = v_ref[src, :]
            vc_ref[g, 0, dstp, :] = jnp.where(low, v, 1.0).astype(BF16)
            vc_ref[g, 1, dstp, :] = jnp.where(low, 1.0, v).astype(BF16)
            return c

        lax.fori_loop(0, ntile, stage, 0, unroll=4)

    def tiles(g, i):
        nblk = ntile // A_DILATIONS[g]
        res = []
        for j in range(UNROLL):
            t = i * UNROLL + j
            qrows = pl.ds(_aligned(t * BLK), BLK)
            krows = pl.ds(_aligned(t * BLK), 2 * BLK)
            bias = [bias_ref[g, hh] for hh in range(2)]
            if nblk <= UNROLL and j % nblk == 0:
                bias = [_first_bias(b) for b in bias]
            elif nblk > UNROLL and j == 0:
                bias = [_first_bias(b, i == 0) for b in bias]
            res.append(_pair_tile(qs_ref[g, 0, qrows, :], qs_ref[g, 1, qrows, :], kc_ref[g, krows, :],
                                  vc_ref[g, 0, krows, :], vc_ref[g, 1, krows, :], bias[0], bias[1]))
        return res

    for g in range(3):
        def body(i, c, g=g):
            res = tiles(g, i)
            for j in range(UNROLL):
                t = i * UNROLL + j
                rows = pl.ds(_aligned(t * BLK), BLK) if g == 2 else src_rows(g, t)
                ru_ref[g, rows, :], rm_ref[g, rows, :], rl_ref[g, rows, :] = res[j]
            return c

        lax.fori_loop(0, ntile // UNROLL, body, 0)

    d2 = A_DILATIONS[2]

    def natural_rows(ref, i):
        parts = [ref[2, pl.ds((v % 2) * 8 * BLK + i * (BLK // d2) + v // 2, 8, stride=BLK), :]
                 for v in range(BLK // 8)]
        return jnp.concatenate(parts, axis=0)

    def combine(i, c):
        rows = pl.ds(pl.multiple_of(i * BLK, BLK), BLK)
        ms = [rm_ref[0, rows, :], rm_ref[1, rows, :], natural_rows(rm_ref, i)]
        mx = jnp.maximum(jnp.maximum(ms[0], ms[1]), ms[2])
        ws = [jnp.exp(m - mx) for m in ms]
        num = ws[0] * ru_ref[0, rows, :] + ws[1] * ru_ref[1, rows, :] + ws[2] * natural_rows(ru_ref, i)
        den = ws[0] * rl_ref[0, rows, :] + ws[1] * rl_ref[1, rows, :] + ws[2] * natural_rows(rl_ref, i)
        o_ref[rows, :] = num / den
        return c

    lax.fori_loop(0, ntile, combine, 0, unroll=2)


def _dil_call(qa, ka, va, bias_a, batch, seq, rider=None):
    assert seq == A_DILATIONS[2] * BLK and UNROLL % (seq // (A_DILATIONS[1] * BLK)) == 0
    blk = pl.BlockSpec((None, seq, 128), lambda b, p: (b, 0, p))
    pairs = H_A // 2
    in_specs = [blk, blk, blk, pl.BlockSpec((3, 2, BLK, 2 * BLK), lambda b, p: (0, p, 0, 0))]
    args = [qa.reshape(batch, seq, 512), ka.reshape(batch, seq, 512), va.reshape(batch, seq, 512), bias_a]
    out_specs = [blk]
    out_shape = [jax.ShapeDtypeStruct((batch, seq, 512), F32)]
    if rider is not None:
        r_in, r_out = _rider_specs(rider[1], lambda b, p: b * pairs + p)
        in_specs += r_in
        args += list(rider[0])
        out_specs.append(r_out)
        out_shape.append(jax.ShapeDtypeStruct((batch * pairs, 1, 512), F32))
    return pl.pallas_call(
        functools.partial(_dil_kernel, rider=rider is not None),
        grid=(batch, pairs),
        in_specs=in_specs,
        out_specs=out_specs,
        out_shape=out_shape,
        scratch_shapes=[pltpu.VMEM((3, 2, seq, 128), BF16), pltpu.VMEM((3, PAD + seq, 128), BF16),
                        pltpu.VMEM((3, 2, PAD + seq, 128), BF16)] + [pltpu.VMEM((3, seq, 128), F32)] * 3,
        compiler_params=_cparams(("arbitrary", "arbitrary")),
        name="dilated_attn",
    )(*args)


def _swa_kernel(sink_ref, q_ref, k_ref, v_ref, bias_ref, o_ref, qs_ref, kc_ref, vc_ref):
    scale = HD_B ** -0.5
    seq = q_ref.shape[0]
    ntile = seq // BLK
    p = pl.program_id(1)
    kv = p // 2
    sinks = (sink_ref[2 * p], sink_ref[2 * p + 1])
    low = _low_lanes(BLK)
    own_half = (lax.broadcasted_iota(jnp.int32, (BLK, 128), 1) >= 64).astype(jnp.int32) == kv

    def dup(xf):
        return jnp.where(own_half, xf, pltpu.roll(xf, 64, 1))

    zero = jnp.zeros((PAD, 128), BF16)
    kc_ref[0:PAD, :] = zero
    vc_ref[0, 0:PAD, :] = zero
    vc_ref[1, 0:PAD, :] = zero

    def stage(t, c):
        src = pl.ds(pl.multiple_of(t * BLK, BLK), BLK)
        dstp = pl.ds(pl.multiple_of(PAD + t * BLK, BLK), BLK)
        q = q_ref[src, :] * scale
        qs_ref[0, src, :] = jnp.where(low, q, 0.0).astype(BF16)
        qs_ref[1, src, :] = jnp.where(low, 0.0, q).astype(BF16)
        kc_ref[dstp, :] = dup(k_ref[src, :]).astype(BF16)
        v = dup(v_ref[src, :])
        vc_ref[0, dstp, :] = jnp.where(low, v, 1.0).astype(BF16)
        vc_ref[1, dstp, :] = jnp.where(low, 1.0, v).astype(BF16)
        return c

    lax.fori_loop(0, ntile, stage, 0, unroll=4)

    def body(i, c):
        res = []
        for j in range(UNROLL):
            t = i * UNROLL + j
            qrows = pl.ds(pl.multiple_of(t * BLK, BLK), BLK)
            krows = pl.ds(pl.multiple_of(t * BLK, BLK), 2 * BLK)
            bias = [bias_ref[hh] for hh in range(2)]
            if j == 0:
                bias = [_first_bias(b, i == 0) for b in bias]
            u, m, l = _pair_tile(qs_ref[0, qrows, :], qs_ref[1, qrows, :], kc_ref[krows, :],
                                 vc_ref[0, krows, :], vc_ref[1, krows, :], bias[0], bias[1], sinks)
            sk = jnp.where(low, sinks[0], sinks[1])
            res.append(u / (l + jnp.exp(sk - m)))
        for j in range(UNROLL):
            o_ref[pl.ds(pl.multiple_of((i * UNROLL + j) * BLK, BLK), BLK), :] = res[j]
        return c

    lax.fori_loop(0, ntile // UNROLL, body, 0)


def _swa_call(sinks, qb, kb, vb, bias_b, batch, seq):
    qblk = pl.BlockSpec((None, seq, 128), lambda b, p: (b, 0, p))
    kblk = pl.BlockSpec((None, seq, 128), lambda b, p: (b, 0, 0))
    return pl.pallas_call(
        _swa_kernel,
        grid=(batch, H_B // 2),
        in_specs=[pl.BlockSpec(memory_space=pltpu.SMEM), qblk, kblk, kblk,
                  pl.BlockSpec((2, BLK, 2 * BLK), lambda b, p: (p, 0, 0))],
        out_specs=qblk,
        out_shape=jax.ShapeDtypeStruct((batch, seq, 512), F32),
        scratch_shapes=[pltpu.VMEM((2, seq, 128), BF16), pltpu.VMEM((PAD + seq, 128), BF16),
                        pltpu.VMEM((2, PAD + seq, 128), BF16)],
        compiler_params=_cparams(("arbitrary", "arbitrary")),
        name="swa_attn",
    )(sinks, qb.reshape(batch, seq, 512), kb.reshape(batch, seq, 128), vb.reshape(batch, seq, 128), bias_b)


def _mem_attn_kernel(q_ref, mk_ref, mv_ref, o_ref):
    scale = HD_C ** -0.5
    for h in range(H_C):
        cs = slice(h * HD_C, (h + 1) * HD_C)
        s = _dot_nt(q_ref[:, cs].astype(BF16), mk_ref[:, cs].astype(BF16)) * scale
        m = jnp.max(s, axis=-1, keepdims=True)
        p = jnp.exp(s - m)
        l = jnp.sum(p, axis=-1, keepdims=True)
        o_ref[:, cs] = _dot(p.astype(BF16), mv_ref[:, cs].astype(BF16)) / l


def _mem_attn_call(qc, mk, mv, batch, seq):
    tq = 512
    return pl.pallas_call(
        _mem_attn_kernel,
        grid=(batch, seq // tq),
        in_specs=[pl.BlockSpec((None, tq, 512), lambda b, i: (b, i, 0)),
                  pl.BlockSpec((None, N_MEM, 512), lambda b, i: (b, 0, 0)),
                  pl.BlockSpec((None, N_MEM, 512), lambda b, i: (b, 0, 0))],
        out_specs=pl.BlockSpec((None, tq, 512), lambda b, i: (b, i, 0)),
        out_shape=jax.ShapeDtypeStruct((batch, seq, 512), F32),
        compiler_params=_cparams(("arbitrary", "arbitrary")),
        name="mem_attn",
    )(qc.reshape(batch, seq, 512), mk.reshape(batch, N_MEM, 512), mv.reshape(batch, N_MEM, 512))


def _head_rows(row, width, head_dim):
    rid = lax.broadcasted_iota(jnp.int32, (8, width), 0)
    hid = lax.broadcasted_iota(jnp.int32, (8, width), 1) // head_dim
    return rid == hid, jnp.where(rid == hid, jnp.broadcast_to(row, (8, width)), 0.0)


def _samp_a_one(q, kn, vn, kt_ref, vt_ref, bias_ref, mult_ref, b0_ref):
    scale = HD_A ** -0.5
    nt = W_A // 128
    own, qh = _head_rows(q, 512, HD_A)
    qcols = jnp.transpose(jnp.broadcast_to(q, (128, 512)))
    s_heads = []
    for h in range(H_A):
        rows = slice(h * HD_A, (h + 1) * HD_A)
        qb = qcols[rows, :]
        s_heads.append(jnp.concatenate(
            [jnp.sum(kt_ref[rows, 128 * j:128 * (j + 1)] * qb, axis=0, keepdims=True) for j in range(nt)], axis=1))
    s = jnp.concatenate(s_heads, axis=0) * scale + bias_ref[...]
    ln = jnp.sum(qh * kn, axis=-1, keepdims=True) * scale + b0_ref[:, 0:1]
    m = jnp.maximum(jnp.max(s, axis=-1, keepdims=True), ln)
    p = jnp.exp(s - m) * mult_ref[...]
    pn = len(A_DILATIONS) * jnp.exp(ln - m)
    den = jnp.sum(p, axis=-1, keepdims=True) + pn
    accs = []
    for h in range(H_A):
        rows = slice(h * HD_A, (h + 1) * HD_A)
        acc = vt_ref[rows, 0:128] * p[h:h + 1, 0:128]
        for j in range(1, nt):
            acc = acc + vt_ref[rows, 128 * j:128 * (j + 1)] * p[h:h + 1, 128 * j:128 * (j + 1)]
        accs.append(acc)
    pv = jnp.sum(jnp.transpose(jnp.concatenate(accs, axis=0)), axis=0, keepdims=True)
    lanes = lambda col: jnp.sum(jnp.where(own, col, 0.0), axis=0, keepdims=True)
    return (pv + lanes(pn) * vn) / lanes(den)


N_RIDER_IN = 8


def _rider_arrays(qa, kn, vn, cache_k, cache_v, bias_sa, mult, s0a):
    nb = qa.shape[0]
    kt = jnp.transpose(cache_k, (0, 2, 3, 1)).reshape(nb, 512, W_A)
    vt = jnp.transpose(cache_v, (0, 2, 3, 1)).reshape(nb, 512, W_A)
    return (qa.reshape(nb, 1, 512), kn.reshape(nb, 1, 512), vn.reshape(nb, 1, 512), kt, vt, bias_sa, mult, s0a)


def _rider_specs(first, step_of):
    row = pl.BlockSpec((None, 1, 512), lambda *ids: (first + step_of(*ids), 0, 0))
    cblk = pl.BlockSpec((None, 512, W_A), lambda *ids: (first + step_of(*ids), 0, 0))
    ins = [row, row, row, cblk, cblk, _const_spec((8, W_A)), _const_spec((1, W_A)), _const_spec((8, 128))]
    return ins, pl.BlockSpec((None, 1, 512), lambda *ids: (step_of(*ids), 0, 0))


def _ride(in_refs, o_ref):
    q_ref, kn_ref, vn_ref, kt_ref, vt_ref, bias_ref, mult_ref, b0_ref = in_refs
    o_ref[...] = _samp_a_one(q_ref[...], kn_ref[...], vn_ref[...], kt_ref, vt_ref, bias_ref, mult_ref, b0_ref)


def _samp_a_kernel(*refs):
    _ride(refs[:N_RIDER_IN], refs[-1])


def _samp_a_call(arrays, first, count):
    r_in, r_out = _rider_specs(first, lambda i: i)
    return pl.pallas_call(
        _samp_a_kernel,
        grid=(count,),
        in_specs=r_in,
        out_specs=r_out,
        out_shape=jax.ShapeDtypeStruct((count, 1, 512), F32),
        compiler_params=_cparams(("arbitrary",)),
        name="sample_dilated_attn",
    )(*arrays)


def _samp_b_kernel(q_ref, kn_ref, vn_ref, kt_ref, vt_ref, bias_ref, b0_ref, sk_ref, o_ref):
    scale = HD_B ** -0.5
    low = lax.broadcasted_iota(jnp.int32, (1, 128), 1) < 64
    rid = lax.broadcasted_iota(jnp.int32, (8, 128), 0)
    kv_half = (lax.broadcasted_iota(jnp.int32, (8, 128), 1) >= 64) == (rid >= 4)

    bb = q_ref.shape[0]
    zero = jnp.zeros((8, 128), F32)
    qhs = []
    for b in range(bb):
        q = q_ref[b:b + 1, :]
        rows = []
        for h in range(H_B):
            chunk = q[:, 128 * (h // 2):128 * (h // 2) + 128]
            if (h % 2) != (h // 4):
                chunk = pltpu.roll(chunk, 64, 1)
            rows.append(chunk)
        qhs.append(jnp.where(kv_half, jnp.concatenate(rows, axis=0), 0.0))
    qblk = jnp.concatenate([jnp.concatenate([qhs[b] if c == b else zero for c in range(bb)], axis=1)
                            for b in range(bb)], axis=0)
    kt = kt_ref[...].reshape(bb * 128, W_B).astype(BF16)
    vt = vt_ref[...].reshape(bb * 128, W_B).astype(BF16)
    tile8 = lambda ref: jnp.concatenate([ref[...]] * bb, axis=0)
    s = _dot(qblk.astype(BF16), kt) * scale + tile8(bias_ref)
    knew = jnp.concatenate([jnp.broadcast_to(kn_ref[b:b + 1, :], (8, 128)) for b in range(bb)], axis=0)
    vnew = jnp.concatenate([jnp.broadcast_to(vn_ref[b:b + 1, :], (8, 128)) for b in range(bb)], axis=0)
    qall = jnp.concatenate(qhs, axis=0)
    ln = jnp.sum(qall * knew, axis=-1, keepdims=True) * scale + tile8(b0_ref)[:, 0:1]
    sk = tile8(sk_ref)[:, 0:1]
    m = jnp.maximum(jnp.maximum(jnp.max(s, axis=-1, keepdims=True), ln), sk)
    p = jnp.exp(s - m)
    pn = jnp.exp(ln - m)
    den = jnp.sum(p, axis=-1, keepdims=True) + pn + jnp.exp(sk - m)
    rall = _dot_nt(p.astype(BF16), vt)
    for b in range(bb):
        r = (rall[8 * b:8 * b + 8, 128 * b:128 * b + 128] + pn[8 * b:8 * b + 8] * vnew[8 * b:8 * b + 8])
        r = jnp.where(kv_half, r / den[8 * b:8 * b + 8], 0.0)
        chunks = []
        for c2 in range(4):
            pair = []
            for h in (2 * c2, 2 * c2 + 1):
                piece = r[h:h + 1, :]
                if (h % 2) != (h // 4):
                    piece = pltpu.roll(piece, 64, 1)
                pair.append(piece)
            chunks.append(jnp.where(low, pair[0], pair[1]))
        o_ref[b:b + 1, :] = jnp.concatenate(chunks, axis=1)


def _samp_b_call(qb, kn, vn, cache_k, cache_v, bias_sb, s0b, skb):
    nb = qb.shape[0]
    bb = 8
    kt = jnp.transpose(cache_k, (0, 2, 3, 1)).reshape(nb, 128, W_B)
    vt = jnp.transpose(cache_v, (0, 2, 3, 1)).reshape(nb, 128, W_B)
    row = lambda w: pl.BlockSpec((bb, w), lambda i: (i, 0))
    cblk = pl.BlockSpec((bb, 128, W_B), lambda i: (i, 0, 0))
    return pl.pallas_call(
        _samp_b_kernel,
        grid=(nb // bb,),
        in_specs=[row(512), row(128), row(128), cblk, cblk, _const_spec((8, W_B)),
                  _const_spec((8, 128)), _const_spec((8, 128))],
        out_specs=row(512),
        out_shape=jax.ShapeDtypeStruct((nb, 512), F32),
        compiler_params=_cparams(("arbitrary",)),
        name="sample_swa_attn",
    )(qb, kn, vn, kt, vt, bias_sb, s0b, skb)


def _samp_c_kernel(q_ref, mk_ref, mv_ref, mask_ref, o_ref):
    scale = HD_C ** -0.5
    zeros = jnp.zeros((4, HD_C), F32)

    def body(b, c):
        q8 = jnp.concatenate([q_ref[b], zeros], axis=0).astype(BF16)
        s = _dot_nt(q8, mk_ref[b].astype(BF16)) * scale + mask_ref[...]
        m = jnp.max(s, axis=-1, keepdims=True)
        p = jnp.exp(s - m)
        den = jnp.sum(p, axis=-1, keepdims=True)
        r = _dot(p.astype(BF16), mv_ref[b].astype(BF16)) / den
        o_ref[b] = r[0:4, :]
        return c

    lax.fori_loop(0, q_ref.shape[0], body, 0, unroll=True)


def _samp_c_call(qc, cache_k, cache_v):
    nb = qc.shape[0]
    bb = 8
    rows = N_MEM * H_C
    head_of_row = np.arange(rows)[None, :] % H_C
    mask = np.where(head_of_row == np.arange(8)[:, None], 0.0, NEG).astype(np.float32)
    qblk = pl.BlockSpec((bb, H_C, HD_C), lambda i: (i, 0, 0))
    cblk = pl.BlockSpec((bb, rows, HD_C), lambda i: (i, 0, 0))
    out = pl.pallas_call(
        _samp_c_kernel,
        grid=(nb // bb,),
        in_specs=[qblk, cblk, cblk, _const_spec((8, rows))],
        out_specs=qblk,
        out_shape=jax.ShapeDtypeStruct((nb, H_C, HD_C), F32),
        compiler_params=_cparams(("arbitrary",)),
        name="sample_mem_attn",
    )(qc.reshape(nb, H_C, HD_C), cache_k.reshape(nb, rows, HD_C), cache_v.reshape(nb, rows, HD_C),
      jnp.asarray(mask))
    return out.reshape(nb, 512)


def kernel(x_prompt, x_sample, mem_prompt, cache_a_k, cache_a_v, cache_b_k, cache_b_v, cache_mem_k,
           cache_mem_v, rel_bias, ln_g, w_in, gq_a, gk_a, gq_b, gk_b, gq_c, gk_c, sinks_b, mem_ln_g,
           w_mem_kv, w_br_a, w_br_b, w_br_c, w_out):
    batch, seq, _ = x_prompt.shape
    nsamp = x_sample.shape[0]
    assert ln_g.shape[0] == 1 and x_sample.shape[1] == 1
    assert (batch, seq, nsamp) == (8, 2048, 128) and w_in.shape == (1, D_MODEL, C_END)
    assert cache_a_k.shape == (1, nsamp, W_A, H_A, HD_A) and cache_b_k.shape == (1, nsamp, W_B, KV_B, HD_B)

    w_bf = w_in[0].astype(BF16)
    wmem_bf = w_mem_kv[0].astype(BF16)
    wbr_bf = jnp.stack([w_br_a[0], w_br_b[0], w_br_c[0]]).astype(BF16)
    wout_bf = w_out[0].astype(BF16)
    lng = ln_g.reshape(1, D_MODEL)
    gains = jnp.stack([jnp.tile(gq_a[0], 8), jnp.tile(gk_a[0], 8), jnp.tile(gq_b[0], 8),
                       jnp.tile(gk_b[0], 8), jnp.tile(gq_c[0], 4)])
    gk_c4 = jnp.tile(gk_c[0], 4).reshape(1, 512)
    blockdiag = np.kron(np.eye(4, dtype=np.float32), np.ones((64, 64), np.float32))
    pmat = jnp.asarray(blockdiag, BF16)

    *tables, mult_a = _bucket_tables()
    bias_a, bias_b, bias_sa, bias_sb, s0a, s0b, skb = _bias_call(rel_bias, sinks_b.reshape(H_B), tables)

    xs2 = x_sample.reshape(nsamp, D_MODEL)
    sqa, ska, sva, sqb, skb_new, svb, sqc = _qkv_call(xs2, lng, w_bf, gains, pmat)
    riders = _rider_arrays(sqa, ska, sva, cache_a_k[0], cache_a_v[0], bias_sa, jnp.asarray(mult_a), s0a)
    out_tm = 256
    n_qkv = batch * seq // 512
    n_dil = batch * (H_A // 2)
    n_out = batch * seq // out_tm
    n_alone = nsamp - n_qkv - n_dil - n_out
    assert n_alone >= 0

    xp2 = x_prompt.reshape(batch * seq, D_MODEL)
    qa, ka, va, qb, kb, vb, qc, ka_t, va_t, soa_1 = _qkv_call(xp2, lng, w_bf, gains, pmat, (batch, seq),
                                                             rider=(riders, 0))
    mk, mv = _memkv_call(mem_prompt.reshape(batch * N_MEM, D_MODEL), mem_ln_g.reshape(1, D_MODEL),
                         wmem_bf, gk_c4)
    oa, soa_2 = _dil_call(qa, ka, va, bias_a, batch, seq, rider=(riders, n_qkv))
    ob = _swa_call(sinks_b.reshape(H_B), qb, kb, vb, bias_b, batch, seq)
    oc = _mem_attn_call(qc, mk, mv, batch, seq)
    yp, soa_3 = _out_call(xp2, oa.reshape(-1, 512), ob.reshape(-1, 512), oc.reshape(-1, 512),
                          lng, w_bf, wbr_bf, wout_bf, tm=out_tm, rider=(riders, n_qkv + n_dil))
    yp = yp.reshape(batch, seq, D_MODEL)

    soa_parts = [soa_1, soa_2, soa_3]
    if n_alone:
        soa_parts.append(_samp_a_call(riders, n_qkv + n_dil + n_out, n_alone))
    soa = jnp.concatenate(soa_parts, axis=0).reshape(nsamp, 512)
    sob = _samp_b_call(sqb, skb_new, svb, cache_b_k[0], cache_b_v[0], bias_sb, s0b, skb)
    soc = _samp_c_call(sqc, cache_mem_k[0], cache_mem_v[0])
    ys, = _out_call(xs2, soa, sob, soc, lng, w_bf, wbr_bf, wout_bf)

    return (yp, ys.reshape(nsamp, 1, D_MODEL),
            jnp.transpose(ka_t.reshape(1, batch, H_A, HD_A, seq), (0, 1, 4, 2, 3)),
            jnp.transpose(va_t.reshape(1, batch, H_A, HD_A, seq), (0, 1, 4, 2, 3)),
            kb.reshape(batch, seq, KV_B, HD_B)[None, :, seq - W_B:],
            vb.reshape(batch, seq, KV_B, HD_B)[None, :, seq - W_B:],
            mk.reshape(1, batch, N_MEM, H_C, HD_C), mv.reshape(1, batch, N_MEM, H_C, HD_C),
            ska.reshape(1, nsamp, 1, H_A, HD_A), sva.reshape(1, nsamp, 1, H_A, HD_A),
            skb_new.reshape(1, nsamp, 1, KV_B, HD_B), svb.reshape(1, nsamp, 1, KV_B, HD_B))
```

```python
import functools
import math

import numpy as np
import jax
import jax.numpy as jnp
from jax import lax
from jax.experimental import pallas as pl
from jax.experimental.pallas import tpu as pltpu

F32 = jnp.float32
BF16 = jnp.bfloat16

D_MODEL = 1024
H_A, HD_A = 8, 64
A_DILATIONS = (1, 4, 16)
A_STEPS = 128
W_A = 2048
H_B, KV_B, HD_B, W_B = 8, 2, 64, 128
H_C, HD_C, N_MEM = 4, 128, 256
NUM_BUCKETS, MAX_DISTANCE = 32, 2048
EPS = 1e-6
NEG = -1e30
BLK = 128
UNROLL = 8

C_QA, C_KA, C_VA, C_ZA = 0, 512, 1024, 1536
C_QB, C_KB, C_VB, C_ZB = 2048, 2560, 2688, 2816
C_QC, C_ZC = 3328, 3840
C_GA, C_GB, C_GC, C_END = 4352, 5376, 6400, 7424

VMEM_LIMIT = 56 * 1024 * 1024


def _cparams(sem):
    return pltpu.CompilerParams(dimension_semantics=sem, vmem_limit_bytes=VMEM_LIMIT)


def _const_spec(shape):
    nd = len(shape)
    return pl.BlockSpec(shape, lambda *_: (0,) * nd, pipeline_mode=pl.Buffered(1))


def _t5_bucket_np(dist):
    n = np.maximum(dist, 0)
    max_exact = NUM_BUCKETS // 2
    nf = np.maximum(n, 1).astype(np.float32)
    large = max_exact + (np.log(nf / np.float32(max_exact))
                         / np.float32(math.log(MAX_DISTANCE / max_exact))
                         * np.float32(NUM_BUCKETS - max_exact)).astype(np.int32)
    return np.where(n < max_exact, n, np.minimum(large, NUM_BUCKETS - 1)).astype(np.int32)


def _a_multiplicity(dist):
    return sum(((dist % d == 0) & (dist <= A_STEPS * d)).astype(np.int32) for d in A_DILATIONS)


def _bucket_tables():
    i = np.arange(BLK)[:, None]
    j = np.arange(2 * BLK)[None, :]
    rel = i + BLK - j
    bkt_a = np.stack([np.where((rel >= 0) & (rel <= A_STEPS), _t5_bucket_np(rel * d), -1)
                      for d in A_DILATIONS]).astype(np.int32)
    bkt_b = np.where((rel >= 0) & (rel < W_B), _t5_bucket_np(rel), -1).astype(np.int32)
    dist_a = W_A - np.arange(W_A)
    mult_a = _a_multiplicity(dist_a)
    sb_a = np.broadcast_to(np.where(mult_a > 0, _t5_bucket_np(dist_a), -1), (8, W_A)).astype(np.int32)
    dist_b = W_B - np.arange(W_B)
    sb_b = np.broadcast_to(np.where(dist_b < W_B, _t5_bucket_np(dist_b), -1), (8, W_B)).astype(np.int32)
    return bkt_a, bkt_b, sb_a, sb_b, mult_a.astype(np.float32).reshape(1, W_A)


def _rms_rows(x, g):
    return x * lax.rsqrt(jnp.mean(x * x, axis=-1, keepdims=True) + EPS) * g


def _dot(a, b):
    return jnp.dot(a, b, preferred_element_type=F32)


def _dot_nt(a, b):
    return lax.dot_general(a, b, (((1,), (1,)), ((), ())), preferred_element_type=F32)


def _headnorm64(t, g, pmat):
    sq = (t * t).astype(BF16)
    width = t.shape[1]
    step = min(width, 256)
    pp = pmat[:step, :step]
    parts = [_dot(sq[:, c:c + step], pp) for c in range(0, width, step)]
    ss = parts[0] if len(parts) == 1 else jnp.concatenate(parts, axis=1)
    return t * lax.rsqrt(ss * (1.0 / 64.0) + EPS) * g


def _headnorm128(t, g):
    parts = []
    for c in range(0, t.shape[1], 128):
        seg = t[:, c:c + 128]
        parts.append(seg * lax.rsqrt(jnp.mean(seg * seg, axis=-1, keepdims=True) + EPS))
    return jnp.concatenate(parts, axis=1) * g


def _bias_kernel(tbl_ref, sink_ref, bkta_ref, bktb_ref, sba_ref, sbb_ref,
                 ba_ref, bb_ref, oa_ref, ob_ref, s0a_ref, s0b_ref, sk_ref, *, present):
    pa, pb, psa, psb = present
    for g in range(3):
        bk = bkta_ref[g]
        for h in range(H_A):
            ba_ref[g, h] = jnp.full((BLK, 2 * BLK), NEG, F32)
        for b in pa[g]:
            hit = bk == b
            for h in range(H_A):
                ba_ref[g, h] = jnp.where(hit, tbl_ref[b, h], ba_ref[g, h])
    bk = bktb_ref[...]
    for h in range(H_B):
        bb_ref[h] = jnp.full((BLK, 2 * BLK), NEG, F32)
    for b in pb:
        hit = bk == b
        for h in range(H_B):
            bb_ref[h] = jnp.where(hit, tbl_ref[b, H_A + h], bb_ref[h])
    for (src, dst, buckets, col0) in ((sba_ref, oa_ref, psa, 0), (sbb_ref, ob_ref, psb, H_A)):
        bk = src[...]
        row = lax.broadcasted_iota(jnp.int32, bk.shape, 0)
        acc = jnp.full(bk.shape, NEG, F32)
        for b in buckets:
            hit = bk == b
            for h in range(8):
                acc = jnp.where(hit & (row == h), tbl_ref[b, col0 + h], acc)
        dst[...] = acc
    row8 = lax.broadcasted_iota(jnp.int32, (8, 128), 0)
    a0 = jnp.zeros((8, 128), F32)
    b0 = jnp.zeros((8, 128), F32)
    sk = jnp.zeros((8, 128), F32)
    for h in range(8):
        a0 = jnp.where(row8 == h, tbl_ref[0, h], a0)
        b0 = jnp.where(row8 == h, tbl_ref[0, H_A + h], b0)
        sk = jnp.where(row8 == h, sink_ref[h], sk)
    s0a_ref[...] = a0
    s0b_ref[...] = b0
    sk_ref[...] = sk


def _bias_call(rel_bias, sinks, tables):
    bkt_a, bkt_b, sb_a, sb_b = tables
    uniq = lambda a: tuple(int(b) for b in np.unique(a) if b >= 0)
    present = (tuple(uniq(bkt_a[g]) for g in range(3)), uniq(bkt_b), uniq(sb_a), uniq(sb_b))
    smem = pl.BlockSpec(memory_space=pltpu.SMEM)
    vmem = pl.BlockSpec(memory_space=pltpu.VMEM)
    return pl.pallas_call(
        functools.partial(_bias_kernel, present=present),
        in_specs=[smem, smem, vmem, vmem, vmem, vmem],
        out_specs=[vmem] * 7,
        out_shape=[jax.ShapeDtypeStruct((3, H_A, BLK, 2 * BLK), F32),
                   jax.ShapeDtypeStruct((H_B, BLK, 2 * BLK), F32),
                   jax.ShapeDtypeStruct((8, W_A), F32),
                   jax.ShapeDtypeStruct((8, W_B), F32),
                   jax.ShapeDtypeStruct((8, 128), F32),
                   jax.ShapeDtypeStruct((8, 128), F32),
                   jax.ShapeDtypeStruct((8, 128), F32)],
        compiler_params=pltpu.CompilerParams(vmem_limit_bytes=VMEM_LIMIT),
        name="bias_expand",
    )(rel_bias, sinks, jnp.asarray(bkt_a), jnp.asarray(bkt_b), jnp.asarray(sb_a), jnp.asarray(sb_b))


def _qkv_kernel(x_ref, lng_ref, w_ref, g_ref, p_ref, *refs, rider=False):
    if rider:
        _ride(refs[:N_RIDER_IN], refs[-1])
        refs = refs[N_RIDER_IN:-1]
    qa_ref, ka_ref, va_ref, qb_ref, kb_ref, vb_ref, qc_ref, *cache_refs = refs
    h = _rms_rows(x_ref[...], lng_ref[...]).astype(BF16)
    pmat = p_ref[...]

    def proj(c0, c1):
        return _dot(h, w_ref[:, c0:c1])

    qa_ref[...] = _headnorm64(proj(C_QA, C_KA), g_ref[0:1, :], pmat)
    ka = _headnorm64(proj(C_KA, C_VA), g_ref[1:2, :], pmat)
    va = proj(C_VA, C_ZA)
    ka_ref[...] = ka
    va_ref[...] = va
    if cache_refs:
        kt_ref, vt_ref = cache_refs
        kt_ref[...] = ka.T
        vt_ref[...] = va.T
    qb_ref[...] = _headnorm64(proj(C_QB, C_KB), g_ref[2:3, :], pmat)
    kb_ref[...] = _headnorm64(proj(C_KB, C_VB), g_ref[3:4, 0:128], pmat)
    vb_ref[...] = proj(C_VB, C_ZB)
    qc_ref[...] = _headnorm128(proj(C_QC, C_ZC), g_ref[4:5, :])


def _qkv_call(x2, lng, w_bf, gains, pmat, cache_layout_for=None, rider=None):
    n = x2.shape[0]
    tm = min(512, n)
    steps = n // tm
    row = lambda w: pl.BlockSpec((tm, w), lambda i: (i, 0))
    widths = (512, 512, 512, 512, 128, 128, 512)
    in_specs = [row(D_MODEL), _const_spec((1, D_MODEL)), _const_spec((D_MODEL, C_END)),
                _const_spec((5, 512)), _const_spec((256, 256))]
    args = [x2, lng, w_bf, gains, pmat]
    out_specs = [row(w) for w in widths]
    out_shape = [jax.ShapeDtypeStruct((n, w), F32) for w in widths]
    if cache_layout_for is not None:
        batch, seq = cache_layout_for
        per = seq // tm
        out_specs += [pl.BlockSpec((None, 512, tm), lambda i: (i // per, 0, i % per))] * 2
        out_shape += [jax.ShapeDtypeStruct((batch, 512, seq), F32)] * 2
    if rider is not None:
        r_in, r_out = _rider_specs(rider[1], lambda i: i)
        in_specs += r_in
        args += list(rider[0])
        out_specs.append(r_out)
        out_shape.append(jax.ShapeDtypeStruct((steps, 1, 512), F32))
    return pl.pallas_call(
        functools.partial(_qkv_kernel, rider=rider is not None),
        grid=(steps,),
        in_specs=in_specs,
        out_specs=out_specs,
        out_shape=out_shape,
        compiler_params=_cparams(("arbitrary",)),
        name="qkv_proj",
    )(*args)


def _memkv_kernel(m_ref, g_ref, w_ref, gk_ref, mk_ref, mv_ref):
    h = _rms_rows(m_ref[...], g_ref[...]).astype(BF16)
    mk_ref[...] = _headnorm128(_dot(h, w_ref[:, 0:512]), gk_ref[...])
    mv_ref[...] = _dot(h, w_ref[:, 512:1024])


def _memkv_call(mem2, g, w_bf, gk):
    n = mem2.shape[0]
    tm = 512
    row = lambda w: pl.BlockSpec((tm, w), lambda i: (i, 0))
    return pl.pallas_call(
        _memkv_kernel,
        grid=(n // tm,),
        in_specs=[row(D_MODEL), _const_spec((1, D_MODEL)), _const_spec((D_MODEL, 2 * 512)),
                  _const_spec((1, 512))],
        out_specs=[row(512), row(512)],
        out_shape=[jax.ShapeDtypeStruct((n, 512), F32)] * 2,
        compiler_params=_cparams(("arbitrary",)),
        name="mem_kv_proj",
    )(mem2, g, w_bf, gk)


def _out_kernel(x_ref, oa_ref, ob_ref, oc_ref, lng_ref, w_ref, wbr_ref, wout_ref, *refs):
    if len(refs) > 1:
        _ride(refs[:N_RIDER_IN], refs[-1])
    y_ref = refs[-2] if len(refs) > 1 else refs[0]
    x = x_ref[...]
    h = _rms_rows(x, lng_ref[...]).astype(BF16)
    acc = None
    for br, (o_ref, cz, cg) in enumerate(((oa_ref, C_ZA, C_GA), (ob_ref, C_ZB, C_GB), (oc_ref, C_ZC, C_GC))):
        z = _dot(h, w_ref[:, cz:cz + 512])
        y = (o_ref[...] * (z * jax.nn.sigmoid(z))).astype(BF16)
        yb = _dot(y, wbr_ref[br])
        gate = jax.nn.sigmoid(_dot(h, w_ref[:, cg:cg + D_MODEL]))
        acc = gate * yb if acc is None else acc + gate * yb
    y_ref[...] = x + _dot(acc.astype(BF16), wout_ref[...])


def _out_call(x2, oa, ob, oc, lng, w_bf, wbr_bf, wout_bf, tm=512, rider=None):
    n = x2.shape[0]
    tm = min(tm, n)
    steps = n // tm
    row = lambda w: pl.BlockSpec((tm, w), lambda i: (i, 0))
    in_specs = [row(D_MODEL), row(512), row(512), row(512), _const_spec((1, D_MODEL)),
                _const_spec((D_MODEL, C_END)), _const_spec((3, 512, D_MODEL)), _const_spec((D_MODEL, D_MODEL))]
    args = [x2, oa, ob, oc, lng, w_bf, wbr_bf, wout_bf]
    out_specs = [row(D_MODEL)]
    out_shape = [jax.ShapeDtypeStruct((n, D_MODEL), F32)]
    if rider is not None:
        r_in, r_out = _rider_specs(rider[1], lambda i: i)
        in_specs += r_in
        args += list(rider[0])
        out_specs.append(r_out)
        out_shape.append(jax.ShapeDtypeStruct((steps, 1, 512), F32))
    return pl.pallas_call(
        _out_kernel,
        grid=(steps,),
        in_specs=in_specs,
        out_specs=out_specs,
        out_shape=out_shape,
        compiler_params=_cparams(("arbitrary",)),
        name="gated_out",
    )(*args)


PAD = BLK
CLS_PITCH = BLK + 8


def _low_lanes(rows):
    return lax.broadcasted_iota(jnp.int32, (rows, 128), 1) < 64


def _aligned(start):
    return start if isinstance(start, int) else pl.multiple_of(start, BLK)


def _pair_tile(q0, q1, kblk, v0, v1, b0, b1, sinks=None):
    low = _low_lanes(BLK)
    ups, ms = [], []
    for hh, (q, v, bias) in enumerate(((q0, v0, b0), (q1, v1, b1))):
        s = _dot_nt(q, kblk) + bias
        m = jnp.max(s, axis=-1, keepdims=True)
        if sinks is not None:
            m = jnp.maximum(m, sinks[hh])
        ups.append(_dot(jnp.exp(s - m).astype(BF16), v))
        ms.append(m)
    u = jnp.where(low, ups[0], ups[1])
    l = pltpu.roll(jnp.where(low, ups[1], ups[0]), 64, 1)
    return u, jnp.where(low, ms[0], ms[1]), l


def _first_bias(bias, also=None):
    gone = lax.broadcasted_iota(jnp.int32, bias.shape, 1) < BLK
    if also is not None:
        gone = gone & also
    return jnp.where(gone, NEG, bias)


def _dil_kernel(q_ref, k_ref, v_ref, bias_ref, *refs, rider=False):
    if rider:
        _ride(refs[:N_RIDER_IN], refs[N_RIDER_IN + 1])
        refs = refs[N_RIDER_IN:N_RIDER_IN + 1] + refs[N_RIDER_IN + 2:]
    o_ref, qs_ref, kc_ref, vc_ref, c4_ref, ru_ref, rm_ref, rl_ref = refs
    scale = HD_A ** -0.5
    seq = q_ref.shape[0]
    ntile = seq // BLK
    low = _low_lanes(BLK)
    d1, d2 = A_DILATIONS[1], A_DILATIONS[2]

    def src_rows(g, t):
        d = A_DILATIONS[g]
        if d == 1:
            return pl.ds(_aligned(t * BLK), BLK)
        nblk = ntile // d
        r, n = t >> (nblk.bit_length() - 1), t & (nblk - 1)
        return pl.ds(r + n * (d * BLK), BLK, stride=d)

    zero = jnp.zeros((PAD, 128), BF16)
    for g in range(3):
        kc_ref[g, 0:PAD, :] = zero
        vc_ref[g, 0, 0:PAD, :] = zero
        vc_ref[g, 1, 0:PAD, :] = zero

        def stage(t, c, g=g):
            dst = pl.ds(pl.multiple_of(t * BLK, BLK), BLK)
            dstp = pl.ds(pl.multiple_of(PAD + t * BLK, BLK), BLK)
            if g < 2:
                src = src_rows(g, t)
                q, k, v = q_ref[src, :], k_ref[src, :], v_ref[src, :]
            else:
                src = pl.ds((t & (d1 - 1)) * (seq // d1) + (t >> (d1.bit_length() - 1)), BLK, stride=d2 // d1)
                q, k, v = c4_ref[0, src, :], c4_ref[1, src, :], c4_ref[2, src, :]
            if g == 1:
                c4_ref[0, dst, :], c4_ref[1, dst, :], c4_ref[2, dst, :] = q, k, v
            q = q * scale
            qs_ref[g, 0, dst, :] = jnp.where(low, q, 0.0).astype(BF16)
            qs_ref[g, 1, dst, :] = jnp.where(low, 0.0, q).astype(BF16)
            kc_ref[g, dstp, :] = k.astype(BF16)
            vc_ref[g, 0, dstp, :] = jnp.where(low, v, 1.0).astype(BF16)
            vc_ref[g, 1, dstp, :] = jnp.where(low, 1.0, v).astype(BF16)
            return c

        lax.fori_loop(0, ntile, stage, 0, unroll=4)

    def tiles(g, i):
        nblk = ntile // A_DILATIONS[g]
        res = []
        for j in range(UNROLL):
            t = i * UNROLL + j
            qrows = pl.ds(_aligned(t * BLK), BLK)
            krows = pl.ds(_aligned(t * BLK), 2 * BLK)
            bias = [bias_ref[g, hh] for hh in range(2)]
            if nblk <= UNROLL and j % nblk == 0:
                bias = [_first_bias(b) for b in bias]
            elif nblk > UNROLL and j == 0:
                bias = [_first_bias(b, i == 0) for b in bias]
            res.append(_pair_tile(qs_ref[g, 0, qrows, :], qs_ref[g, 1, qrows, :], kc_ref[g, krows, :],
                                  vc_ref[g, 0, krows, :], vc_ref[g, 1, krows, :], bias[0], bias[1]))
        return res

    for g in range(3):
        def body(i, c, g=g):
            res = tiles(g, i)
            for j in range(UNROLL):
                t = i * UNROLL + j
                rows = pl.ds(pl.multiple_of(t * CLS_PITCH, 8), BLK) if g == 2 else src_rows(g, t)
                ru_ref[g, rows, :], rm_ref[g, rows, :], rl_ref[g, rows, :] = res[j]
            return c

        lax.fori_loop(0, ntile // UNROLL, body, 0)

    def natural_rows(ref, i):
        parts = [ref[2, pl.ds((v % 2) * 8 * CLS_PITCH + i * (BLK // d2) + v // 2, 8, stride=CLS_PITCH), :]
                 for v in range(BLK // 8)]
        return jnp.concatenate(parts, axis=0)

    def combine(i, c):
        rows = pl.ds(pl.multiple_of(i * BLK, BLK), BLK)
        ms = [rm_ref[0, rows, :], rm_ref[1, rows, :], natural_rows(rm_ref, i)]
        mx = jnp.maximum(jnp.maximum(ms[0], ms[1]), ms[2])
        ws = [jnp.exp(m - mx) for m in ms]
        num = ws[0] * ru_ref[0, rows, :] + ws[1] * ru_ref[1, rows, :] + ws[2] * natural_rows(ru_ref, i)
        den = ws[0] * rl_ref[0, rows, :] + ws[1] * rl_ref[1, rows, :] + ws[2] * natural_rows(rl_ref, i)
        o_ref[rows, :] = num / den
        return c

    lax.fori_loop(0, ntile, combine, 0, unroll=2)


def _dil_call(qa, ka, va, bias_a, batch, seq, rider=None):
    assert seq == A_DILATIONS[2] * BLK and UNROLL % (seq // (A_DILATIONS[1] * BLK)) == 0
    blk = pl.BlockSpec((None, seq, 128), lambda b, p: (b, 0, p))
    pairs = H_A // 2
    in_specs = [blk, blk, blk, pl.BlockSpec((3, 2, BLK, 2 * BLK), lambda b, p: (0, p, 0, 0))]
    args = [qa.reshape(batch, seq, 512), ka.reshape(batch, seq, 512), va.reshape(batch, seq, 512), bias_a]
    out_specs = [blk]
    out_shape = [jax.ShapeDtypeStruct((batch, seq, 512), F32)]
    if rider is not None:
        r_in, r_out = _rider_specs(rider[1], lambda b, p: b * pairs + p)
        in_specs += r_in
        args += list(rider[0])
        out_specs.append(r_out)
        out_shape.append(jax.ShapeDtypeStruct((batch * pairs, 1, 512), F32))
    return pl.pallas_call(
        functools.partial(_dil_kernel, rider=rider is not None),
        grid=(batch, pairs),
        in_specs=in_specs,
        out_specs=out_specs,
        out_shape=out_shape,
        scratch_shapes=[pltpu.VMEM((3, 2, seq, 128), BF16), pltpu.VMEM((3, PAD + seq, 128), BF16),
                        pltpu.VMEM((3, 2, PAD + seq, 128), BF16), pltpu.VMEM((3, seq, 128), F32)]
                       + [pltpu.VMEM((3, A_DILATIONS[2] * CLS_PITCH, 128), F32)] * 3,
        compiler_params=_cparams(("arbitrary", "arbitrary")),
        name="dilated_attn",
    )(*args)


def _swa_kernel(sink_ref, q_ref, k_ref, v_ref, bias_ref, o_ref, qs_ref, kc_ref, vc_ref):
    scale = HD_B ** -0.5
    seq = q_ref.shape[0]
    ntile = seq // BLK
    p = pl.program_id(1)
    kv = p // 2
    sinks = (sink_ref[2 * p], sink_ref[2 * p + 1])
    low = _low_lanes(BLK)
    own_half = (lax.broadcasted_iota(jnp.int32, (BLK, 128), 1) >= 64).astype(jnp.int32) == kv

    def dup(xf):
        return jnp.where(own_half, xf, pltpu.roll(xf, 64, 1))

    zero = jnp.zeros((PAD, 128), BF16)
    kc_ref[0:PAD, :] = zero
    vc_ref[0, 0:PAD, :] = zero
    vc_ref[1, 0:PAD, :] = zero

    def stage(t, c):
        src = pl.ds(pl.multiple_of(t * BLK, BLK), BLK)
        dstp = pl.ds(pl.multiple_of(PAD + t * BLK, BLK), BLK)
        q = q_ref[src, :] * scale
        qs_ref[0, src, :] = jnp.where(low, q, 0.0).astype(BF16)
        qs_ref[1, src, :] = jnp.where(low, 0.0, q).astype(BF16)
        kc_ref[dstp, :] = dup(k_ref[src, :]).astype(BF16)
        v = dup(v_ref[src, :])
        vc_ref[0, dstp, :] = jnp.where(low, v, 1.0).astype(BF16)
        vc_ref[1, dstp, :] = jnp.where(low, 1.0, v).astype(BF16)
        return c

    lax.fori_loop(0, ntile, stage, 0, unroll=4)

    def body(i, c):
        res = []
        for j in range(UNROLL):
            t = i * UNROLL + j
            qrows = pl.ds(pl.multiple_of(t * BLK, BLK), BLK)
            krows = pl.ds(pl.multiple_of(t * BLK, BLK), 2 * BLK)
            bias = [bias_ref[hh] for hh in range(2)]
            if j == 0:
                bias = [_first_bias(b, i == 0) for b in bias]
            u, m, l = _pair_tile(qs_ref[0, qrows, :], qs_ref[1, qrows, :], kc_ref[krows, :],
                                 vc_ref[0, krows, :], vc_ref[1, krows, :], bias[0], bias[1], sinks)
            sk = jnp.where(low, sinks[0], sinks[1])
            res.append(u / (l + jnp.exp(sk - m)))
        for j in range(UNROLL):
            o_ref[pl.ds(pl.multiple_of((i * UNROLL + j) * BLK, BLK), BLK), :] = res[j]
        return c

    lax.fori_loop(0, ntile // UNROLL, body, 0)


def _swa_call(sinks, qb, kb, vb, bias_b, batch, seq):
    qblk = pl.BlockSpec((None, seq, 128), lambda b, p: (b, 0, p))
    kblk = pl.BlockSpec((None, seq, 128), lambda b, p: (b, 0, 0))
    return pl.pallas_call(
        _swa_kernel,
        grid=(batch, H_B // 2),
        in_specs=[pl.BlockSpec(memory_space=pltpu.SMEM), qblk, kblk, kblk,
                  pl.BlockSpec((2, BLK, 2 * BLK), lambda b, p: (p, 0, 0))],
        out_specs=qblk,
        out_shape=jax.ShapeDtypeStruct((batch, seq, 512), F32),
        scratch_shapes=[pltpu.VMEM((2, seq, 128), BF16), pltpu.VMEM((PAD + seq, 128), BF16),
                        pltpu.VMEM((2, PAD + seq, 128), BF16)],
        compiler_params=_cparams(("arbitrary", "arbitrary")),
        name="swa_attn",
    )(sinks, qb.reshape(batch, seq, 512), kb.reshape(batch, seq, 128), vb.reshape(batch, seq, 128), bias_b)


def _mem_attn_kernel(q_ref, mk_ref, mv_ref, o_ref):
    scale = HD_C ** -0.5
    for h in range(H_C):
        cs = slice(h * HD_C, (h + 1) * HD_C)
        s = _dot_nt(q_ref[:, cs].astype(BF16), mk_ref[:, cs].astype(BF16)) * scale
        m = jnp.max(s, axis=-1, keepdims=True)
        p = jnp.exp(s - m)
        l = jnp.sum(p, axis=-1, keepdims=True)
        o_ref[:, cs] = _dot(p.astype(BF16), mv_ref[:, cs].astype(BF16)) / l


def _mem_attn_call(qc, mk, mv, batch, seq):
    tq = 512
    return pl.pallas_call(
        _mem_attn_kernel,
        grid=(batch, seq // tq),
        in_specs=[pl.BlockSpec((None, tq, 512), lambda b, i: (b, i, 0)),
                  pl.BlockSpec((None, N_MEM, 512), lambda b, i: (b, 0, 0)),
                  pl.BlockSpec((None, N_MEM, 512), lambda b, i: (b, 0, 0))],
        out_specs=pl.BlockSpec((None, tq, 512), lambda b, i: (b, i, 0)),
        out_shape=jax.ShapeDtypeStruct((batch, seq, 512), F32),
        compiler_params=_cparams(("arbitrary", "arbitrary")),
        name="mem_attn",
    )(qc.reshape(batch, seq, 512), mk.reshape(batch, N_MEM, 512), mv.reshape(batch, N_MEM, 512))


def _head_rows(row, width, head_dim):
    rid = lax.broadcasted_iota(jnp.int32, (8, width), 0)
    hid = lax.broadcasted_iota(jnp.int32, (8, width), 1) // head_dim
    return rid == hid, jnp.where(rid == hid, jnp.broadcast_to(row, (8, width)), 0.0)


def _samp_a_one(q, kn, vn, kt_ref, vt_ref, bias_ref, mult_ref, b0_ref):
    scale = HD_A ** -0.5
    nt = W_A // 128
    own, qh = _head_rows(q, 512, HD_A)
    qcols = jnp.transpose(jnp.broadcast_to(q, (128, 512)))
    s_heads = []
    for h in range(H_A):
        rows = slice(h * HD_A, (h + 1) * HD_A)
        qb = qcols[rows, :]
        s_heads.append(jnp.concatenate(
            [jnp.sum(kt_ref[rows, 128 * j:128 * (j + 1)] * qb, axis=0, keepdims=True) for j in range(nt)], axis=1))
    s = jnp.concatenate(s_heads, axis=0) * scale + bias_ref[...]
    ln = jnp.sum(qh * kn, axis=-1, keepdims=True) * scale + b0_ref[:, 0:1]
    m = jnp.maximum(jnp.max(s, axis=-1, keepdims=True), ln)
    p = jnp.exp(s - m) * mult_ref[...]
    pn = len(A_DILATIONS) * jnp.exp(ln - m)
    den = jnp.sum(p, axis=-1, keepdims=True) + pn
    accs = []
    for h in range(H_A):
        rows = slice(h * HD_A, (h + 1) * HD_A)
        acc = vt_ref[rows, 0:128] * p[h:h + 1, 0:128]
        for j in range(1, nt):
            acc = acc + vt_ref[rows, 128 * j:128 * (j + 1)] * p[h:h + 1, 128 * j:128 * (j + 1)]
        accs.append(acc)
    pv = jnp.sum(jnp.transpose(jnp.concatenate(accs, axis=0)), axis=0, keepdims=True)
    lanes = lambda col: jnp.sum(jnp.where(own, col, 0.0), axis=0, keepdims=True)
    return (pv + lanes(pn) * vn) / lanes(den)


N_RIDER_IN = 8


def _rider_arrays(qa, kn, vn, cache_k, cache_v, bias_sa, mult, s0a):
    nb = qa.shape[0]
    kt = jnp.transpose(cache_k, (0, 2, 3, 1)).reshape(nb, 512, W_A)
    vt = jnp.transpose(cache_v, (0, 2, 3, 1)).reshape(nb, 512, W_A)
    return (qa.reshape(nb, 1, 512), kn.reshape(nb, 1, 512), vn.reshape(nb, 1, 512), kt, vt, bias_sa, mult, s0a)


def _rider_specs(first, step_of):
    row = pl.BlockSpec((None, 1, 512), lambda *ids: (first + step_of(*ids), 0, 0))
    cblk = pl.BlockSpec((None, 512, W_A), lambda *ids: (first + step_of(*ids), 0, 0))
    ins = [row, row, row, cblk, cblk, _const_spec((8, W_A)), _const_spec((1, W_A)), _const_spec((8, 128))]
    return ins, pl.BlockSpec((None, 1, 512), lambda *ids: (step_of(*ids), 0, 0))


def _ride(in_refs, o_ref):
    q_ref, kn_ref, vn_ref, kt_ref, vt_ref, bias_ref, mult_ref, b0_ref = in_refs
    o_ref[...] = _samp_a_one(q_ref[...], kn_ref[...], vn_ref[...], kt_ref, vt_ref, bias_ref, mult_ref, b0_ref)


def _samp_a_kernel(*refs):
    _ride(refs[:N_RIDER_IN], refs[-1])


def _samp_a_call(arrays, first, count):
    r_in, r_out = _rider_specs(first, lambda i: i)
    return pl.pallas_call(
        _samp_a_kernel,
        grid=(count,),
        in_specs=r_in,
        out_specs=r_out,
        out_shape=jax.ShapeDtypeStruct((count, 1, 512), F32),
        compiler_params=_cparams(("arbitrary",)),
        name="sample_dilated_attn",
    )(*arrays)


def _samp_b_kernel(q_ref, kn_ref, vn_ref, kt_ref, vt_ref, bias_ref, b0_ref, sk_ref, o_ref):
    scale = HD_B ** -0.5
    low = lax.broadcasted_iota(jnp.int32, (1, 128), 1) < 64
    rid = lax.broadcasted_iota(jnp.int32, (8, 128), 0)
    kv_half = (lax.broadcasted_iota(jnp.int32, (8, 128), 1) >= 64) == (rid >= 4)

    bb = q_ref.shape[0]
    zero = jnp.zeros((8, 128), F32)
    qhs = []
    for b in range(bb):
        q = q_ref[b:b + 1, :]
        rows = []
        for h in range(H_B):
            chunk = q[:, 128 * (h // 2):128 * (h // 2) + 128]
            if (h % 2) != (h // 4):
                chunk = pltpu.roll(chunk, 64, 1)
            rows.append(chunk)
        qhs.append(jnp.where(kv_half, jnp.concatenate(rows, axis=0), 0.0))
    qblk = jnp.concatenate([jnp.concatenate([qhs[b] if c == b else zero for c in range(bb)], axis=1)
                            for b in range(bb)], axis=0)
    kt = kt_ref[...].reshape(bb * 128, W_B).astype(BF16)
    vt = vt_ref[...].reshape(bb * 128, W_B).astype(BF16)
    tile8 = lambda ref: jnp.concatenate([ref[...]] * bb, axis=0)
    s = _dot(qblk.astype(BF16), kt) * scale + tile8(bias_ref)
    knew = jnp.concatenate([jnp.broadcast_to(kn_ref[b:b + 1, :], (8, 128)) for b in range(bb)], axis=0)
    vnew = jnp.concatenate([jnp.broadcast_to(vn_ref[b:b + 1, :], (8, 128)) for b in range(bb)], axis=0)
    qall = jnp.concatenate(qhs, axis=0)
    ln = jnp.sum(qall * knew, axis=-1, keepdims=True) * scale + tile8(b0_ref)[:, 0:1]
    sk = tile8(sk_ref)[:, 0:1]
    m = jnp.maximum(jnp.maximum(jnp.max(s, axis=-1, keepdims=True), ln), sk)
    p = jnp.exp(s - m)
    pn = jnp.exp(ln - m)
    den = jnp.sum(p, axis=-1, keepdims=True) + pn + jnp.exp(sk - m)
    rall = _dot_nt(p.astype(BF16), vt)
    for b in range(bb):
        r = (rall[8 * b:8 * b + 8, 128 * b:128 * b + 128] + pn[8 * b:8 * b + 8] * vnew[8 * b:8 * b + 8])
        r = jnp.where(kv_half, r / den[8 * b:8 * b + 8], 0.0)
        chunks = []
        for c2 in range(4):
            pair = []
            for h in (2 * c2, 2 * c2 + 1):
                piece = r[h:h + 1, :]
                if (h % 2) != (h // 4):
                    piece = pltpu.roll(piece, 64, 1)
                pair.append(piece)
            chunks.append(jnp.where(low, pair[0], pair[1]))
        o_ref[b:b + 1, :] = jnp.concatenate(chunks, axis=1)


def _samp_b_call(qb, kn, vn, cache_k, cache_v, bias_sb, s0b, skb):
    nb = qb.shape[0]
    bb = 8
    kt = jnp.transpose(cache_k, (0, 2, 3, 1)).reshape(nb, 128, W_B)
    vt = jnp.transpose(cache_v, (0, 2, 3, 1)).reshape(nb, 128, W_B)
    row = lambda w: pl.BlockSpec((bb, w), lambda i: (i, 0))
    cblk = pl.BlockSpec((bb, 128, W_B), lambda i: (i, 0, 0))
    return pl.pallas_call(
        _samp_b_kernel,
        grid=(nb // bb,),
        in_specs=[row(512), row(128), row(128), cblk, cblk, _const_spec((8, W_B)),
                  _const_spec((8, 128)), _const_spec((8, 128))],
        out_specs=row(512),
        out_shape=jax.ShapeDtypeStruct((nb, 512), F32),
        compiler_params=_cparams(("arbitrary",)),
        name="sample_swa_attn",
    )(qb, kn, vn, kt, vt, bias_sb, s0b, skb)


def _samp_c_kernel(q_ref, mk_ref, mv_ref, mask_ref, o_ref):
    scale = HD_C ** -0.5
    zeros = jnp.zeros((4, HD_C), F32)

    def body(b, c):
        q8 = jnp.concatenate([q_ref[b], zeros], axis=0).astype(BF16)
        s = _dot_nt(q8, mk_ref[b].astype(BF16)) * scale + mask_ref[...]
        m = jnp.max(s, axis=-1, keepdims=True)
        p = jnp.exp(s - m)
        den = jnp.sum(p, axis=-1, keepdims=True)
        r = _dot(p.astype(BF16), mv_ref[b].astype(BF16)) / den
        o_ref[b] = r[0:4, :]
        return c

    lax.fori_loop(0, q_ref.shape[0], body, 0, unroll=True)


def _samp_c_call(qc, cache_k, cache_v):
    nb = qc.shape[0]
    bb = 8
    rows = N_MEM * H_C
    head_of_row = np.arange(rows)[None, :] % H_C
    mask = np.where(head_of_row == np.arange(8)[:, None], 0.0, NEG).astype(np.float32)
    qblk = pl.BlockSpec((bb, H_C, HD_C), lambda i: (i, 0, 0))
    cblk = pl.BlockSpec((bb, rows, HD_C), lambda i: (i, 0, 0))
    out = pl.pallas_call(
        _samp_c_kernel,
        grid=(nb // bb,),
        in_specs=[qblk, cblk, cblk, _const_spec((8, rows))],
        out_specs=qblk,
        out_shape=jax.ShapeDtypeStruct((nb, H_C, HD_C), F32),
        compiler_params=_cparams(("arbitrary",)),
        name="sample_mem_attn",
    )(qc.reshape(nb, H_C, HD_C), cache_k.reshape(nb, rows, HD_C), cache_v.reshape(nb, rows, HD_C),
      jnp.asarray(mask))
    return out.reshape(nb, 512)


def kernel(x_prompt, x_sample, mem_prompt, cache_a_k, cache_a_v, cache_b_k, cache_b_v, cache_mem_k,
           cache_mem_v, rel_bias, ln_g, w_in, gq_a, gk_a, gq_b, gk_b, gq_c, gk_c, sinks_b, mem_ln_g,
           w_mem_kv, w_br_a, w_br_b, w_br_c, w_out):
    batch, seq, _ = x_prompt.shape
    nsamp = x_sample.shape[0]
    assert ln_g.shape[0] == 1 and x_sample.shape[1] == 1
    assert (batch, seq, nsamp) == (8, 2048, 128) and w_in.shape == (1, D_MODEL, C_END)
    assert cache_a_k.shape == (1, nsamp, W_A, H_A, HD_A) and cache_b_k.shape == (1, nsamp, W_B, KV_B, HD_B)

    w_bf = w_in[0].astype(BF16)
    wmem_bf = w_mem_kv[0].astype(BF16)
    wbr_bf = jnp.stack([w_br_a[0], w_br_b[0], w_br_c[0]]).astype(BF16)
    wout_bf = w_out[0].astype(BF16)
    lng = ln_g.reshape(1, D_MODEL)
    gains = jnp.stack([jnp.tile(gq_a[0], 8), jnp.tile(gk_a[0], 8), jnp.tile(gq_b[0], 8),
                       jnp.tile(gk_b[0], 8), jnp.tile(gq_c[0], 4)])
    gk_c4 = jnp.tile(gk_c[0], 4).reshape(1, 512)
    blockdiag = np.kron(np.eye(4, dtype=np.float32), np.ones((64, 64), np.float32))
    pmat = jnp.asarray(blockdiag, BF16)

    *tables, mult_a = _bucket_tables()
    bias_a, bias_b, bias_sa, bias_sb, s0a, s0b, skb = _bias_call(rel_bias, sinks_b.reshape(H_B), tables)

    xs2 = x_sample.reshape(nsamp, D_MODEL)
    sqa, ska, sva, sqb, skb_new, svb, sqc = _qkv_call(xs2, lng, w_bf, gains, pmat)
    riders = _rider_arrays(sqa, ska, sva, cache_a_k[0], cache_a_v[0], bias_sa, jnp.asarray(mult_a), s0a)
    out_tm = 256
    n_qkv = batch * seq // 512
    n_dil = batch * (H_A // 2)
    n_out = batch * seq // out_tm
    n_alone = nsamp - n_qkv - n_dil - n_out
    assert n_alone >= 0

    xp2 = x_prompt.reshape(batch * seq, D_MODEL)
    qa, ka, va, qb, kb, vb, qc, ka_t, va_t, soa_1 = _qkv_call(xp2, lng, w_bf, gains, pmat, (batch, seq),
                                                             rider=(riders, 0))
    mk, mv = _memkv_call(mem_prompt.reshape(batch * N_MEM, D_MODEL), mem_ln_g.reshape(1, D_MODEL),
                         wmem_bf, gk_c4)
    oa, soa_2 = _dil_call(qa, ka, va, bias_a, batch, seq, rider=(riders, n_qkv))
    ob = _swa_call(sinks_b.reshape(H_B), qb, kb, vb, bias_b, batch, seq)
    oc = _mem_attn_call(qc, mk, mv, batch, seq)
    yp, soa_3 = _out_call(xp2, oa.reshape(-1, 512), ob.reshape(-1, 512), oc.reshape(-1, 512),
                          lng, w_bf, wbr_bf, wout_bf, tm=out_tm, rider=(riders, n_qkv + n_dil))
    yp = yp.reshape(batch, seq, D_MODEL)

    soa_parts = [soa_1, soa_2, soa_3]
    if n_alone:
        soa_parts.append(_samp_a_call(riders, n_qkv + n_dil + n_out, n_alone))
    soa = jnp.concatenate(soa_parts, axis=0).reshape(nsamp, 512)
    sob = _samp_b_call(sqb, skb_new, svb, cache_b_k[0], cache_b_v[0], bias_sb, s0b, skb)
    soc = _samp_c_call(sqc, cache_mem_k[0], cache_mem_v[0])
    ys, = _out_call(xs2, soa, sob, soc, lng, w_bf, wbr_bf, wout_bf)

    return (yp, ys.reshape(nsamp, 1, D_MODEL),
            jnp.transpose(ka_t.reshape(1, batch, H_A, HD_A, seq), (0, 1, 4, 2, 3)),
            jnp.transpose(va_t.reshape(1, batch, H_A, HD_A, seq), (0, 1, 4, 2, 3)),
            kb.reshape(batch, seq, KV_B, HD_B)[None, :, seq - W_B:],
            vb.reshape(batch, seq, KV_B, HD_B)[None, :, seq - W_B:],
            mk.reshape(1, batch, N_MEM, H_C, HD_C), mv.reshape(1, batch, N_MEM, H_C, HD_C),
            ska.reshape(1, nsamp, 1, H_A, HD_A), sva.reshape(1, nsamp, 1, H_A, HD_A),
            skb_new.reshape(1, nsamp, 1, KV_B, HD_B), svb.reshape(1, nsamp, 1, KV_B, HD_B))
```

```python
import functools
import math

import numpy as np
import jax
import jax.numpy as jnp
from jax import lax
from jax.experimental import pallas as pl
from jax.experimental.pallas import tpu as pltpu

F32 = jnp.float32
BF16 = jnp.bfloat16

D_MODEL = 1024
H_A, HD_A = 8, 64
A_DILATIONS = (1, 4, 16)
A_STEPS = 128
W_A = 2048
H_B, KV_B, HD_B, W_B = 8, 2, 64, 128
H_C, HD_C, N_MEM = 4, 128, 256
NUM_BUCKETS, MAX_DISTANCE = 32, 2048
EPS = 1e-6
NEG = -1e30
BLK = 128
UNROLL = 8

C_QA, C_KA, C_VA, C_ZA = 0, 512, 1024, 1536
C_QB, C_KB, C_VB, C_ZB = 2048, 2560, 2688, 2816
C_QC, C_ZC = 3328, 3840
C_GA, C_GB, C_GC, C_END = 4352, 5376, 6400, 7424

VMEM_LIMIT = 56 * 1024 * 1024


def _cparams(sem):
    return pltpu.CompilerParams(dimension_semantics=sem, vmem_limit_bytes=VMEM_LIMIT)


def _const_spec(shape):
    nd = len(shape)
    return pl.BlockSpec(shape, lambda *_: (0,) * nd, pipeline_mode=pl.Buffered(1))


def _t5_bucket_np(dist):
    n = np.maximum(dist, 0)
    max_exact = NUM_BUCKETS // 2
    nf = np.maximum(n, 1).astype(np.float32)
    large = max_exact + (np.log(nf / np.float32(max_exact))
                         / np.float32(math.log(MAX_DISTANCE / max_exact))
                         * np.float32(NUM_BUCKETS - max_exact)).astype(np.int32)
    return np.where(n < max_exact, n, np.minimum(large, NUM_BUCKETS - 1)).astype(np.int32)


def _a_multiplicity(dist):
    return sum(((dist % d == 0) & (dist <= A_STEPS * d)).astype(np.int32) for d in A_DILATIONS)


def _bucket_tables():
    i = np.arange(BLK)[:, None]
    j = np.arange(2 * BLK)[None, :]
    rel = i + BLK - j
    bkt_a = np.stack([np.where((rel >= 0) & (rel <= A_STEPS), _t5_bucket_np(rel * d), -1)
                      for d in A_DILATIONS]).astype(np.int32)
    bkt_b = np.where((rel >= 0) & (rel < W_B), _t5_bucket_np(rel), -1).astype(np.int32)
    dist_a = W_A - np.arange(W_A)
    mult_a = _a_multiplicity(dist_a)
    sb_a = np.broadcast_to(np.where(mult_a > 0, _t5_bucket_np(dist_a), -1), (8, W_A)).astype(np.int32)
    dist_b = W_B - np.arange(W_B)
    sb_b = np.broadcast_to(np.where(dist_b < W_B, _t5_bucket_np(dist_b), -1), (8, W_B)).astype(np.int32)
    return bkt_a, bkt_b, sb_a, sb_b, mult_a.astype(np.float32).reshape(1, W_A)


def _rms_rows(x, g):
    return x * lax.rsqrt(jnp.mean(x * x, axis=-1, keepdims=True) + EPS) * g


def _dot(a, b):
    return jnp.dot(a, b, preferred_element_type=F32)


def _dot_nt(a, b):
    return lax.dot_general(a, b, (((1,), (1,)), ((), ())), preferred_element_type=F32)


def _headnorm64(t, g, pmat):
    sq = (t * t).astype(BF16)
    width = t.shape[1]
    step = min(width, 256)
    pp = pmat[:step, :step]
    parts = [_dot(sq[:, c:c + step], pp) for c in range(0, width, step)]
    ss = parts[0] if len(parts) == 1 else jnp.concatenate(parts, axis=1)
    return t * lax.rsqrt(ss * (1.0 / 64.0) + EPS) * g


def _headnorm128(t, g):
    parts = []
    for c in range(0, t.shape[1], 128):
        seg = t[:, c:c + 128]
        parts.append(seg * lax.rsqrt(jnp.mean(seg * seg, axis=-1, keepdims=True) + EPS))
    return jnp.concatenate(parts, axis=1) * g


def _bias_kernel(tbl_ref, sink_ref, bkta_ref, bktb_ref, sba_ref, sbb_ref,
                 ba_ref, bb_ref, oa_ref, ob_ref, s0a_ref, s0b_ref, sk_ref, *, present):
    pa, pb, psa, psb = present
    for g in range(3):
        bk = bkta_ref[g]
        for h in range(H_A):
            ba_ref[g, h] = jnp.full((BLK, 2 * BLK), NEG, F32)
        for b in pa[g]:
            hit = bk == b
            for h in range(H_A):
                ba_ref[g, h] = jnp.where(hit, tbl_ref[b, h], ba_ref[g, h])
    bk = bktb_ref[...]
    for h in range(H_B):
        bb_ref[h] = jnp.full((BLK, 2 * BLK), NEG, F32)
    for b in pb:
        hit = bk == b
        for h in range(H_B):
            bb_ref[h] = jnp.where(hit, tbl_ref[b, H_A + h], bb_ref[h])
    for (src, dst, buckets, col0) in ((sba_ref, oa_ref, psa, 0), (sbb_ref, ob_ref, psb, H_A)):
        bk = src[...]
        row = lax.broadcasted_iota(jnp.int32, bk.shape, 0)
        acc = jnp.full(bk.shape, NEG, F32)
        for b in buckets:
            hit = bk == b
            for h in range(8):
                acc = jnp.where(hit & (row == h), tbl_ref[b, col0 + h], acc)
        dst[...] = acc
    row8 = lax.broadcasted_iota(jnp.int32, (8, 128), 0)
    a0 = jnp.zeros((8, 128), F32)
    b0 = jnp.zeros((8, 128), F32)
    sk = jnp.zeros((8, 128), F32)
    for h in range(8):
        a0 = jnp.where(row8 == h, tbl_ref[0, h], a0)
        b0 = jnp.where(row8 == h, tbl_ref[0, H_A + h], b0)
        sk = jnp.where(row8 == h, sink_ref[h], sk)
    s0a_ref[...] = a0
    s0b_ref[...] = b0
    sk_ref[...] = sk


def _bias_call(rel_bias, sinks, tables):
    bkt_a, bkt_b, sb_a, sb_b = tables
    uniq = lambda a: tuple(int(b) for b in np.unique(a) if b >= 0)
    present = (tuple(uniq(bkt_a[g]) for g in range(3)), uniq(bkt_b), uniq(sb_a), uniq(sb_b))
    smem = pl.BlockSpec(memory_space=pltpu.SMEM)
    vmem = pl.BlockSpec(memory_space=pltpu.VMEM)
    return pl.pallas_call(
        functools.partial(_bias_kernel, present=present),
        in_specs=[smem, smem, vmem, vmem, vmem, vmem],
        out_specs=[vmem] * 7,
        out_shape=[jax.ShapeDtypeStruct((3, H_A, BLK, 2 * BLK), F32),
                   jax.ShapeDtypeStruct((H_B, BLK, 2 * BLK), F32),
                   jax.ShapeDtypeStruct((8, W_A), F32),
                   jax.ShapeDtypeStruct((8, W_B), F32),
                   jax.ShapeDtypeStruct((8, 128), F32),
                   jax.ShapeDtypeStruct((8, 128), F32),
                   jax.ShapeDtypeStruct((8, 128), F32)],
        compiler_params=pltpu.CompilerParams(vmem_limit_bytes=VMEM_LIMIT),
        name="bias_expand",
    )(rel_bias, sinks, jnp.asarray(bkt_a), jnp.asarray(bkt_b), jnp.asarray(sb_a), jnp.asarray(sb_b))


def _qkv_kernel(x_ref, lng_ref, w_ref, g_ref, p_ref, *refs, rider=False, tiles_per_seq=1):
    if rider:
        _ride(refs[:N_RIDER_IN], refs[-1])
        refs = refs[N_RIDER_IN:-1]
    qa_ref, ka_ref, va_ref, qb_ref, kb_ref, vb_ref, qc_ref, *cache_refs = refs
    h = _rms_rows(x_ref[...], lng_ref[...]).astype(BF16)
    pmat = p_ref[...]

    def proj(c0, c1):
        return _dot(h, w_ref[:, c0:c1])

    qa_ref[...] = _headnorm64(proj(C_QA, C_KA), g_ref[0:1, :], pmat)
    ka = _headnorm64(proj(C_KA, C_VA), g_ref[1:2, :], pmat)
    va = proj(C_VA, C_ZA)
    ka_ref[...] = ka
    va_ref[...] = va
    kb = _headnorm64(proj(C_KB, C_VB), g_ref[3:4, 0:128], pmat)
    vb = proj(C_VB, C_ZB)
    if cache_refs:
        kt_ref, vt_ref, kbt_ref, vbt_ref = cache_refs
        kt_ref[...] = ka.T
        vt_ref[...] = va.T

        @pl.when((pl.program_id(0) + 1) % tiles_per_seq == 0)
        def _():
            tm = kb.shape[0]
            kbt_ref[...] = kb[tm - W_B:, :].T
            vbt_ref[...] = vb[tm - W_B:, :].T
    qb_ref[...] = _headnorm64(proj(C_QB, C_KB), g_ref[2:3, :], pmat).astype(qb_ref.dtype)
    kb_ref[...] = kb
    vb_ref[...] = vb
    qc_ref[...] = _headnorm128(proj(C_QC, C_ZC), g_ref[4:5, :]).astype(qc_ref.dtype)


def _qkv_call(x2, lng, w_bf, gains, pmat, cache_layout_for=None, rider=None):
    n = x2.shape[0]
    tm = min(512, n)
    steps = n // tm
    row = lambda w: pl.BlockSpec((tm, w), lambda i: (i, 0))
    widths = (512, 512, 512, 512, 128, 128, 512)
    in_specs = [row(D_MODEL), _const_spec((1, D_MODEL)), _const_spec((D_MODEL, C_END)),
                _const_spec((5, 512)), _const_spec((256, 256))]
    args = [x2, lng, w_bf, gains, pmat]
    out_specs = [row(w) for w in widths]
    q_dtype = F32 if cache_layout_for is None else BF16
    out_shape = [jax.ShapeDtypeStruct((n, w), q_dtype if i in (3, 6) else F32) for i, w in enumerate(widths)]
    per = 1
    if cache_layout_for is not None:
        batch, seq = cache_layout_for
        per = seq // tm
        out_specs += [pl.BlockSpec((None, 512, tm), lambda i: (i // per, 0, i % per))] * 2
        out_shape += [jax.ShapeDtypeStruct((batch, 512, seq), F32)] * 2
        out_specs += [pl.BlockSpec((None, KV_B * HD_B, W_B), lambda i: (i // per, 0, 0))] * 2
        out_shape += [jax.ShapeDtypeStruct((batch, KV_B * HD_B, W_B), F32)] * 2
    if rider is not None:
        r_in, r_out = _rider_specs(rider[1], lambda i: i)
        in_specs += r_in
        args += list(rider[0])
        out_specs.append(r_out)
        out_shape.append(jax.ShapeDtypeStruct((steps, 1, 512), F32))
    return pl.pallas_call(
        functools.partial(_qkv_kernel, rider=rider is not None, tiles_per_seq=per),
        grid=(steps,),
        in_specs=in_specs,
        out_specs=out_specs,
        out_shape=out_shape,
        compiler_params=_cparams(("arbitrary",)),
        name="qkv_proj",
    )(*args)


def _memkv_kernel(m_ref, g_ref, w_ref, gk_ref, mk_ref, mv_ref):
    h = _rms_rows(m_ref[...], g_ref[...]).astype(BF16)
    mk_ref[...] = _headnorm128(_dot(h, w_ref[:, 0:512]), gk_ref[...])
    mv_ref[...] = _dot(h, w_ref[:, 512:1024])


def _memkv_call(mem2, g, w_bf, gk):
    n = mem2.shape[0]
    tm = 512
    row = lambda w: pl.BlockSpec((tm, w), lambda i: (i, 0))
    return pl.pallas_call(
        _memkv_kernel,
        grid=(n // tm,),
        in_specs=[row(D_MODEL), _const_spec((1, D_MODEL)), _const_spec((D_MODEL, 2 * 512)),
                  _const_spec((1, 512))],
        out_specs=[row(512), row(512)],
        out_shape=[jax.ShapeDtypeStruct((n, 512), F32)] * 2,
        compiler_params=_cparams(("arbitrary",)),
        name="mem_kv_proj",
    )(mem2, g, w_bf, gk)


def _out_kernel(x_ref, oa_ref, ob_ref, oc_ref, lng_ref, w_ref, wbr_ref, wout_ref, *refs):
    if len(refs) > 1:
        _ride(refs[:N_RIDER_IN], refs[-1])
    y_ref = refs[-2] if len(refs) > 1 else refs[0]
    x = x_ref[...]
    h = _rms_rows(x, lng_ref[...]).astype(BF16)
    acc = None
    for br, (o_ref, cz, cg) in enumerate(((oa_ref, C_ZA, C_GA), (ob_ref, C_ZB, C_GB), (oc_ref, C_ZC, C_GC))):
        z = _dot(h, w_ref[:, cz:cz + 512])
        y = (o_ref[...] * (z * jax.nn.sigmoid(z))).astype(BF16)
        yb = _dot(y, wbr_ref[br])
        gate = jax.nn.sigmoid(_dot(h, w_ref[:, cg:cg + D_MODEL]))
        acc = gate * yb if acc is None else acc + gate * yb
    y_ref[...] = x + _dot(acc.astype(BF16), wout_ref[...])


def _out_call(x2, oa, ob, oc, lng, w_bf, wbr_bf, wout_bf, tm=512, rider=None):
    n = x2.shape[0]
    tm = min(tm, n)
    steps = n // tm
    row = lambda w: pl.BlockSpec((tm, w), lambda i: (i, 0))
    in_specs = [row(D_MODEL), row(512), row(512), row(512), _const_spec((1, D_MODEL)),
                _const_spec((D_MODEL, C_END)), _const_spec((3, 512, D_MODEL)), _const_spec((D_MODEL, D_MODEL))]
    args = [x2, oa, ob, oc, lng, w_bf, wbr_bf, wout_bf]
    out_specs = [row(D_MODEL)]
    out_shape = [jax.ShapeDtypeStruct((n, D_MODEL), F32)]
    if rider is not None:
        r_in, r_out = _rider_specs(rider[1], lambda i: i)
        in_specs += r_in
        args += list(rider[0])
        out_specs.append(r_out)
        out_shape.append(jax.ShapeDtypeStruct((steps, 1, 512), F32))
    return pl.pallas_call(
        _out_kernel,
        grid=(steps,),
        in_specs=in_specs,
        out_specs=out_specs,
        out_shape=out_shape,
        compiler_params=_cparams(("arbitrary",)),
        name="gated_out",
    )(*args)


PAD = BLK
CLS_PITCH = BLK + 8


def _low_lanes(rows):
    return lax.broadcasted_iota(jnp.int32, (rows, 128), 1) < 64


def _aligned(start):
    return start if isinstance(start, int) else pl.multiple_of(start, BLK)


def _pair_tile(q0, q1, kblk, v0, v1, b0, b1, sinks=None):
    low = _low_lanes(BLK)
    ups, ms = [], []
    for hh, (q, v, bias) in enumerate(((q0, v0, b0), (q1, v1, b1))):
        s = _dot_nt(q, kblk) + bias
        m = jnp.max(s, axis=-1, keepdims=True)
        if sinks is not None:
            m = jnp.maximum(m, sinks[hh])
        ups.append(_dot(jnp.exp(s - m).astype(BF16), v))
        ms.append(m)
    u = jnp.where(low, ups[0], ups[1])
    l = pltpu.roll(jnp.where(low, ups[1], ups[0]), 64, 1)
    return u, jnp.where(low, ms[0], ms[1]), l


def _first_bias(bias, also=None):
    gone = lax.broadcasted_iota(jnp.int32, bias.shape, 1) < BLK
    if also is not None:
        gone = gone & also
    return jnp.where(gone, NEG, bias)


def _dil_kernel(q_ref, k_ref, v_ref, bias_ref, *refs, rider=False):
    if rider:
        _ride(refs[:N_RIDER_IN], refs[N_RIDER_IN + 1])
        refs = refs[N_RIDER_IN:N_RIDER_IN + 1] + refs[N_RIDER_IN + 2:]
    o_ref, qs_ref, kc_ref, vc_ref, c4_ref, ru_ref, rm_ref, rl_ref = refs
    scale = HD_A ** -0.5
    seq = q_ref.shape[0]
    ntile = seq // BLK
    low = _low_lanes(BLK)
    d1, d2 = A_DILATIONS[1], A_DILATIONS[2]

    def src_rows(g, t):
        d = A_DILATIONS[g]
        if d == 1:
            return pl.ds(_aligned(t * BLK), BLK)
        nblk = ntile // d
        r, n = t >> (nblk.bit_length() - 1), t & (nblk - 1)
        return pl.ds(r + n * (d * BLK), BLK, stride=d)

    zero = jnp.zeros((PAD, 128), BF16)
    for g in range(3):
        kc_ref[g, 0:PAD, :] = zero
        vc_ref[g, 0, 0:PAD, :] = zero
        vc_ref[g, 1, 0:PAD, :] = zero

        def stage(t, c, g=g):
            dst = pl.ds(pl.multiple_of(t * BLK, BLK), BLK)
            dstp = pl.ds(pl.multiple_of(PAD + t * BLK, BLK), BLK)
            if g < 2:
                src = src_rows(g, t)
                q, k, v = q_ref[src, :], k_ref[src, :], v_ref[src, :]
            else:
                src = pl.ds((t & (d1 - 1)) * (seq // d1) + (t >> (d1.bit_length() - 1)), BLK, stride=d2 // d1)
                q, k, v = c4_ref[0, src, :], c4_ref[1, src, :], c4_ref[2, src, :]
            if g == 1:
                c4_ref[0, dst, :], c4_ref[1, dst, :], c4_ref[2, dst, :] = q, k, v
            q = q * scale
            qs_ref[g, 0, dst, :] = jnp.where(low, q, 0.0).astype(BF16)
            qs_ref[g, 1, dst, :] = jnp.where(low, 0.0, q).astype(BF16)
            kc_ref[g, dstp, :] = k.astype(BF16)
            vc_ref[g, 0, dstp, :] = jnp.where(low, v, 1.0).astype(BF16)
            vc_ref[g, 1, dstp, :] = jnp.where(low, 1.0, v).astype(BF16)
            return c

        lax.fori_loop(0, ntile, stage, 0, unroll=4)

    def tiles(g, i):
        nblk = ntile // A_DILATIONS[g]
        res = []
        for j in range(UNROLL):
            t = i * UNROLL + j
            qrows = pl.ds(_aligned(t * BLK), BLK)
            krows = pl.ds(_aligned(t * BLK), 2 * BLK)
            bias = [bias_ref[g, hh] for hh in range(2)]
            if nblk <= UNROLL and j % nblk == 0:
                bias = [_first_bias(b) for b in bias]
            elif nblk > UNROLL and j == 0:
                bias = [_first_bias(b, i == 0) for b in bias]
            res.append(_pair_tile(qs_ref[g, 0, qrows, :], qs_ref[g, 1, qrows, :], kc_ref[g, krows, :],
                                  vc_ref[g, 0, krows, :], vc_ref[g, 1, krows, :], bias[0], bias[1]))
        return res

    for g in range(3):
        def body(i, c, g=g):
            res = tiles(g, i)
            for j in range(UNROLL):
                t = i * UNROLL + j
                rows = pl.ds(pl.multiple_of(t * CLS_PITCH, 8), BLK) if g == 2 else src_rows(g, t)
                ru_ref[g, rows, :], rm_ref[g, rows, :], rl_ref[g, rows, :] = res[j]
            return c

        lax.fori_loop(0, ntile // UNROLL, body, 0)

    def natural_rows(ref, i):
        parts = [ref[2, pl.ds((v % 2) * 8 * CLS_PITCH + i * (BLK // d2) + v // 2, 8, stride=CLS_PITCH), :]
                 for v in range(BLK // 8)]
        return jnp.concatenate(parts, axis=0)

    def combine(i, c):
        rows = pl.ds(pl.multiple_of(i * BLK, BLK), BLK)
        ms = [rm_ref[0, rows, :], rm_ref[1, rows, :], natural_rows(rm_ref, i)]
        mx = jnp.maximum(jnp.maximum(ms[0], ms[1]), ms[2])
        ws = [jnp.exp(m - mx) for m in ms]
        num = ws[0] * ru_ref[0, rows, :] + ws[1] * ru_ref[1, rows, :] + ws[2] * natural_rows(ru_ref, i)
        den = ws[0] * rl_ref[0, rows, :] + ws[1] * rl_ref[1, rows, :] + ws[2] * natural_rows(rl_ref, i)
        o_ref[rows, :] = num / den
        return c

    lax.fori_loop(0, ntile, combine, 0, unroll=2)


def _dil_call(qa, ka, va, bias_a, batch, seq, rider=None):
    assert seq == A_DILATIONS[2] * BLK and UNROLL % (seq // (A_DILATIONS[1] * BLK)) == 0
    blk = pl.BlockSpec((None, seq, 128), lambda b, p: (b, 0, p))
    pairs = H_A // 2
    in_specs = [blk, blk, blk, pl.BlockSpec((3, 2, BLK, 2 * BLK), lambda b, p: (0, p, 0, 0))]
    args = [qa.reshape(batch, seq, 512), ka.reshape(batch, seq, 512), va.reshape(batch, seq, 512), bias_a]
    out_specs = [blk]
    out_shape = [jax.ShapeDtypeStruct((batch, seq, 512), F32)]
    if rider is not None:
        r_in, r_out = _rider_specs(rider[1], lambda b, p: b * pairs + p)
        in_specs += r_in
        args += list(rider[0])
        out_specs.append(r_out)
        out_shape.append(jax.ShapeDtypeStruct((batch * pairs, 1, 512), F32))
    return pl.pallas_call(
        functools.partial(_dil_kernel, rider=rider is not None),
        grid=(batch, pairs),
        in_specs=in_specs,
        out_specs=out_specs,
        out_shape=out_shape,
        scratch_shapes=[pltpu.VMEM((3, 2, seq, 128), BF16), pltpu.VMEM((3, PAD + seq, 128), BF16),
                        pltpu.VMEM((3, 2, PAD + seq, 128), BF16), pltpu.VMEM((3, seq, 128), F32)]
                       + [pltpu.VMEM((3, A_DILATIONS[2] * CLS_PITCH, 128), F32)] * 3,
        compiler_params=_cparams(("arbitrary", "arbitrary")),
        name="dilated_attn",
    )(*args)


def _swa_kernel(sink_ref, q_ref, k_ref, v_ref, bias_ref, o_ref, qs_ref, kc_ref, vc_ref):
    scale = HD_B ** -0.5
    seq = q_ref.shape[0]
    ntile = seq // BLK
    p = pl.program_id(1)
    kv = p // 2
    sinks = (sink_ref[2 * p], sink_ref[2 * p + 1])
    low = _low_lanes(BLK)
    own_half = (lax.broadcasted_iota(jnp.int32, (BLK, 128), 1) >= 64).astype(jnp.int32) == kv

    def dup(xf):
        return jnp.where(own_half, xf, pltpu.roll(xf, 64, 1))

    zero = jnp.zeros((PAD, 128), BF16)
    kc_ref[0:PAD, :] = zero
    vc_ref[0, 0:PAD, :] = zero
    vc_ref[1, 0:PAD, :] = zero

    def stage(t, c):
        src = pl.ds(pl.multiple_of(t * BLK, BLK), BLK)
        dstp = pl.ds(pl.multiple_of(PAD + t * BLK, BLK), BLK)
        q = q_ref[src, :] * scale
        qs_ref[0, src, :] = jnp.where(low, q, 0.0).astype(BF16)
        qs_ref[1, src, :] = jnp.where(low, 0.0, q).astype(BF16)
        kc_ref[dstp, :] = dup(k_ref[src, :]).astype(BF16)
        v = dup(v_ref[src, :])
        vc_ref[0, dstp, :] = jnp.where(low, v, 1.0).astype(BF16)
        vc_ref[1, dstp, :] = jnp.where(low, 1.0, v).astype(BF16)
        return c

    lax.fori_loop(0, ntile, stage, 0, unroll=4)

    def body(i, c):
        res = []
        for j in range(UNROLL):
            t = i * UNROLL + j
            qrows = pl.ds(pl.multiple_of(t * BLK, BLK), BLK)
            krows = pl.ds(pl.multiple_of(t * BLK, BLK), 2 * BLK)
            bias = [bias_ref[hh] for hh in range(2)]
            if j == 0:
                bias = [_first_bias(b, i == 0) for b in bias]
            u, m, l = _pair_tile(qs_ref[0, qrows, :], qs_ref[1, qrows, :], kc_ref[krows, :],
                                 vc_ref[0, krows, :], vc_ref[1, krows, :], bias[0], bias[1], sinks)
            sk = jnp.where(low, sinks[0], sinks[1])
            res.append(u / (l + jnp.exp(sk - m)))
        for j in range(UNROLL):
            o_ref[pl.ds(pl.multiple_of((i * UNROLL + j) * BLK, BLK), BLK), :] = res[j]
        return c

    lax.fori_loop(0, ntile // UNROLL, body, 0)


def _swa_call(sinks, qb, kb, vb, bias_b, batch, seq):
    qblk = pl.BlockSpec((None, seq, 128), lambda b, p: (b, 0, p))
    kblk = pl.BlockSpec((None, seq, 128), lambda b, p: (b, 0, 0))
    return pl.pallas_call(
        _swa_kernel,
        grid=(batch, H_B // 2),
        in_specs=[pl.BlockSpec(memory_space=pltpu.SMEM), qblk, kblk, kblk,
                  pl.BlockSpec((2, BLK, 2 * BLK), lambda b, p: (p, 0, 0))],
        out_specs=qblk,
        out_shape=jax.ShapeDtypeStruct((batch, seq, 512), F32),
        scratch_shapes=[pltpu.VMEM((2, seq, 128), BF16), pltpu.VMEM((PAD + seq, 128), BF16),
                        pltpu.VMEM((2, PAD + seq, 128), BF16)],
        compiler_params=_cparams(("arbitrary", "arbitrary")),
        name="swa_attn",
    )(sinks, qb.reshape(batch, seq, 512), kb.reshape(batch, seq, 128), vb.reshape(batch, seq, 128), bias_b)


def _mem_attn_kernel(q_ref, mk_ref, mv_ref, o_ref):
    scale = HD_C ** -0.5
    for h in range(H_C):
        cs = slice(h * HD_C, (h + 1) * HD_C)
        s = _dot_nt(q_ref[:, cs].astype(BF16), mk_ref[:, cs].astype(BF16)) * scale
        m = jnp.max(s, axis=-1, keepdims=True)
        p = jnp.exp(s - m)
        l = jnp.sum(p, axis=-1, keepdims=True)
        o_ref[:, cs] = _dot(p.astype(BF16), mv_ref[:, cs].astype(BF16)) / l


def _mem_attn_call(qc, mk, mv, batch, seq):
    tq = 512
    return pl.pallas_call(
        _mem_attn_kernel,
        grid=(batch, seq // tq),
        in_specs=[pl.BlockSpec((None, tq, 512), lambda b, i: (b, i, 0)),
                  pl.BlockSpec((None, N_MEM, 512), lambda b, i: (b, 0, 0)),
                  pl.BlockSpec((None, N_MEM, 512), lambda b, i: (b, 0, 0))],
        out_specs=pl.BlockSpec((None, tq, 512), lambda b, i: (b, i, 0)),
        out_shape=jax.ShapeDtypeStruct((batch, seq, 512), F32),
        compiler_params=_cparams(("arbitrary", "arbitrary")),
        name="mem_attn",
    )(qc.reshape(batch, seq, 512), mk.reshape(batch, N_MEM, 512), mv.reshape(batch, N_MEM, 512))


def _head_rows(row, width, head_dim):
    rid = lax.broadcasted_iota(jnp.int32, (8, width), 0)
    hid = lax.broadcasted_iota(jnp.int32, (8, width), 1) // head_dim
    return rid == hid, jnp.where(rid == hid, jnp.broadcast_to(row, (8, width)), 0.0)


def _samp_a_one(q, kn, vn, kt_ref, vt_ref, bias_ref, mult_ref, b0_ref):
    scale = HD_A ** -0.5
    nt = W_A // 128
    own, qh = _head_rows(q, 512, HD_A)
    qcols = jnp.transpose(jnp.broadcast_to(q, (128, 512)))
    s_heads = []
    for h in range(H_A):
        rows = slice(h * HD_A, (h + 1) * HD_A)
        qb = qcols[rows, :]
        s_heads.append(jnp.concatenate(
            [jnp.sum(kt_ref[rows, 128 * j:128 * (j + 1)] * qb, axis=0, keepdims=True) for j in range(nt)], axis=1))
    s = jnp.concatenate(s_heads, axis=0) * scale + bias_ref[...]
    ln = jnp.sum(qh * kn, axis=-1, keepdims=True) * scale + b0_ref[:, 0:1]
    m = jnp.maximum(jnp.max(s, axis=-1, keepdims=True), ln)
    p = jnp.exp(s - m) * mult_ref[...]
    pn = len(A_DILATIONS) * jnp.exp(ln - m)
    den = jnp.sum(p, axis=-1, keepdims=True) + pn
    accs = []
    for h in range(H_A):
        rows = slice(h * HD_A, (h + 1) * HD_A)
        acc = vt_ref[rows, 0:128] * p[h:h + 1, 0:128]
        for j in range(1, nt):
            acc = acc + vt_ref[rows, 128 * j:128 * (j + 1)] * p[h:h + 1, 128 * j:128 * (j + 1)]
        accs.append(acc)
    pv = jnp.sum(jnp.transpose(jnp.concatenate(accs, axis=0)), axis=0, keepdims=True)
    lanes = lambda col: jnp.sum(jnp.where(own, col, 0.0), axis=0, keepdims=True)
    return (pv + lanes(pn) * vn) / lanes(den)


N_RIDER_IN = 8


def _rider_arrays(qa, kn, vn, cache_k, cache_v, bias_sa, mult, s0a):
    nb = qa.shape[0]
    kt = jnp.transpose(cache_k, (0, 2, 3, 1)).reshape(nb, 512, W_A)
    vt = jnp.transpose(cache_v, (0, 2, 3, 1)).reshape(nb, 512, W_A)
    return (qa.reshape(nb, 1, 512), kn.reshape(nb, 1, 512), vn.reshape(nb, 1, 512), kt, vt, bias_sa, mult, s0a)


def _rider_specs(first, step_of):
    row = pl.BlockSpec((None, 1, 512), lambda *ids: (first + step_of(*ids), 0, 0))
    cblk = pl.BlockSpec((None, 512, W_A), lambda *ids: (first + step_of(*ids), 0, 0))
    ins = [row, row, row, cblk, cblk, _const_spec((8, W_A)), _const_spec((1, W_A)), _const_spec((8, 128))]
    return ins, pl.BlockSpec((None, 1, 512), lambda *ids: (step_of(*ids), 0, 0))


def _ride(in_refs, o_ref):
    q_ref, kn_ref, vn_ref, kt_ref, vt_ref, bias_ref, mult_ref, b0_ref = in_refs
    o_ref[...] = _samp_a_one(q_ref[...], kn_ref[...], vn_ref[...], kt_ref, vt_ref, bias_ref, mult_ref, b0_ref)


def _samp_a_kernel(*refs):
    _ride(refs[:N_RIDER_IN], refs[-1])


def _samp_a_call(arrays, first, count):
    r_in, r_out = _rider_specs(first, lambda i: i)
    return pl.pallas_call(
        _samp_a_kernel,
        grid=(count,),
        in_specs=r_in,
        out_specs=r_out,
        out_shape=jax.ShapeDtypeStruct((count, 1, 512), F32),
        compiler_params=_cparams(("arbitrary",)),
        name="sample_dilated_attn",
    )(*arrays)


def _samp_b_kernel(q_ref, kn_ref, vn_ref, kt_ref, vt_ref, bias_ref, b0_ref, sk_ref, o_ref):
    scale = HD_B ** -0.5
    low = lax.broadcasted_iota(jnp.int32, (1, 128), 1) < 64
    rid = lax.broadcasted_iota(jnp.int32, (8, 128), 0)
    kv_half = (lax.broadcasted_iota(jnp.int32, (8, 128), 1) >= 64) == (rid >= 4)

    bb = q_ref.shape[0]
    zero = jnp.zeros((8, 128), F32)
    qhs = []
    for b in range(bb):
        q = q_ref[b:b + 1, :]
        rows = []
        for h in range(H_B):
            chunk = q[:, 128 * (h // 2):128 * (h // 2) + 128]
            if (h % 2) != (h // 4):
                chunk = pltpu.roll(chunk, 64, 1)
            rows.append(chunk)
        qhs.append(jnp.where(kv_half, jnp.concatenate(rows, axis=0), 0.0))
    qblk = jnp.concatenate([jnp.concatenate([qhs[b] if c == b else zero for c in range(bb)], axis=1)
                            for b in range(bb)], axis=0)
    kt = kt_ref[...].reshape(bb * 128, W_B).astype(BF16)
    vt = vt_ref[...].reshape(bb * 128, W_B).astype(BF16)
    tile8 = lambda ref: jnp.concatenate([ref[...]] * bb, axis=0)
    s = _dot(qblk.astype(BF16), kt) * scale + tile8(bias_ref)
    knew = jnp.concatenate([jnp.broadcast_to(kn_ref[b:b + 1, :], (8, 128)) for b in range(bb)], axis=0)
    vnew = jnp.concatenate([jnp.broadcast_to(vn_ref[b:b + 1, :], (8, 128)) for b in range(bb)], axis=0)
    qall = jnp.concatenate(qhs, axis=0)
    ln = jnp.sum(qall * knew, axis=-1, keepdims=True) * scale + tile8(b0_ref)[:, 0:1]
    sk = tile8(sk_ref)[:, 0:1]
    m = jnp.maximum(jnp.maximum(jnp.max(s, axis=-1, keepdims=True), ln), sk)
    p = jnp.exp(s - m)
    pn = jnp.exp(ln - m)
    den = jnp.sum(p, axis=-1, keepdims=True) + pn + jnp.exp(sk - m)
    rall = _dot_nt(p.astype(BF16), vt)
    for b in range(bb):
        r = (rall[8 * b:8 * b + 8, 128 * b:128 * b + 128] + pn[8 * b:8 * b + 8] * vnew[8 * b:8 * b + 8])
        r = jnp.where(kv_half, r / den[8 * b:8 * b + 8], 0.0)
        chunks = []
        for c2 in range(4):
            pair = []
            for h in (2 * c2, 2 * c2 + 1):
                piece = r[h:h + 1, :]
                if (h % 2) != (h // 4):
                    piece = pltpu.roll(piece, 64, 1)
                pair.append(piece)
            chunks.append(jnp.where(low, pair[0], pair[1]))
        o_ref[b:b + 1, :] = jnp.concatenate(chunks, axis=1)


def _samp_b_call(qb, kn, vn, cache_k, cache_v, bias_sb, s0b, skb):
    nb = qb.shape[0]
    bb = 8
    kt = jnp.transpose(cache_k, (0, 2, 3, 1)).reshape(nb, 128, W_B)
    vt = jnp.transpose(cache_v, (0, 2, 3, 1)).reshape(nb, 128, W_B)
    row = lambda w: pl.BlockSpec((bb, w), lambda i: (i, 0))
    cblk = pl.BlockSpec((bb, 128, W_B), lambda i: (i, 0, 0))
    return pl.pallas_call(
        _samp_b_kernel,
        grid=(nb // bb,),
        in_specs=[row(512), row(128), row(128), cblk, cblk, _const_spec((8, W_B)),
                  _const_spec((8, 128)), _const_spec((8, 128))],
        out_specs=row(512),
        out_shape=jax.ShapeDtypeStruct((nb, 512), F32),
        compiler_params=_cparams(("arbitrary",)),
        name="sample_swa_attn",
    )(qb, kn, vn, kt, vt, bias_sb, s0b, skb)


def _samp_c_kernel(q_ref, mk_ref, mv_ref, mask_ref, o_ref):
    scale = HD_C ** -0.5
    zeros = jnp.zeros((4, HD_C), F32)

    def body(b, c):
        q8 = jnp.concatenate([q_ref[b], zeros], axis=0).astype(BF16)
        s = _dot_nt(q8, mk_ref[b].astype(BF16)) * scale + mask_ref[...]
        m = jnp.max(s, axis=-1, keepdims=True)
        p = jnp.exp(s - m)
        den = jnp.sum(p, axis=-1, keepdims=True)
        r = _dot(p.astype(BF16), mv_ref[b].astype(BF16)) / den
        o_ref[b] = r[0:4, :]
        return c

    lax.fori_loop(0, q_ref.shape[0], body, 0, unroll=True)


def _samp_c_call(qc, cache_k, cache_v):
    nb = qc.shape[0]
    bb = 8
    rows = N_MEM * H_C
    head_of_row = np.arange(rows)[None, :] % H_C
    mask = np.where(head_of_row == np.arange(8)[:, None], 0.0, NEG).astype(np.float32)
    qblk = pl.BlockSpec((bb, H_C, HD_C), lambda i: (i, 0, 0))
    cblk = pl.BlockSpec((bb, rows, HD_C), lambda i: (i, 0, 0))
    out = pl.pallas_call(
        _samp_c_kernel,
        grid=(nb // bb,),
        in_specs=[qblk, cblk, cblk, _const_spec((8, rows))],
        out_specs=qblk,
        out_shape=jax.ShapeDtypeStruct((nb, H_C, HD_C), F32),
        compiler_params=_cparams(("arbitrary",)),
        name="sample_mem_attn",
    )(qc.reshape(nb, H_C, HD_C), cache_k.reshape(nb, rows, HD_C), cache_v.reshape(nb, rows, HD_C),
      jnp.asarray(mask))
    return out.reshape(nb, 512)


def kernel(x_prompt, x_sample, mem_prompt, cache_a_k, cache_a_v, cache_b_k, cache_b_v, cache_mem_k,
           cache_mem_v, rel_bias, ln_g, w_in, gq_a, gk_a, gq_b, gk_b, gq_c, gk_c, sinks_b, mem_ln_g,
           w_mem_kv, w_br_a, w_br_b, w_br_c, w_out):
    batch, seq, _ = x_prompt.shape
    nsamp = x_sample.shape[0]
    assert ln_g.shape[0] == 1 and x_sample.shape[1] == 1
    assert (batch, seq, nsamp) == (8, 2048, 128) and w_in.shape == (1, D_MODEL, C_END)
    assert cache_a_k.shape == (1, nsamp, W_A, H_A, HD_A) and cache_b_k.shape == (1, nsamp, W_B, KV_B, HD_B)

    w_bf = w_in[0].astype(BF16)
    wmem_bf = w_mem_kv[0].astype(BF16)
    wbr_bf = jnp.stack([w_br_a[0], w_br_b[0], w_br_c[0]]).astype(BF16)
    wout_bf = w_out[0].astype(BF16)
    lng = ln_g.reshape(1, D_MODEL)
    gains = jnp.stack([jnp.tile(gq_a[0], 8), jnp.tile(gk_a[0], 8), jnp.tile(gq_b[0], 8),
                       jnp.tile(gk_b[0], 8), jnp.tile(gq_c[0], 4)])
    gk_c4 = jnp.tile(gk_c[0], 4).reshape(1, 512)
    blockdiag = np.kron(np.eye(4, dtype=np.float32), np.ones((64, 64), np.float32))
    pmat = jnp.asarray(blockdiag, BF16)

    *tables, mult_a = _bucket_tables()
    bias_a, bias_b, bias_sa, bias_sb, s0a, s0b, skb = _bias_call(rel_bias, sinks_b.reshape(H_B), tables)

    xs2 = x_sample.reshape(nsamp, D_MODEL)
    sqa, ska, sva, sqb, skb_new, svb, sqc = _qkv_call(xs2, lng, w_bf, gains, pmat)
    riders = _rider_arrays(sqa, ska, sva, cache_a_k[0], cache_a_v[0], bias_sa, jnp.asarray(mult_a), s0a)
    out_tm = 256
    n_qkv = batch * seq // 512
    n_dil = batch * (H_A // 2)
    n_out = batch * seq // out_tm
    n_alone = nsamp - n_qkv - n_dil - n_out
    assert n_alone >= 0

    xp2 = x_prompt.reshape(batch * seq, D_MODEL)
    qa, ka, va, qb, kb, vb, qc, ka_t, va_t, kb_t, vb_t, soa_1 = _qkv_call(
        xp2, lng, w_bf, gains, pmat, (batch, seq), rider=(riders, 0))
    mk, mv = _memkv_call(mem_prompt.reshape(batch * N_MEM, D_MODEL), mem_ln_g.reshape(1, D_MODEL),
                         wmem_bf, gk_c4)
    oa, soa_2 = _dil_call(qa, ka, va, bias_a, batch, seq, rider=(riders, n_qkv))
    ob = _swa_call(sinks_b.reshape(H_B), qb, kb, vb, bias_b, batch, seq)
    oc = _mem_attn_call(qc, mk, mv, batch, seq)
    yp, soa_3 = _out_call(xp2, oa.reshape(-1, 512), ob.reshape(-1, 512), oc.reshape(-1, 512),
                          lng, w_bf, wbr_bf, wout_bf, tm=out_tm, rider=(riders, n_qkv + n_dil))
    yp = yp.reshape(batch, seq, D_MODEL)

    soa_parts = [soa_1, soa_2, soa_3]
    if n_alone:
        soa_parts.append(_samp_a_call(riders, n_qkv + n_dil + n_out, n_alone))
    soa = jnp.concatenate(soa_parts, axis=0).reshape(nsamp, 512)
    sob = _samp_b_call(sqb, skb_new, svb, cache_b_k[0], cache_b_v[0], bias_sb, s0b, skb)
    soc = _samp_c_call(sqc, cache_mem_k[0], cache_mem_v[0])
    ys, = _out_call(xs2, soa, sob, soc, lng, w_bf, wbr_bf, wout_bf)

    return (yp, ys.reshape(nsamp, 1, D_MODEL),
            jnp.transpose(ka_t.reshape(1, batch, H_A, HD_A, seq), (0, 1, 4, 2, 3)),
            jnp.transpose(va_t.reshape(1, batch, H_A, HD_A, seq), (0, 1, 4, 2, 3)),
            jnp.transpose(kb_t.reshape(1, batch, KV_B, HD_B, W_B), (0, 1, 4, 2, 3)),
            jnp.transpose(vb_t.reshape(1, batch, KV_B, HD_B, W_B), (0, 1, 4, 2, 3)),
            mk.reshape(1, batch, N_MEM, H_C, HD_C), mv.reshape(1, batch, N_MEM, H_C, HD_C),
            ska.reshape(1, nsamp, 1, H_A, HD_A), sva.reshape(1, nsamp, 1, H_A, HD_A),
            skb_new.reshape(1, nsamp, 1, KV_B, HD_B), svb.reshape(1, nsamp, 1, KV_B, HD_B))
```

```python
import functools
import math

import numpy as np
import jax
import jax.numpy as jnp
from jax import lax
from jax.experimental import pallas as pl
from jax.experimental.pallas import tpu as pltpu

F32 = jnp.float32
BF16 = jnp.bfloat16

D_MODEL = 1024
H_A, HD_A = 8, 64
A_DILATIONS = (1, 4, 16)
A_STEPS = 128
W_A = 2048
H_B, KV_B, HD_B, W_B = 8, 2, 64, 128
H_C, HD_C, N_MEM = 4, 128, 256
NUM_BUCKETS, MAX_DISTANCE = 32, 2048
EPS = 1e-6
NEG = -1e30
BLK = 128
UNROLL = 8

C_QA, C_KA, C_VA, C_ZA = 0, 512, 1024, 1536
C_QB, C_KB, C_VB, C_ZB = 2048, 2560, 2688, 2816
C_QC, C_ZC = 3328, 3840
C_GA, C_GB, C_GC, C_END = 4352, 5376, 6400, 7424

VMEM_LIMIT = 56 * 1024 * 1024


def _cparams(sem):
    return pltpu.CompilerParams(dimension_semantics=sem, vmem_limit_bytes=VMEM_LIMIT)


def _const_spec(shape):
    nd = len(shape)
    return pl.BlockSpec(shape, lambda *_: (0,) * nd, pipeline_mode=pl.Buffered(1))


def _t5_bucket_np(dist):
    n = np.maximum(dist, 0)
    max_exact = NUM_BUCKETS // 2
    nf = np.maximum(n, 1).astype(np.float32)
    large = max_exact + (np.log(nf / np.float32(max_exact))
                         / np.float32(math.log(MAX_DISTANCE / max_exact))
                         * np.float32(NUM_BUCKETS - max_exact)).astype(np.int32)
    return np.where(n < max_exact, n, np.minimum(large, NUM_BUCKETS - 1)).astype(np.int32)


def _a_multiplicity(dist):
    return sum(((dist % d == 0) & (dist <= A_STEPS * d)).astype(np.int32) for d in A_DILATIONS)


def _bucket_tables():
    i = np.arange(BLK)[:, None]
    j = np.arange(2 * BLK)[None, :]
    rel = i + BLK - j
    bkt_a = np.stack([np.where((rel >= 0) & (rel <= A_STEPS), _t5_bucket_np(rel * d), -1)
                      for d in A_DILATIONS]).astype(np.int32)
    bkt_b = np.where((rel >= 0) & (rel < W_B), _t5_bucket_np(rel), -1).astype(np.int32)
    dist_a = W_A - np.arange(W_A)
    mult_a = _a_multiplicity(dist_a)
    sb_a = np.broadcast_to(np.where(mult_a > 0, _t5_bucket_np(dist_a), -1), (8, W_A)).astype(np.int32)
    dist_b = W_B - np.arange(W_B)
    sb_b = np.broadcast_to(np.where(dist_b < W_B, _t5_bucket_np(dist_b), -1), (8, W_B)).astype(np.int32)
    return bkt_a, bkt_b, sb_a, sb_b, mult_a.astype(np.float32).reshape(1, W_A)


def _rms_rows(x, g):
    return x * lax.rsqrt(jnp.mean(x * x, axis=-1, keepdims=True) + EPS) * g


def _dot(a, b):
    return jnp.dot(a, b, preferred_element_type=F32)


def _dot_nt(a, b):
    return lax.dot_general(a, b, (((1,), (1,)), ((), ())), preferred_element_type=F32)


def _headnorm64(t, g, pmat):
    sq = (t * t).astype(BF16)
    width = t.shape[1]
    step = min(width, 256)
    pp = pmat[:step, :step]
    parts = [_dot(sq[:, c:c + step], pp) for c in range(0, width, step)]
    ss = parts[0] if len(parts) == 1 else jnp.concatenate(parts, axis=1)
    return t * lax.rsqrt(ss * (1.0 / 64.0) + EPS) * g


def _headnorm128(t, g):
    parts = []
    for c in range(0, t.shape[1], 128):
        seg = t[:, c:c + 128]
        parts.append(seg * lax.rsqrt(jnp.mean(seg * seg, axis=-1, keepdims=True) + EPS))
    return jnp.concatenate(parts, axis=1) * g


def _bias_kernel(tbl_ref, sink_ref, bkta_ref, bktb_ref, sba_ref, sbb_ref,
                 ba_ref, bb_ref, oa_ref, ob_ref, s0a_ref, s0b_ref, sk_ref, *, present):
    pa, pb, psa, psb = present
    for g in range(3):
        bk = bkta_ref[g]
        for h in range(H_A):
            ba_ref[g, h] = jnp.full((BLK, 2 * BLK), NEG, F32)
        for b in pa[g]:
            hit = bk == b
            for h in range(H_A):
                ba_ref[g, h] = jnp.where(hit, tbl_ref[b, h], ba_ref[g, h])
    bk = bktb_ref[...]
    for h in range(H_B):
        bb_ref[h] = jnp.full((BLK, 2 * BLK), NEG, F32)
    for b in pb:
        hit = bk == b
        for h in range(H_B):
            bb_ref[h] = jnp.where(hit, tbl_ref[b, H_A + h], bb_ref[h])
    for (src, dst, buckets, col0) in ((sba_ref, oa_ref, psa, 0), (sbb_ref, ob_ref, psb, H_A)):
        bk = src[...]
        row = lax.broadcasted_iota(jnp.int32, bk.shape, 0)
        acc = jnp.full(bk.shape, NEG, F32)
        for b in buckets:
            hit = bk == b
            for h in range(8):
                acc = jnp.where(hit & (row == h), tbl_ref[b, col0 + h], acc)
        dst[...] = acc
    row8 = lax.broadcasted_iota(jnp.int32, (8, 128), 0)
    a0 = jnp.zeros((8, 128), F32)
    b0 = jnp.zeros((8, 128), F32)
    sk = jnp.zeros((8, 128), F32)
    for h in range(8):
        a0 = jnp.where(row8 == h, tbl_ref[0, h], a0)
        b0 = jnp.where(row8 == h, tbl_ref[0, H_A + h], b0)
        sk = jnp.where(row8 == h, sink_ref[h], sk)
    s0a_ref[...] = a0
    s0b_ref[...] = b0
    sk_ref[...] = sk


def _bias_call(rel_bias, sinks, tables):
    bkt_a, bkt_b, sb_a, sb_b = tables
    uniq = lambda a: tuple(int(b) for b in np.unique(a) if b >= 0)
    present = (tuple(uniq(bkt_a[g]) for g in range(3)), uniq(bkt_b), uniq(sb_a), uniq(sb_b))
    smem = pl.BlockSpec(memory_space=pltpu.SMEM)
    vmem = pl.BlockSpec(memory_space=pltpu.VMEM)
    return pl.pallas_call(
        functools.partial(_bias_kernel, present=present),
        in_specs=[smem, smem, vmem, vmem, vmem, vmem],
        out_specs=[vmem] * 7,
        out_shape=[jax.ShapeDtypeStruct((3, H_A, BLK, 2 * BLK), F32),
                   jax.ShapeDtypeStruct((H_B, BLK, 2 * BLK), F32),
                   jax.ShapeDtypeStruct((8, W_A), F32),
                   jax.ShapeDtypeStruct((8, W_B), F32),
                   jax.ShapeDtypeStruct((8, 128), F32),
                   jax.ShapeDtypeStruct((8, 128), F32),
                   jax.ShapeDtypeStruct((8, 128), F32)],
        compiler_params=pltpu.CompilerParams(vmem_limit_bytes=VMEM_LIMIT),
        name="bias_expand",
    )(rel_bias, sinks, jnp.asarray(bkt_a), jnp.asarray(bkt_b), jnp.asarray(sb_a), jnp.asarray(sb_b))


def _qkv_kernel(x_ref, lng_ref, w_ref, g_ref, p_ref, *refs, rider=False, tiles_per_seq=1):
    if rider:
        _ride(refs[:N_RIDER_IN], refs[-1])
        refs = refs[N_RIDER_IN:-1]
    qa_ref, ka_ref, va_ref, qb_ref, kb_ref, vb_ref, qc_ref, *cache_refs = refs
    h = _rms_rows(x_ref[...], lng_ref[...]).astype(BF16)
    pmat = p_ref[...]

    def proj(c0, c1):
        return _dot(h, w_ref[:, c0:c1])

    qa_ref[...] = _headnorm64(proj(C_QA, C_KA), g_ref[0:1, :], pmat)
    ka = _headnorm64(proj(C_KA, C_VA), g_ref[1:2, :], pmat)
    va = proj(C_VA, C_ZA)
    ka_ref[...] = ka
    va_ref[...] = va
    kb = _headnorm64(proj(C_KB, C_VB), g_ref[3:4, 0:128], pmat)
    vb = proj(C_VB, C_ZB)
    if cache_refs:
        kt_ref, vt_ref, kbt_ref, vbt_ref = cache_refs
        kt_ref[...] = ka.T
        vt_ref[...] = va.T

        @pl.when((pl.program_id(0) + 1) % tiles_per_seq == 0)
        def _():
            tm = kb.shape[0]
            kbt_ref[...] = kb[tm - W_B:, :].T
            vbt_ref[...] = vb[tm - W_B:, :].T
    qb_ref[...] = _headnorm64(proj(C_QB, C_KB), g_ref[2:3, :], pmat).astype(qb_ref.dtype)
    kb_ref[...] = kb
    vb_ref[...] = vb
    qc_ref[...] = _headnorm128(proj(C_QC, C_ZC), g_ref[4:5, :]).astype(qc_ref.dtype)


def _qkv_call(x2, lng, w_bf, gains, pmat, cache_layout_for=None, rider=None):
    n = x2.shape[0]
    tm = min(512, n)
    steps = n // tm
    row = lambda w: pl.BlockSpec((tm, w), lambda i: (i, 0))
    widths = (512, 512, 512, 512, 128, 128, 512)
    in_specs = [row(D_MODEL), _const_spec((1, D_MODEL)), _const_spec((D_MODEL, C_END)),
                _const_spec((5, 512)), _const_spec((256, 256))]
    args = [x2, lng, w_bf, gains, pmat]
    out_specs = [row(w) for w in widths]
    q_dtype = F32 if cache_layout_for is None else BF16
    out_shape = [jax.ShapeDtypeStruct((n, w), q_dtype if i in (3, 6) else F32) for i, w in enumerate(widths)]
    per = 1
    if cache_layout_for is not None:
        batch, seq = cache_layout_for
        per = seq // tm
        out_specs += [pl.BlockSpec((None, 512, tm), lambda i: (i // per, 0, i % per))] * 2
        out_shape += [jax.ShapeDtypeStruct((batch, 512, seq), F32)] * 2
        out_specs += [pl.BlockSpec((None, KV_B * HD_B, W_B), lambda i: (i // per, 0, 0))] * 2
        out_shape += [jax.ShapeDtypeStruct((batch, KV_B * HD_B, W_B), F32)] * 2
    if rider is not None:
        r_in, r_out = _rider_specs(rider[1], lambda i: i)
        in_specs += r_in
        args += list(rider[0])
        out_specs.append(r_out)
        out_shape.append(jax.ShapeDtypeStruct((steps, 1, 512), F32))
    return pl.pallas_call(
        functools.partial(_qkv_kernel, rider=rider is not None, tiles_per_seq=per),
        grid=(steps,),
        in_specs=in_specs,
        out_specs=out_specs,
        out_shape=out_shape,
        compiler_params=_cparams(("arbitrary",)),
        name="qkv_proj",
    )(*args)


def _memkv_kernel(m_ref, g_ref, w_ref, gk_ref, mk_ref, mv_ref):
    h = _rms_rows(m_ref[...], g_ref[...]).astype(BF16)
    mk_ref[...] = _headnorm128(_dot(h, w_ref[:, 0:512]), gk_ref[...])
    mv_ref[...] = _dot(h, w_ref[:, 512:1024])


def _memkv_call(mem2, g, w_bf, gk):
    n = mem2.shape[0]
    tm = 512
    row = lambda w: pl.BlockSpec((tm, w), lambda i: (i, 0))
    return pl.pallas_call(
        _memkv_kernel,
        grid=(n // tm,),
        in_specs=[row(D_MODEL), _const_spec((1, D_MODEL)), _const_spec((D_MODEL, 2 * 512)),
                  _const_spec((1, 512))],
        out_specs=[row(512), row(512)],
        out_shape=[jax.ShapeDtypeStruct((n, 512), F32)] * 2,
        compiler_params=_cparams(("arbitrary",)),
        name="mem_kv_proj",
    )(mem2, g, w_bf, gk)


def _out_kernel(x_ref, oa_ref, ob_ref, oc_ref, lng_ref, w_ref, wbr_ref, wout_ref, *refs):
    if len(refs) > 1:
        _ride(refs[:N_RIDER_IN], refs[-1])
    y_ref = refs[-2] if len(refs) > 1 else refs[0]
    x = x_ref[...]
    h = _rms_rows(x, lng_ref[...]).astype(BF16)
    acc = None
    for br, (o_ref, cz, cg) in enumerate(((oa_ref, C_ZA, C_GA), (ob_ref, C_ZB, C_GB), (oc_ref, C_ZC, C_GC))):
        z = _dot(h, w_ref[:, cz:cz + 512])
        y = (o_ref[...] * (z * jax.nn.sigmoid(z))).astype(BF16)
        yb = _dot(y, wbr_ref[br])
        gate = jax.nn.sigmoid(_dot(h, w_ref[:, cg:cg + D_MODEL]))
        acc = gate * yb if acc is None else acc + gate * yb
    y_ref[...] = x + _dot(acc.astype(BF16), wout_ref[...])


def _out_call(x2, oa, ob, oc, lng, w_bf, wbr_bf, wout_bf, tm=512, rider=None):
    n = x2.shape[0]
    tm = min(tm, n)
    steps = n // tm
    row = lambda w: pl.BlockSpec((tm, w), lambda i: (i, 0))
    in_specs = [row(D_MODEL), row(512), row(512), row(512), _const_spec((1, D_MODEL)),
                _const_spec((D_MODEL, C_END)), _const_spec((3, 512, D_MODEL)), _const_spec((D_MODEL, D_MODEL))]
    args = [x2, oa, ob, oc, lng, w_bf, wbr_bf, wout_bf]
    out_specs = [row(D_MODEL)]
    out_shape = [jax.ShapeDtypeStruct((n, D_MODEL), F32)]
    if rider is not None:
        r_in, r_out = _rider_specs(rider[1], lambda i: i)
        in_specs += r_in
        args += list(rider[0])
        out_specs.append(r_out)
        out_shape.append(jax.ShapeDtypeStruct((steps, 1, 512), F32))
    return pl.pallas_call(
        _out_kernel,
        grid=(steps,),
        in_specs=in_specs,
        out_specs=out_specs,
        out_shape=out_shape,
        compiler_params=_cparams(("arbitrary",)),
        name="gated_out",
    )(*args)


PAD = BLK
CLS_PITCH = BLK + 8


def _low_lanes(rows):
    return lax.broadcasted_iota(jnp.int32, (rows, 128), 1) < 64


def _aligned(start):
    return start if isinstance(start, int) else pl.multiple_of(start, BLK)


def _pair_tile(q, k0, k1, v0, v1, b0, b1, sinks=None):
    nk = k0.shape[0]
    low = _low_lanes(BLK)
    s = _dot_nt(q, jnp.concatenate([k0, k1], axis=0)) + jnp.concatenate([b0, b1], axis=1)
    ms = [jnp.max(s[:, hh * nk:(hh + 1) * nk], axis=-1, keepdims=True) for hh in range(2)]
    if sinks is not None:
        ms = [jnp.maximum(ms[hh], sinks[hh]) for hh in range(2)]
    p = jnp.concatenate([jnp.exp(s[:, hh * nk:(hh + 1) * nk] - ms[hh]) for hh in range(2)], axis=1).astype(BF16)
    own = _low_lanes(nk)
    ones = [jnp.where(own, 1.0, 0.0).astype(BF16), jnp.where(own, 0.0, 1.0).astype(BF16)]
    vblk = jnp.concatenate([jnp.concatenate([v0, ones[0]], axis=1),
                            jnp.concatenate([v1, ones[1]], axis=1)], axis=0)
    up = _dot(p, vblk)
    return up[:, :128], jnp.where(low, ms[0], ms[1]), up[:, 128:]


def _first_bias(bias, also=None):
    gone = lax.broadcasted_iota(jnp.int32, bias.shape, 1) < BLK
    if also is not None:
        gone = gone & also
    return jnp.where(gone, NEG, bias)


def _dil_kernel(q_ref, k_ref, v_ref, bias_ref, *refs, rider=False):
    if rider:
        _ride(refs[:N_RIDER_IN], refs[N_RIDER_IN + 1])
        refs = refs[N_RIDER_IN:N_RIDER_IN + 1] + refs[N_RIDER_IN + 2:]
    o_ref, qs_ref, kc_ref, vc_ref, c4_ref, ru_ref, rm_ref, rl_ref = refs
    scale = HD_A ** -0.5
    seq = q_ref.shape[0]
    ntile = seq // BLK
    low = _low_lanes(BLK)
    d1, d2 = A_DILATIONS[1], A_DILATIONS[2]

    def src_rows(g, t):
        d = A_DILATIONS[g]
        if d == 1:
            return pl.ds(_aligned(t * BLK), BLK)
        nblk = ntile // d
        r, n = t >> (nblk.bit_length() - 1), t & (nblk - 1)
        return pl.ds(r + n * (d * BLK), BLK, stride=d)

    zero = jnp.zeros((PAD, 128), BF16)
    for g in range(3):
        for hh in range(2):
            kc_ref[g, hh, 0:PAD, :] = zero
            vc_ref[g, hh, 0:PAD, :] = zero

        def stage(t, c, g=g):
            dst = pl.ds(pl.multiple_of(t * BLK, BLK), BLK)
            dstp = pl.ds(pl.multiple_of(PAD + t * BLK, BLK), BLK)
            if g < 2:
                src = src_rows(g, t)
                q, k, v = q_ref[src, :], k_ref[src, :], v_ref[src, :]
            else:
                src = pl.ds((t & (d1 - 1)) * (seq // d1) + (t >> (d1.bit_length() - 1)), BLK, stride=d2 // d1)
                q, k, v = c4_ref[0, src, :], c4_ref[1, src, :], c4_ref[2, src, :]
            if g == 1:
                c4_ref[0, dst, :], c4_ref[1, dst, :], c4_ref[2, dst, :] = q, k, v
            qs_ref[g, dst, :] = (q * scale).astype(BF16)
            kc_ref[g, 0, dstp, :] = jnp.where(low, k, 0.0).astype(BF16)
            kc_ref[g, 1, dstp, :] = jnp.where(low, 0.0, k).astype(BF16)
            vc_ref[g, 0, dstp, :] = jnp.where(low, v, 0.0).astype(BF16)
            vc_ref[g, 1, dstp, :] = jnp.where(low, 0.0, v).astype(BF16)
            return c

        lax.fori_loop(0, ntile, stage, 0, unroll=4)

    def tiles(g, i):
        nblk = ntile // A_DILATIONS[g]
        res = []
        for j in range(UNROLL):
            t = i * UNROLL + j
            qrows = pl.ds(_aligned(t * BLK), BLK)
            krows = pl.ds(_aligned(t * BLK), 2 * BLK)
            bias = [bias_ref[g, hh] for hh in range(2)]
            if nblk <= UNROLL and j % nblk == 0:
                bias = [_first_bias(b) for b in bias]
            elif nblk > UNROLL and j == 0:
                bias = [_first_bias(b, i == 0) for b in bias]
            res.append(_pair_tile(qs_ref[g, qrows, :], kc_ref[g, 0, krows, :], kc_ref[g, 1, krows, :],
                                  vc_ref[g, 0, krows, :], vc_ref[g, 1, krows, :], bias[0], bias[1]))
        return res

    for g in range(3):
        def body(i, c, g=g):
            res = tiles(g, i)
            for j in range(UNROLL):
                t = i * UNROLL + j
                rows = pl.ds(pl.multiple_of(t * CLS_PITCH, 8), BLK) if g == 2 else src_rows(g, t)
                ru_ref[g, rows, :], rm_ref[g, rows, :], rl_ref[g, rows, :] = res[j]
            return c

        lax.fori_loop(0, ntile // UNROLL, body, 0)

    def natural_rows(ref, i):
        parts = [ref[2, pl.ds((v % 2) * 8 * CLS_PITCH + i * (BLK // d2) + v // 2, 8, stride=CLS_PITCH), :]
                 for v in range(BLK // 8)]
        return jnp.concatenate(parts, axis=0)

    def combine(i, c):
        rows = pl.ds(pl.multiple_of(i * BLK, BLK), BLK)
        ms = [rm_ref[0, rows, :], rm_ref[1, rows, :], natural_rows(rm_ref, i)]
        mx = jnp.maximum(jnp.maximum(ms[0], ms[1]), ms[2])
        ws = [jnp.exp(m - mx) for m in ms]
        num = ws[0] * ru_ref[0, rows, :] + ws[1] * ru_ref[1, rows, :] + ws[2] * natural_rows(ru_ref, i)
        den = ws[0] * rl_ref[0, rows, :] + ws[1] * rl_ref[1, rows, :] + ws[2] * natural_rows(rl_ref, i)
        o_ref[rows, :] = num / den
        return c

    lax.fori_loop(0, ntile, combine, 0, unroll=2)


def _dil_call(qa, ka, va, bias_a, batch, seq, rider=None):
    assert seq == A_DILATIONS[2] * BLK and UNROLL % (seq // (A_DILATIONS[1] * BLK)) == 0
    blk = pl.BlockSpec((None, seq, 128), lambda b, p: (b, 0, p))
    pairs = H_A // 2
    in_specs = [blk, blk, blk, pl.BlockSpec((3, 2, BLK, 2 * BLK), lambda b, p: (0, p, 0, 0))]
    args = [qa.reshape(batch, seq, 512), ka.reshape(batch, seq, 512), va.reshape(batch, seq, 512), bias_a]
    out_specs = [blk]
    out_shape = [jax.ShapeDtypeStruct((batch, seq, 512), F32)]
    if rider is not None:
        r_in, r_out = _rider_specs(rider[1], lambda b, p: b * pairs + p)
        in_specs += r_in
        args += list(rider[0])
        out_specs.append(r_out)
        out_shape.append(jax.ShapeDtypeStruct((batch * pairs, 1, 512), F32))
    return pl.pallas_call(
        functools.partial(_dil_kernel, rider=rider is not None),
        grid=(batch, pairs),
        in_specs=in_specs,
        out_specs=out_specs,
        out_shape=out_shape,
        scratch_shapes=[pltpu.VMEM((3, seq, 128), BF16), pltpu.VMEM((3, 2, PAD + seq, 128), BF16),
                        pltpu.VMEM((3, 2, PAD + seq, 128), BF16), pltpu.VMEM((3, seq, 128), F32)]
                       + [pltpu.VMEM((3, A_DILATIONS[2] * CLS_PITCH, 128), F32)] * 3,
        compiler_params=_cparams(("arbitrary", "arbitrary")),
        name="dilated_attn",
    )(*args)


def _swa_kernel(sink_ref, q_ref, k_ref, v_ref, bias_ref, o_ref, qs_ref, kc_ref, vc_ref):
    scale = HD_B ** -0.5
    seq = q_ref.shape[0]
    ntile = seq // BLK
    p = pl.program_id(1)
    kv = p // 2
    sinks = (sink_ref[2 * p], sink_ref[2 * p + 1])
    low = _low_lanes(BLK)
    own_half = (lax.broadcasted_iota(jnp.int32, (BLK, 128), 1) >= 64).astype(jnp.int32) == kv

    def dup(xf):
        return jnp.where(own_half, xf, pltpu.roll(xf, 64, 1))

    zero = jnp.zeros((PAD, 128), BF16)
    for hh in range(2):
        kc_ref[hh, 0:PAD, :] = zero
        vc_ref[hh, 0:PAD, :] = zero

    def stage(t, c):
        src = pl.ds(pl.multiple_of(t * BLK, BLK), BLK)
        dstp = pl.ds(pl.multiple_of(PAD + t * BLK, BLK), BLK)
        qs_ref[src, :] = (q_ref[src, :] * scale).astype(BF16)
        k = dup(k_ref[src, :])
        v = dup(v_ref[src, :])
        kc_ref[0, dstp, :] = jnp.where(low, k, 0.0).astype(BF16)
        kc_ref[1, dstp, :] = jnp.where(low, 0.0, k).astype(BF16)
        vc_ref[0, dstp, :] = jnp.where(low, v, 0.0).astype(BF16)
        vc_ref[1, dstp, :] = jnp.where(low, 0.0, v).astype(BF16)
        return c

    lax.fori_loop(0, ntile, stage, 0, unroll=4)

    def body(i, c):
        res = []
        for j in range(UNROLL):
            t = i * UNROLL + j
            qrows = pl.ds(pl.multiple_of(t * BLK, BLK), BLK)
            krows = pl.ds(pl.multiple_of(t * BLK, BLK), 2 * BLK)
            bias = [bias_ref[hh] for hh in range(2)]
            if j == 0:
                bias = [_first_bias(b, i == 0) for b in bias]
            u, m, l = _pair_tile(qs_ref[qrows, :], kc_ref[0, krows, :], kc_ref[1, krows, :],
                                 vc_ref[0, krows, :], vc_ref[1, krows, :], bias[0], bias[1], sinks)
            sk = jnp.where(low, sinks[0], sinks[1])
            res.append(u / (l + jnp.exp(sk - m)))
        for j in range(UNROLL):
            o_ref[pl.ds(pl.multiple_of((i * UNROLL + j) * BLK, BLK), BLK), :] = res[j]
        return c

    lax.fori_loop(0, ntile // UNROLL, body, 0)


def _swa_call(sinks, qb, kb, vb, bias_b, batch, seq):
    qblk = pl.BlockSpec((None, seq, 128), lambda b, p: (b, 0, p))
    kblk = pl.BlockSpec((None, seq, 128), lambda b, p: (b, 0, 0))
    return pl.pallas_call(
        _swa_kernel,
        grid=(batch, H_B // 2),
        in_specs=[pl.BlockSpec(memory_space=pltpu.SMEM), qblk, kblk, kblk,
                  pl.BlockSpec((2, BLK, 2 * BLK), lambda b, p: (p, 0, 0))],
        out_specs=qblk,
        out_shape=jax.ShapeDtypeStruct((batch, seq, 512), F32),
        scratch_shapes=[pltpu.VMEM((seq, 128), BF16), pltpu.VMEM((2, PAD + seq, 128), BF16),
                        pltpu.VMEM((2, PAD + seq, 128), BF16)],
        compiler_params=_cparams(("arbitrary", "arbitrary")),
        name="swa_attn",
    )(sinks, qb.reshape(batch, seq, 512), kb.reshape(batch, seq, 128), vb.reshape(batch, seq, 128), bias_b)


def _mem_attn_kernel(q_ref, mk_ref, mv_ref, o_ref):
    scale = HD_C ** -0.5
    for h in range(H_C):
        cs = slice(h * HD_C, (h + 1) * HD_C)
        s = _dot_nt(q_ref[:, cs].astype(BF16), mk_ref[:, cs].astype(BF16)) * scale
        m = jnp.max(s, axis=-1, keepdims=True)
        p = jnp.exp(s - m)
        l = jnp.sum(p, axis=-1, keepdims=True)
        o_ref[:, cs] = _dot(p.astype(BF16), mv_ref[:, cs].astype(BF16)) / l


def _mem_attn_call(qc, mk, mv, batch, seq):
    tq = 512
    return pl.pallas_call(
        _mem_attn_kernel,
        grid=(batch, seq // tq),
        in_specs=[pl.BlockSpec((None, tq, 512), lambda b, i: (b, i, 0)),
                  pl.BlockSpec((None, N_MEM, 512), lambda b, i: (b, 0, 0)),
                  pl.BlockSpec((None, N_MEM, 512), lambda b, i: (b, 0, 0))],
        out_specs=pl.BlockSpec((None, tq, 512), lambda b, i: (b, i, 0)),
        out_shape=jax.ShapeDtypeStruct((batch, seq, 512), F32),
        compiler_params=_cparams(("arbitrary", "arbitrary")),
        name="mem_attn",
    )(qc.reshape(batch, seq, 512), mk.reshape(batch, N_MEM, 512), mv.reshape(batch, N_MEM, 512))


def _head_rows(row, width, head_dim):
    rid = lax.broadcasted_iota(jnp.int32, (8, width), 0)
    hid = lax.broadcasted_iota(jnp.int32, (8, width), 1) // head_dim
    return rid == hid, jnp.where(rid == hid, jnp.broadcast_to(row, (8, width)), 0.0)


def _samp_a_one(q, kn, vn, kt_ref, vt_ref, bias_ref, mult_ref, b0_ref):
    scale = HD_A ** -0.5
    nt = W_A // 128
    own, qh = _head_rows(q, 512, HD_A)
    qcols = jnp.transpose(jnp.broadcast_to(q, (128, 512)))
    s_heads = []
    for h in range(H_A):
        rows = slice(h * HD_A, (h + 1) * HD_A)
        qb = qcols[rows, :]
        s_heads.append(jnp.concatenate(
            [jnp.sum(kt_ref[rows, 128 * j:128 * (j + 1)] * qb, axis=0, keepdims=True) for j in range(nt)], axis=1))
    s = jnp.concatenate(s_heads, axis=0) * scale + bias_ref[...]
    ln = jnp.sum(qh * kn, axis=-1, keepdims=True) * scale + b0_ref[:, 0:1]
    m = jnp.maximum(jnp.max(s, axis=-1, keepdims=True), ln)
    p = jnp.exp(s - m) * mult_ref[...]
    pn = len(A_DILATIONS) * jnp.exp(ln - m)
    den = jnp.sum(p, axis=-1, keepdims=True) + pn
    accs = []
    for h in range(H_A):
        rows = slice(h * HD_A, (h + 1) * HD_A)
        acc = vt_ref[rows, 0:128] * p[h:h + 1, 0:128]
        for j in range(1, nt):
            acc = acc + vt_ref[rows, 128 * j:128 * (j + 1)] * p[h:h + 1, 128 * j:128 * (j + 1)]
        accs.append(acc)
    pv = jnp.sum(jnp.transpose(jnp.concatenate(accs, axis=0)), axis=0, keepdims=True)
    lanes = lambda col: jnp.sum(jnp.where(own, col, 0.0), axis=0, keepdims=True)
    return (pv + lanes(pn) * vn) / lanes(den)


N_RIDER_IN = 8


def _rider_arrays(qa, kn, vn, cache_k, cache_v, bias_sa, mult, s0a):
    nb = qa.shape[0]
    kt = jnp.transpose(cache_k, (0, 2, 3, 1)).reshape(nb, 512, W_A)
    vt = jnp.transpose(cache_v, (0, 2, 3, 1)).reshape(nb, 512, W_A)
    return (qa.reshape(nb, 1, 512), kn.reshape(nb, 1, 512), vn.reshape(nb, 1, 512), kt, vt, bias_sa, mult, s0a)


def _rider_specs(first, step_of):
    row = pl.BlockSpec((None, 1, 512), lambda *ids: (first + step_of(*ids), 0, 0))
    cblk = pl.BlockSpec((None, 512, W_A), lambda *ids: (first + step_of(*ids), 0, 0))
    ins = [row, row, row, cblk, cblk, _const_spec((8, W_A)), _const_spec((1, W_A)), _const_spec((8, 128))]
    return ins, pl.BlockSpec((None, 1, 512), lambda *ids: (step_of(*ids), 0, 0))


def _ride(in_refs, o_ref):
    q_ref, kn_ref, vn_ref, kt_ref, vt_ref, bias_ref, mult_ref, b0_ref = in_refs
    o_ref[...] = _samp_a_one(q_ref[...], kn_ref[...], vn_ref[...], kt_ref, vt_ref, bias_ref, mult_ref, b0_ref)


def _samp_a_kernel(*refs):
    _ride(refs[:N_RIDER_IN], refs[-1])


def _samp_a_call(arrays, first, count):
    r_in, r_out = _rider_specs(first, lambda i: i)
    return pl.pallas_call(
        _samp_a_kernel,
        grid=(count,),
        in_specs=r_in,
        out_specs=r_out,
        out_shape=jax.ShapeDtypeStruct((count, 1, 512), F32),
        compiler_params=_cparams(("arbitrary",)),
        name="sample_dilated_attn",
    )(*arrays)


def _samp_b_kernel(q_ref, kn_ref, vn_ref, kt_ref, vt_ref, bias_ref, b0_ref, sk_ref, o_ref):
    scale = HD_B ** -0.5
    low = lax.broadcasted_iota(jnp.int32, (1, 128), 1) < 64
    rid = lax.broadcasted_iota(jnp.int32, (8, 128), 0)
    kv_half = (lax.broadcasted_iota(jnp.int32, (8, 128), 1) >= 64) == (rid >= 4)

    bb = q_ref.shape[0]
    zero = jnp.zeros((8, 128), F32)
    qhs = []
    for b in range(bb):
        q = q_ref[b:b + 1, :]
        rows = []
        for h in range(H_B):
            chunk = q[:, 128 * (h // 2):128 * (h // 2) + 128]
            if (h % 2) != (h // 4):
                chunk = pltpu.roll(chunk, 64, 1)
            rows.append(chunk)
        qhs.append(jnp.where(kv_half, jnp.concatenate(rows, axis=0), 0.0))
    qblk = jnp.concatenate([jnp.concatenate([qhs[b] if c == b else zero for c in range(bb)], axis=1)
                            for b in range(bb)], axis=0)
    kt = kt_ref[...].reshape(bb * 128, W_B).astype(BF16)
    vt = vt_ref[...].reshape(bb * 128, W_B).astype(BF16)
    tile8 = lambda ref: jnp.concatenate([ref[...]] * bb, axis=0)
    s = _dot(qblk.astype(BF16), kt) * scale + tile8(bias_ref)
    knew = jnp.concatenate([jnp.broadcast_to(kn_ref[b:b + 1, :], (8, 128)) for b in range(bb)], axis=0)
    vnew = jnp.concatenate([jnp.broadcast_to(vn_ref[b:b + 1, :], (8, 128)) for b in range(bb)], axis=0)
    qall = jnp.concatenate(qhs, axis=0)
    ln = jnp.sum(qall * knew, axis=-1, keepdims=True) * scale + tile8(b0_ref)[:, 0:1]
    sk = tile8(sk_ref)[:, 0:1]
    m = jnp.maximum(jnp.maximum(jnp.max(s, axis=-1, keepdims=True), ln), sk)
    p = jnp.exp(s - m)
    pn = jnp.exp(ln - m)
    den = jnp.sum(p, axis=-1, keepdims=True) + pn + jnp.exp(sk - m)
    rall = _dot_nt(p.astype(BF16), vt)
    for b in range(bb):
        r = (rall[8 * b:8 * b + 8, 128 * b:128 * b + 128] + pn[8 * b:8 * b + 8] * vnew[8 * b:8 * b + 8])
        r = jnp.where(kv_half, r / den[8 * b:8 * b + 8], 0.0)
        chunks = []
        for c2 in range(4):
            pair = []
            for h in (2 * c2, 2 * c2 + 1):
                piece = r[h:h + 1, :]
                if (h % 2) != (h // 4):
                    piece = pltpu.roll(piece, 64, 1)
                pair.append(piece)
            chunks.append(jnp.where(low, pair[0], pair[1]))
        o_ref[b:b + 1, :] = jnp.concatenate(chunks, axis=1)


def _samp_b_call(qb, kn, vn, cache_k, cache_v, bias_sb, s0b, skb):
    nb = qb.shape[0]
    bb = 8
    kt = jnp.transpose(cache_k, (0, 2, 3, 1)).reshape(nb, 128, W_B)
    vt = jnp.transpose(cache_v, (0, 2, 3, 1)).reshape(nb, 128, W_B)
    row = lambda w: pl.BlockSpec((bb, w), lambda i: (i, 0))
    cblk = pl.BlockSpec((bb, 128, W_B), lambda i: (i, 0, 0))
    return pl.pallas_call(
        _samp_b_kernel,
        grid=(nb // bb,),
        in_specs=[row(512), row(128), row(128), cblk, cblk, _const_spec((8, W_B)),
                  _const_spec((8, 128)), _const_spec((8, 128))],
        out_specs=row(512),
        out_shape=jax.ShapeDtypeStruct((nb, 512), F32),
        compiler_params=_cparams(("arbitrary",)),
        name="sample_swa_attn",
    )(qb, kn, vn, kt, vt, bias_sb, s0b, skb)


def _samp_c_kernel(q_ref, mk_ref, mv_ref, mask_ref, o_ref):
    scale = HD_C ** -0.5
    zeros = jnp.zeros((4, HD_C), F32)

    def body(b, c):
        q8 = jnp.concatenate([q_ref[b], zeros], axis=0).astype(BF16)
        s = _dot_nt(q8, mk_ref[b].astype(BF16)) * scale + mask_ref[...]
        m = jnp.max(s, axis=-1, keepdims=True)
        p = jnp.exp(s - m)
        den = jnp.sum(p, axis=-1, keepdims=True)
        r = _dot(p.astype(BF16), mv_ref[b].astype(BF16)) / den
        o_ref[b] = r[0:4, :]
        return c

    lax.fori_loop(0, q_ref.shape[0], body, 0, unroll=True)


def _samp_c_call(qc, cache_k, cache_v):
    nb = qc.shape[0]
    bb = 8
    rows = N_MEM * H_C
    head_of_row = np.arange(rows)[None, :] % H_C
    mask = np.where(head_of_row == np.arange(8)[:, None], 0.0, NEG).astype(np.float32)
    qblk = pl.BlockSpec((bb, H_C, HD_C), lambda i: (i, 0, 0))
    cblk = pl.BlockSpec((bb, rows, HD_C), lambda i: (i, 0, 0))
    out = pl.pallas_call(
        _samp_c_kernel,
        grid=(nb // bb,),
        in_specs=[qblk, cblk, cblk, _const_spec((8, rows))],
        out_specs=qblk,
        out_shape=jax.ShapeDtypeStruct((nb, H_C, HD_C), F32),
        compiler_params=_cparams(("arbitrary",)),
        name="sample_mem_attn",
    )(qc.reshape(nb, H_C, HD_C), cache_k.reshape(nb, rows, HD_C), cache_v.reshape(nb, rows, HD_C),
      jnp.asarray(mask))
    return out.reshape(nb, 512)


def kernel(x_prompt, x_sample, mem_prompt, cache_a_k, cache_a_v, cache_b_k, cache_b_v, cache_mem_k,
           cache_mem_v, rel_bias, ln_g, w_in, gq_a, gk_a, gq_b, gk_b, gq_c, gk_c, sinks_b, mem_ln_g,
           w_mem_kv, w_br_a, w_br_b, w_br_c, w_out):
    batch, seq, _ = x_prompt.shape
    nsamp = x_sample.shape[0]
    assert ln_g.shape[0] == 1 and x_sample.shape[1] == 1
    assert (batch, seq, nsamp) == (8, 2048, 128) and w_in.shape == (1, D_MODEL, C_END)
    assert cache_a_k.shape == (1, nsamp, W_A, H_A, HD_A) and cache_b_k.shape == (1, nsamp, W_B, KV_B, HD_B)

    w_bf = w_in[0].astype(BF16)
    wmem_bf = w_mem_kv[0].astype(BF16)
    wbr_bf = jnp.stack([w_br_a[0], w_br_b[0], w_br_c[0]]).astype(BF16)
    wout_bf = w_out[0].astype(BF16)
    lng = ln_g.reshape(1, D_MODEL)
    gains = jnp.stack([jnp.tile(gq_a[0], 8), jnp.tile(gk_a[0], 8), jnp.tile(gq_b[0], 8),
                       jnp.tile(gk_b[0], 8), jnp.tile(gq_c[0], 4)])
    gk_c4 = jnp.tile(gk_c[0], 4).reshape(1, 512)
    blockdiag = np.kron(np.eye(4, dtype=np.float32), np.ones((64, 64), np.float32))
    pmat = jnp.asarray(blockdiag, BF16)

    *tables, mult_a = _bucket_tables()
    bias_a, bias_b, bias_sa, bias_sb, s0a, s0b, skb = _bias_call(rel_bias, sinks_b.reshape(H_B), tables)

    xs2 = x_sample.reshape(nsamp, D_MODEL)
    sqa, ska, sva, sqb, skb_new, svb, sqc = _qkv_call(xs2, lng, w_bf, gains, pmat)
    riders = _rider_arrays(sqa, ska, sva, cache_a_k[0], cache_a_v[0], bias_sa, jnp.asarray(mult_a), s0a)
    out_tm = 256
    n_qkv = batch * seq // 512
    n_dil = batch * (H_A // 2)
    n_out = batch * seq // out_tm
    n_alone = nsamp - n_qkv - n_dil - n_out
    assert n_alone >= 0

    xp2 = x_prompt.reshape(batch * seq, D_MODEL)
    qa, ka, va, qb, kb, vb, qc, ka_t, va_t, kb_t, vb_t, soa_1 = _qkv_call(
        xp2, lng, w_bf, gains, pmat, (batch, seq), rider=(riders, 0))
    mk, mv = _memkv_call(mem_prompt.reshape(batch * N_MEM, D_MODEL), mem_ln_g.reshape(1, D_MODEL),
                         wmem_bf, gk_c4)
    oa, soa_2 = _dil_call(qa, ka, va, bias_a, batch, seq, rider=(riders, n_qkv))
    ob = _swa_call(sinks_b.reshape(H_B), qb, kb, vb, bias_b, batch, seq)
    oc = _mem_attn_call(qc, mk, mv, batch, seq)
    yp, soa_3 = _out_call(xp2, oa.reshape(-1, 512), ob.reshape(-1, 512), oc.reshape(-1, 512),
                          lng, w_bf, wbr_bf, wout_bf, tm=out_tm, rider=(riders, n_qkv + n_dil))
    yp = yp.reshape(batch, seq, D_MODEL)

    soa_parts = [soa_1, soa_2, soa_3]
    if n_alone:
        soa_parts.append(_samp_a_call(riders, n_qkv + n_dil + n_out, n_alone))
    soa = jnp.concatenate(soa_parts, axis=0).reshape(nsamp, 512)
    sob = _samp_b_call(sqb, skb_new, svb, cache_b_k[0], cache_b_v[0], bias_sb, s0b, skb)
    soc = _samp_c_call(sqc, cache_mem_k[0], cache_mem_v[0])
    ys, = _out_call(xs2, soa, sob, soc, lng, w_bf, wbr_bf, wout_bf)

    return (yp, ys.reshape(nsamp, 1, D_MODEL),
            jnp.transpose(ka_t.reshape(1, batch, H_A, HD_A, seq), (0, 1, 4, 2, 3)),
            jnp.transpose(va_t.reshape(1, batch, H_A, HD_A, seq), (0, 1, 4, 2, 3)),
            jnp.transpose(kb_t.reshape(1, batch, KV_B, HD_B, W_B), (0, 1, 4, 2, 3)),
            jnp.transpose(vb_t.reshape(1, batch, KV_B, HD_B, W_B), (0, 1, 4, 2, 3)),
            mk.reshape(1, batch, N_MEM, H_C, HD_C), mv.reshape(1, batch, N_MEM, H_C, HD_C),
            ska.reshape(1, nsamp, 1, H_A, HD_A), sva.reshape(1, nsamp, 1, H_A, HD_A),
            skb_new.reshape(1, nsamp, 1, KV_B, HD_B), svb.reshape(1, nsamp, 1, KV_B, HD_B))
```

```python
import functools
import math

import numpy as np
import jax
import jax.numpy as jnp
from jax import lax
from jax.experimental import pallas as pl
from jax.experimental.pallas import tpu as pltpu

F32 = jnp.float32
BF16 = jnp.bfloat16

D_MODEL = 1024
H_A, HD_A = 8, 64
A_DILATIONS = (1, 4, 16)
A_STEPS = 128
W_A = 2048
H_B, KV_B, HD_B, W_B = 8, 2, 64, 128
H_C, HD_C, N_MEM = 4, 128, 256
NUM_BUCKETS, MAX_DISTANCE = 32, 2048
EPS = 1e-6
NEG = -1e30
BLK = 128
UNROLL = 16

C_QA, C_KA, C_VA, C_ZA = 0, 512, 1024, 1536
C_QB, C_KB, C_VB, C_ZB = 2048, 2560, 2688, 2816
C_QC, C_ZC = 3328, 3840
C_GA, C_GB, C_GC, C_END = 4352, 5376, 6400, 7424

VMEM_LIMIT = 56 * 1024 * 1024


def _cparams(sem):
    return pltpu.CompilerParams(dimension_semantics=sem, vmem_limit_bytes=VMEM_LIMIT)


def _const_spec(shape):
    nd = len(shape)
    return pl.BlockSpec(shape, lambda *_: (0,) * nd, pipeline_mode=pl.Buffered(1))


def _t5_bucket_np(dist):
    n = np.maximum(dist, 0)
    max_exact = NUM_BUCKETS // 2
    nf = np.maximum(n, 1).astype(np.float32)
    large = max_exact + (np.log(nf / np.float32(max_exact))
                         / np.float32(math.log(MAX_DISTANCE / max_exact))
                         * np.float32(NUM_BUCKETS - max_exact)).astype(np.int32)
    return np.where(n < max_exact, n, np.minimum(large, NUM_BUCKETS - 1)).astype(np.int32)


def _a_multiplicity(dist):
    return sum(((dist % d == 0) & (dist <= A_STEPS * d)).astype(np.int32) for d in A_DILATIONS)


def _bucket_tables():
    i = np.arange(BLK)[:, None]
    j = np.arange(2 * BLK)[None, :]
    rel = i + BLK - j
    bkt_a = np.stack([np.where((rel >= 0) & (rel <= A_STEPS), _t5_bucket_np(rel * d), -1)
                      for d in A_DILATIONS]).astype(np.int32)
    bkt_b = np.where((rel >= 0) & (rel < W_B), _t5_bucket_np(rel), -1).astype(np.int32)
    dist_a = W_A - np.arange(W_A)
    mult_a = _a_multiplicity(dist_a)
    sb_a = np.broadcast_to(np.where(mult_a > 0, _t5_bucket_np(dist_a), -1), (8, W_A)).astype(np.int32)
    dist_b = W_B - np.arange(W_B)
    sb_b = np.broadcast_to(np.where(dist_b < W_B, _t5_bucket_np(dist_b), -1), (8, W_B)).astype(np.int32)
    return bkt_a, bkt_b, sb_a, sb_b, mult_a.astype(np.float32).reshape(1, W_A)


def _rms_rows(x, g):
    return x * lax.rsqrt(jnp.mean(x * x, axis=-1, keepdims=True) + EPS) * g


def _dot(a, b):
    return jnp.dot(a, b, preferred_element_type=F32)


def _dot_nt(a, b):
    return lax.dot_general(a, b, (((1,), (1,)), ((), ())), preferred_element_type=F32)


def _headnorm64(t, g, pmat):
    sq = (t * t).astype(BF16)
    width = t.shape[1]
    step = min(width, 256)
    pp = pmat[:step, :step]
    parts = [_dot(sq[:, c:c + step], pp) for c in range(0, width, step)]
    ss = parts[0] if len(parts) == 1 else jnp.concatenate(parts, axis=1)
    return t * lax.rsqrt(ss * (1.0 / 64.0) + EPS) * g


def _headnorm128(t, g):
    parts = []
    for c in range(0, t.shape[1], 128):
        seg = t[:, c:c + 128]
        parts.append(seg * lax.rsqrt(jnp.mean(seg * seg, axis=-1, keepdims=True) + EPS))
    return jnp.concatenate(parts, axis=1) * g


def _bias_kernel(tbl_ref, sink_ref, bkta_ref, bktb_ref, sba_ref, sbb_ref,
                 ba_ref, bb_ref, oa_ref, ob_ref, s0a_ref, s0b_ref, sk_ref, *, present):
    pa, pb, psa, psb = present
    for g in range(3):
        bk = bkta_ref[g]
        for h in range(H_A):
            ba_ref[g, h] = jnp.full((BLK, 2 * BLK), NEG, F32)
        for b in pa[g]:
            hit = bk == b
            for h in range(H_A):
                ba_ref[g, h] = jnp.where(hit, tbl_ref[b, h], ba_ref[g, h])
    bk = bktb_ref[...]
    for h in range(H_B):
        bb_ref[h] = jnp.full((BLK, 2 * BLK), NEG, F32)
    for b in pb:
        hit = bk == b
        for h in range(H_B):
            bb_ref[h] = jnp.where(hit, tbl_ref[b, H_A + h], bb_ref[h])
    for (src, dst, buckets, col0) in ((sba_ref, oa_ref, psa, 0), (sbb_ref, ob_ref, psb, H_A)):
        bk = src[...]
        row = lax.broadcasted_iota(jnp.int32, bk.shape, 0)
        acc = jnp.full(bk.shape, NEG, F32)
        for b in buckets:
            hit = bk == b
            for h in range(8):
                acc = jnp.where(hit & (row == h), tbl_ref[b, col0 + h], acc)
        dst[...] = acc
    row8 = lax.broadcasted_iota(jnp.int32, (8, 128), 0)
    a0 = jnp.zeros((8, 128), F32)
    b0 = jnp.zeros((8, 128), F32)
    sk = jnp.zeros((8, 128), F32)
    for h in range(8):
        a0 = jnp.where(row8 == h, tbl_ref[0, h], a0)
        b0 = jnp.where(row8 == h, tbl_ref[0, H_A + h], b0)
        sk = jnp.where(row8 == h, sink_ref[h], sk)
    s0a_ref[...] = a0
    s0b_ref[...] = b0
    sk_ref[...] = sk


def _bias_call(rel_bias, sinks, tables):
    bkt_a, bkt_b, sb_a, sb_b = tables
    uniq = lambda a: tuple(int(b) for b in np.unique(a) if b >= 0)
    present = (tuple(uniq(bkt_a[g]) for g in range(3)), uniq(bkt_b), uniq(sb_a), uniq(sb_b))
    smem = pl.BlockSpec(memory_space=pltpu.SMEM)
    vmem = pl.BlockSpec(memory_space=pltpu.VMEM)
    return pl.pallas_call(
        functools.partial(_bias_kernel, present=present),
        in_specs=[smem, smem, vmem, vmem, vmem, vmem],
        out_specs=[vmem] * 7,
        out_shape=[jax.ShapeDtypeStruct((3, H_A, BLK, 2 * BLK), F32),
                   jax.ShapeDtypeStruct((H_B, BLK, 2 * BLK), F32),
                   jax.ShapeDtypeStruct((8, W_A), F32),
                   jax.ShapeDtypeStruct((8, W_B), F32),
                   jax.ShapeDtypeStruct((8, 128), F32),
                   jax.ShapeDtypeStruct((8, 128), F32),
                   jax.ShapeDtypeStruct((8, 128), F32)],
        compiler_params=pltpu.CompilerParams(vmem_limit_bytes=VMEM_LIMIT),
        name="bias_expand",
    )(rel_bias, sinks, jnp.asarray(bkt_a), jnp.asarray(bkt_b), jnp.asarray(sb_a), jnp.asarray(sb_b))


def _qkv_kernel(x_ref, lng_ref, w_ref, g_ref, p_ref, *refs, rider=False, tiles_per_seq=1):
    if rider:
        _ride(refs[:N_RIDER_IN], refs[-1])
        refs = refs[N_RIDER_IN:-1]
    qa_ref, ka_ref, va_ref, qb_ref, kb_ref, vb_ref, qc_ref, *cache_refs = refs
    h = _rms_rows(x_ref[...], lng_ref[...]).astype(BF16)
    pmat = p_ref[...]

    def proj(c0, c1):
        return _dot(h, w_ref[:, c0:c1])

    qa_ref[...] = _headnorm64(proj(C_QA, C_KA), g_ref[0:1, :], pmat)
    ka = _headnorm64(proj(C_KA, C_VA), g_ref[1:2, :], pmat)
    va = proj(C_VA, C_ZA)
    ka_ref[...] = ka
    va_ref[...] = va
    kb = _headnorm64(proj(C_KB, C_VB), g_ref[3:4, 0:128], pmat)
    vb = proj(C_VB, C_ZB)
    if cache_refs:
        kt_ref, vt_ref, kbt_ref, vbt_ref = cache_refs
        kt_ref[...] = ka.T
        vt_ref[...] = va.T

        @pl.when((pl.program_id(0) + 1) % tiles_per_seq == 0)
        def _():
            tm = kb.shape[0]
            kbt_ref[...] = kb[tm - W_B:, :].T
            vbt_ref[...] = vb[tm - W_B:, :].T
    qb_ref[...] = _headnorm64(proj(C_QB, C_KB), g_ref[2:3, :], pmat).astype(qb_ref.dtype)
    kb_ref[...] = kb
    vb_ref[...] = vb
    qc_ref[...] = _headnorm128(proj(C_QC, C_ZC), g_ref[4:5, :]).astype(qc_ref.dtype)


def _qkv_call(x2, lng, w_bf, gains, pmat, cache_layout_for=None, rider=None):
    n = x2.shape[0]
    tm = min(512, n)
    steps = n // tm
    row = lambda w: pl.BlockSpec((tm, w), lambda i: (i, 0))
    widths = (512, 512, 512, 512, 128, 128, 512)
    in_specs = [row(D_MODEL), _const_spec((1, D_MODEL)), _const_spec((D_MODEL, C_END)),
                _const_spec((5, 512)), _const_spec((256, 256))]
    args = [x2, lng, w_bf, gains, pmat]
    out_specs = [row(w) for w in widths]
    q_dtype = F32 if cache_layout_for is None else BF16
    out_shape = [jax.ShapeDtypeStruct((n, w), q_dtype if i in (3, 6) else F32) for i, w in enumerate(widths)]
    per = 1
    if cache_layout_for is not None:
        batch, seq = cache_layout_for
        per = seq // tm
        out_specs += [pl.BlockSpec((None, 512, tm), lambda i: (i // per, 0, i % per))] * 2
        out_shape += [jax.ShapeDtypeStruct((batch, 512, seq), F32)] * 2
        out_specs += [pl.BlockSpec((None, KV_B * HD_B, W_B), lambda i: (i // per, 0, 0))] * 2
        out_shape += [jax.ShapeDtypeStruct((batch, KV_B * HD_B, W_B), F32)] * 2
    if rider is not None:
        r_in, r_out = _rider_specs(rider[1], lambda i: i)
        in_specs += r_in
        args += list(rider[0])
        out_specs.append(r_out)
        out_shape.append(jax.ShapeDtypeStruct((steps, 1, 512), F32))
    return pl.pallas_call(
        functools.partial(_qkv_kernel, rider=rider is not None, tiles_per_seq=per),
        grid=(steps,),
        in_specs=in_specs,
        out_specs=out_specs,
        out_shape=out_shape,
        compiler_params=_cparams(("arbitrary",)),
        name="qkv_proj",
    )(*args)


def _memkv_kernel(m_ref, g_ref, w_ref, gk_ref, mk_ref, mv_ref):
    h = _rms_rows(m_ref[...], g_ref[...]).astype(BF16)
    mk_ref[...] = _headnorm128(_dot(h, w_ref[:, 0:512]), gk_ref[...])
    mv_ref[...] = _dot(h, w_ref[:, 512:1024])


def _memkv_call(mem2, g, w_bf, gk):
    n = mem2.shape[0]
    tm = 512
    row = lambda w: pl.BlockSpec((tm, w), lambda i: (i, 0))
    return pl.pallas_call(
        _memkv_kernel,
        grid=(n // tm,),
        in_specs=[row(D_MODEL), _const_spec((1, D_MODEL)), _const_spec((D_MODEL, 2 * 512)),
                  _const_spec((1, 512))],
        out_specs=[row(512), row(512)],
        out_shape=[jax.ShapeDtypeStruct((n, 512), F32)] * 2,
        compiler_params=_cparams(("arbitrary",)),
        name="mem_kv_proj",
    )(mem2, g, w_bf, gk)


def _out_kernel(x_ref, oa_ref, ob_ref, oc_ref, lng_ref, w_ref, wbr_ref, wout_ref, *refs):
    if len(refs) > 1:
        _ride(refs[:N_RIDER_IN], refs[-1])
    y_ref = refs[-2] if len(refs) > 1 else refs[0]
    x = x_ref[...]
    h = _rms_rows(x, lng_ref[...]).astype(BF16)
    acc = None
    for br, (o_ref, cz, cg) in enumerate(((oa_ref, C_ZA, C_GA), (ob_ref, C_ZB, C_GB), (oc_ref, C_ZC, C_GC))):
        z = _dot(h, w_ref[:, cz:cz + 512])
        y = (o_ref[...] * (z * jax.nn.sigmoid(z))).astype(BF16)
        yb = _dot(y, wbr_ref[br])
        gate = jax.nn.sigmoid(_dot(h, w_ref[:, cg:cg + D_MODEL]))
        acc = gate * yb if acc is None else acc + gate * yb
    y_ref[...] = x + _dot(acc.astype(BF16), wout_ref[...])


def _out_call(x2, oa, ob, oc, lng, w_bf, wbr_bf, wout_bf, tm=512, rider=None):
    n = x2.shape[0]
    tm = min(tm, n)
    steps = n // tm
    row = lambda w: pl.BlockSpec((tm, w), lambda i: (i, 0))
    in_specs = [row(D_MODEL), row(512), row(512), row(512), _const_spec((1, D_MODEL)),
                _const_spec((D_MODEL, C_END)), _const_spec((3, 512, D_MODEL)), _const_spec((D_MODEL, D_MODEL))]
    args = [x2, oa, ob, oc, lng, w_bf, wbr_bf, wout_bf]
    out_specs = [row(D_MODEL)]
    out_shape = [jax.ShapeDtypeStruct((n, D_MODEL), F32)]
    if rider is not None:
        r_in, r_out = _rider_specs(rider[1], lambda i: i)
        in_specs += r_in
        args += list(rider[0])
        out_specs.append(r_out)
        out_shape.append(jax.ShapeDtypeStruct((steps, 1, 512), F32))
    return pl.pallas_call(
        _out_kernel,
        grid=(steps,),
        in_specs=in_specs,
        out_specs=out_specs,
        out_shape=out_shape,
        compiler_params=_cparams(("arbitrary",)),
        name="gated_out",
    )(*args)


PAD = BLK
CLS_PITCH = BLK + 8


def _low_lanes(rows):
    return lax.broadcasted_iota(jnp.int32, (rows, 128), 1) < 64


def _aligned(start):
    return start if isinstance(start, int) else pl.multiple_of(start, BLK)


def _pair_tile(q, k0, k1, v0, v1, b0, b1, sinks=None):
    nk = k0.shape[0]
    low = _low_lanes(BLK)
    s = _dot_nt(q, jnp.concatenate([k0, k1], axis=0)) + jnp.concatenate([b0, b1], axis=1)
    ms = [jnp.max(s[:, hh * nk:(hh + 1) * nk], axis=-1, keepdims=True) for hh in range(2)]
    if sinks is not None:
        ms = [jnp.maximum(ms[hh], sinks[hh]) for hh in range(2)]
    p = jnp.concatenate([jnp.exp(s[:, hh * nk:(hh + 1) * nk] - ms[hh]) for hh in range(2)], axis=1).astype(BF16)
    own = _low_lanes(nk)
    ones = [jnp.where(own, 1.0, 0.0).astype(BF16), jnp.where(own, 0.0, 1.0).astype(BF16)]
    vblk = jnp.concatenate([jnp.concatenate([v0, ones[0]], axis=1),
                            jnp.concatenate([v1, ones[1]], axis=1)], axis=0)
    up = _dot(p, vblk)
    return up[:, :128], jnp.where(low, ms[0], ms[1]), up[:, 128:]


def _first_bias(bias, also=None):
    gone = lax.broadcasted_iota(jnp.int32, bias.shape, 1) < BLK
    if also is not None:
        gone = gone & also
    return jnp.where(gone, NEG, bias)


def _dil_kernel(q_ref, k_ref, v_ref, bias_ref, *refs, rider=False):
    if rider:
        _ride(refs[:N_RIDER_IN], refs[N_RIDER_IN + 1])
        refs = refs[N_RIDER_IN:N_RIDER_IN + 1] + refs[N_RIDER_IN + 2:]
    o_ref, qs_ref, kc_ref, vc_ref, c4_ref, ru_ref, rm_ref, rl_ref = refs
    scale = HD_A ** -0.5
    seq = q_ref.shape[0]
    ntile = seq // BLK
    low = _low_lanes(BLK)
    d1, d2 = A_DILATIONS[1], A_DILATIONS[2]

    def src_rows(g, t):
        d = A_DILATIONS[g]
        if d == 1:
            return pl.ds(_aligned(t * BLK), BLK)
        nblk = ntile // d
        r, n = t >> (nblk.bit_length() - 1), t & (nblk - 1)
        return pl.ds(r + n * (d * BLK), BLK, stride=d)

    zero = jnp.zeros((PAD, 128), BF16)
    for g in range(3):
        for hh in range(2):
            kc_ref[g, hh, 0:PAD, :] = zero
            vc_ref[g, hh, 0:PAD, :] = zero

        def stage(t, c, g=g):
            dst = pl.ds(pl.multiple_of(t * BLK, BLK), BLK)
            dstp = pl.ds(pl.multiple_of(PAD + t * BLK, BLK), BLK)
            if g < 2:
                src = src_rows(g, t)
                q, k, v = q_ref[src, :], k_ref[src, :], v_ref[src, :]
            else:
                src = pl.ds((t & (d1 - 1)) * (seq // d1) + (t >> (d1.bit_length() - 1)), BLK, stride=d2 // d1)
                q, k, v = c4_ref[0, src, :], c4_ref[1, src, :], c4_ref[2, src, :]
            if g == 1:
                c4_ref[0, dst, :], c4_ref[1, dst, :], c4_ref[2, dst, :] = q, k, v
            qs_ref[g, dst, :] = (q * scale).astype(BF16)
            kc_ref[g, 0, dstp, :] = jnp.where(low, k, 0.0).astype(BF16)
            kc_ref[g, 1, dstp, :] = jnp.where(low, 0.0, k).astype(BF16)
            vc_ref[g, 0, dstp, :] = jnp.where(low, v, 0.0).astype(BF16)
            vc_ref[g, 1, dstp, :] = jnp.where(low, 0.0, v).astype(BF16)
            return c

        lax.fori_loop(0, ntile, stage, 0, unroll=4)

    def tiles(g, i):
        nblk = ntile // A_DILATIONS[g]
        res = []
        for j in range(UNROLL):
            t = i * UNROLL + j
            qrows = pl.ds(_aligned(t * BLK), BLK)
            krows = pl.ds(_aligned(t * BLK), 2 * BLK)
            bias = [bias_ref[g, hh] for hh in range(2)]
            if nblk <= UNROLL and j % nblk == 0:
                bias = [_first_bias(b) for b in bias]
            elif nblk > UNROLL and j == 0:
                bias = [_first_bias(b, i == 0) for b in bias]
            res.append(_pair_tile(qs_ref[g, qrows, :], kc_ref[g, 0, krows, :], kc_ref[g, 1, krows, :],
                                  vc_ref[g, 0, krows, :], vc_ref[g, 1, krows, :], bias[0], bias[1]))
        return res

    for g in range(3):
        def body(i, c, g=g):
            res = tiles(g, i)
            for j in range(UNROLL):
                t = i * UNROLL + j
                rows = pl.ds(pl.multiple_of(t * CLS_PITCH, 8), BLK) if g == 2 else src_rows(g, t)
                ru_ref[g, rows, :], rm_ref[g, rows, :], rl_ref[g, rows, :] = res[j]
            return c

        lax.fori_loop(0, ntile // UNROLL, body, 0)

    def natural_rows(ref, i):
        parts = [ref[2, pl.ds((v % 2) * 8 * CLS_PITCH + i * (BLK // d2) + v // 2, 8, stride=CLS_PITCH), :]
                 for v in range(BLK // 8)]
        return jnp.concatenate(parts, axis=0)

    def combine(i, c):
        rows = pl.ds(pl.multiple_of(i * BLK, BLK), BLK)
        ms = [rm_ref[0, rows, :], rm_ref[1, rows, :], natural_rows(rm_ref, i)]
        mx = jnp.maximum(jnp.maximum(ms[0], ms[1]), ms[2])
        ws = [jnp.exp(m - mx) for m in ms]
        num = ws[0] * ru_ref[0, rows, :] + ws[1] * ru_ref[1, rows, :] + ws[2] * natural_rows(ru_ref, i)
        den = ws[0] * rl_ref[0, rows, :] + ws[1] * rl_ref[1, rows, :] + ws[2] * natural_rows(rl_ref, i)
        o_ref[rows, :] = num / den
        return c

    lax.fori_loop(0, ntile, combine, 0, unroll=2)


def _dil_call(qa, ka, va, bias_a, batch, seq, rider=None):
    assert seq == A_DILATIONS[2] * BLK and UNROLL % (seq // (A_DILATIONS[1] * BLK)) == 0
    blk = pl.BlockSpec((None, seq, 128), lambda b, p: (b, 0, p))
    pairs = H_A // 2
    in_specs = [blk, blk, blk, pl.BlockSpec((3, 2, BLK, 2 * BLK), lambda b, p: (0, p, 0, 0))]
    args = [qa.reshape(batch, seq, 512), ka.reshape(batch, seq, 512), va.reshape(batch, seq, 512), bias_a]
    out_specs = [blk]
    out_shape = [jax.ShapeDtypeStruct((batch, seq, 512), F32)]
    if rider is not None:
        r_in, r_out = _rider_specs(rider[1], lambda b, p: b * pairs + p)
        in_specs += r_in
        args += list(rider[0])
        out_specs.append(r_out)
        out_shape.append(jax.ShapeDtypeStruct((batch * pairs, 1, 512), F32))
    return pl.pallas_call(
        functools.partial(_dil_kernel, rider=rider is not None),
        grid=(batch, pairs),
        in_specs=in_specs,
        out_specs=out_specs,
        out_shape=out_shape,
        scratch_shapes=[pltpu.VMEM((3, seq, 128), BF16), pltpu.VMEM((3, 2, PAD + seq, 128), BF16),
                        pltpu.VMEM((3, 2, PAD + seq, 128), BF16), pltpu.VMEM((3, seq, 128), F32)]
                       + [pltpu.VMEM((3, A_DILATIONS[2] * CLS_PITCH, 128), F32)] * 3,
        compiler_params=_cparams(("arbitrary", "arbitrary")),
        name="dilated_attn",
    )(*args)


def _swa_kernel(sink_ref, q_ref, k_ref, v_ref, bias_ref, o_ref, qs_ref, kc_ref, vc_ref):
    scale = HD_B ** -0.5
    seq = q_ref.shape[0]
    ntile = seq // BLK
    p = pl.program_id(1)
    kv = p // 2
    sinks = (sink_ref[2 * p], sink_ref[2 * p + 1])
    low = _low_lanes(BLK)
    own_half = (lax.broadcasted_iota(jnp.int32, (BLK, 128), 1) >= 64).astype(jnp.int32) == kv

    def dup(xf):
        return jnp.where(own_half, xf, pltpu.roll(xf, 64, 1))

    zero = jnp.zeros((PAD, 128), BF16)
    for hh in range(2):
        kc_ref[hh, 0:PAD, :] = zero
        vc_ref[hh, 0:PAD, :] = zero

    def stage(t, c):
        src = pl.ds(pl.multiple_of(t * BLK, BLK), BLK)
        dstp = pl.ds(pl.multiple_of(PAD + t * BLK, BLK), BLK)
        qs_ref[src, :] = (q_ref[src, :] * scale).astype(BF16)
        k = dup(k_ref[src, :])
        v = dup(v_ref[src, :])
        kc_ref[0, dstp, :] = jnp.where(low, k, 0.0).astype(BF16)
        kc_ref[1, dstp, :] = jnp.where(low, 0.0, k).astype(BF16)
        vc_ref[0, dstp, :] = jnp.where(low, v, 0.0).astype(BF16)
        vc_ref[1, dstp, :] = jnp.where(low, 0.0, v).astype(BF16)
        return c

    lax.fori_loop(0, ntile, stage, 0, unroll=4)

    def body(i, c):
        res = []
        for j in range(UNROLL):
            t = i * UNROLL + j
            qrows = pl.ds(pl.multiple_of(t * BLK, BLK), BLK)
            krows = pl.ds(pl.multiple_of(t * BLK, BLK), 2 * BLK)
            bias = [bias_ref[hh] for hh in range(2)]
            if j == 0:
                bias = [_first_bias(b, i == 0) for b in bias]
            u, m, l = _pair_tile(qs_ref[qrows, :], kc_ref[0, krows, :], kc_ref[1, krows, :],
                                 vc_ref[0, krows, :], vc_ref[1, krows, :], bias[0], bias[1], sinks)
            sk = jnp.where(low, sinks[0], sinks[1])
            res.append(u / (l + jnp.exp(sk - m)))
        for j in range(UNROLL):
            o_ref[pl.ds(pl.multiple_of((i * UNROLL + j) * BLK, BLK), BLK), :] = res[j]
        return c

    lax.fori_loop(0, ntile // UNROLL, body, 0)


def _swa_call(sinks, qb, kb, vb, bias_b, batch, seq):
    qblk = pl.BlockSpec((None, seq, 128), lambda b, p: (b, 0, p))
    kblk = pl.BlockSpec((None, seq, 128), lambda b, p: (b, 0, 0))
    return pl.pallas_call(
        _swa_kernel,
        grid=(batch, H_B // 2),
        in_specs=[pl.BlockSpec(memory_space=pltpu.SMEM), qblk, kblk, kblk,
                  pl.BlockSpec((2, BLK, 2 * BLK), lambda b, p: (p, 0, 0))],
        out_specs=qblk,
        out_shape=jax.ShapeDtypeStruct((batch, seq, 512), F32),
        scratch_shapes=[pltpu.VMEM((seq, 128), BF16), pltpu.VMEM((2, PAD + seq, 128), BF16),
                        pltpu.VMEM((2, PAD + seq, 128), BF16)],
        compiler_params=_cparams(("arbitrary", "arbitrary")),
        name="swa_attn",
    )(sinks, qb.reshape(batch, seq, 512), kb.reshape(batch, seq, 128), vb.reshape(batch, seq, 128), bias_b)


def _mem_attn_kernel(q_ref, mk_ref, mv_ref, o_ref):
    scale = HD_C ** -0.5
    for h in range(H_C):
        cs = slice(h * HD_C, (h + 1) * HD_C)
        s = _dot_nt(q_ref[:, cs].astype(BF16), mk_ref[:, cs].astype(BF16)) * scale
        m = jnp.max(s, axis=-1, keepdims=True)
        p = jnp.exp(s - m)
        l = jnp.sum(p, axis=-1, keepdims=True)
        o_ref[:, cs] = _dot(p.astype(BF16), mv_ref[:, cs].astype(BF16)) / l


def _mem_attn_call(qc, mk, mv, batch, seq):
    tq = 512
    return pl.pallas_call(
        _mem_attn_kernel,
        grid=(batch, seq // tq),
        in_specs=[pl.BlockSpec((None, tq, 512), lambda b, i: (b, i, 0)),
                  pl.BlockSpec((None, N_MEM, 512), lambda b, i: (b, 0, 0)),
                  pl.BlockSpec((None, N_MEM, 512), lambda b, i: (b, 0, 0))],
        out_specs=pl.BlockSpec((None, tq, 512), lambda b, i: (b, i, 0)),
        out_shape=jax.ShapeDtypeStruct((batch, seq, 512), F32),
        compiler_params=_cparams(("arbitrary", "arbitrary")),
        name="mem_attn",
    )(qc.reshape(batch, seq, 512), mk.reshape(batch, N_MEM, 512), mv.reshape(batch, N_MEM, 512))


def _head_rows(row, width, head_dim):
    rid = lax.broadcasted_iota(jnp.int32, (8, width), 0)
    hid = lax.broadcasted_iota(jnp.int32, (8, width), 1) // head_dim
    return rid == hid, jnp.where(rid == hid, jnp.broadcast_to(row, (8, width)), 0.0)


def _samp_a_one(q, kn, vn, kt_ref, vt_ref, bias_ref, mult_ref, b0_ref):
    scale = HD_A ** -0.5
    nt = W_A // 128
    own, qh = _head_rows(q, 512, HD_A)
    qcols = jnp.transpose(jnp.broadcast_to(q, (128, 512)))
    s_heads = []
    for h in range(H_A):
        rows = slice(h * HD_A, (h + 1) * HD_A)
        qb = qcols[rows, :]
        s_heads.append(jnp.concatenate(
            [jnp.sum(kt_ref[rows, 128 * j:128 * (j + 1)] * qb, axis=0, keepdims=True) for j in range(nt)], axis=1))
    s = jnp.concatenate(s_heads, axis=0) * scale + bias_ref[...]
    ln = jnp.sum(qh * kn, axis=-1, keepdims=True) * scale + b0_ref[:, 0:1]
    m = jnp.maximum(jnp.max(s, axis=-1, keepdims=True), ln)
    p = jnp.exp(s - m) * mult_ref[...]
    pn = len(A_DILATIONS) * jnp.exp(ln - m)
    den = jnp.sum(p, axis=-1, keepdims=True) + pn
    accs = []
    for h in range(H_A):
        rows = slice(h * HD_A, (h + 1) * HD_A)
        acc = vt_ref[rows, 0:128] * p[h:h + 1, 0:128]
        for j in range(1, nt):
            acc = acc + vt_ref[rows, 128 * j:128 * (j + 1)] * p[h:h + 1, 128 * j:128 * (j + 1)]
        accs.append(acc)
    pv = jnp.sum(jnp.transpose(jnp.concatenate(accs, axis=0)), axis=0, keepdims=True)
    lanes = lambda col: jnp.sum(jnp.where(own, col, 0.0), axis=0, keepdims=True)
    return (pv + lanes(pn) * vn) / lanes(den)


N_RIDER_IN = 8


def _rider_arrays(qa, kn, vn, cache_k, cache_v, bias_sa, mult, s0a):
    nb = qa.shape[0]
    kt = jnp.transpose(cache_k, (0, 2, 3, 1)).reshape(nb, 512, W_A)
    vt = jnp.transpose(cache_v, (0, 2, 3, 1)).reshape(nb, 512, W_A)
    return (qa.reshape(nb, 1, 512), kn.reshape(nb, 1, 512), vn.reshape(nb, 1, 512), kt, vt, bias_sa, mult, s0a)


def _rider_specs(first, step_of):
    row = pl.BlockSpec((None, 1, 512), lambda *ids: (first + step_of(*ids), 0, 0))
    cblk = pl.BlockSpec((None, 512, W_A), lambda *ids: (first + step_of(*ids), 0, 0))
    ins = [row, row, row, cblk, cblk, _const_spec((8, W_A)), _const_spec((1, W_A)), _const_spec((8, 128))]
    return ins, pl.BlockSpec((None, 1, 512), lambda *ids: (step_of(*ids), 0, 0))


def _ride(in_refs, o_ref):
    q_ref, kn_ref, vn_ref, kt_ref, vt_ref, bias_ref, mult_ref, b0_ref = in_refs
    o_ref[...] = _samp_a_one(q_ref[...], kn_ref[...], vn_ref[...], kt_ref, vt_ref, bias_ref, mult_ref, b0_ref)


def _samp_a_kernel(*refs):
    _ride(refs[:N_RIDER_IN], refs[-1])


def _samp_a_call(arrays, first, count):
    r_in, r_out = _rider_specs(first, lambda i: i)
    return pl.pallas_call(
        _samp_a_kernel,
        grid=(count,),
        in_specs=r_in,
        out_specs=r_out,
        out_shape=jax.ShapeDtypeStruct((count, 1, 512), F32),
        compiler_params=_cparams(("arbitrary",)),
        name="sample_dilated_attn",
    )(*arrays)


def _samp_b_kernel(q_ref, kn_ref, vn_ref, kt_ref, vt_ref, bias_ref, b0_ref, sk_ref, o_ref):
    scale = HD_B ** -0.5
    low = lax.broadcasted_iota(jnp.int32, (1, 128), 1) < 64
    rid = lax.broadcasted_iota(jnp.int32, (8, 128), 0)
    kv_half = (lax.broadcasted_iota(jnp.int32, (8, 128), 1) >= 64) == (rid >= 4)

    bb = q_ref.shape[0]
    zero = jnp.zeros((8, 128), F32)
    qhs = []
    for b in range(bb):
        q = q_ref[b:b + 1, :]
        rows = []
        for h in range(H_B):
            chunk = q[:, 128 * (h // 2):128 * (h // 2) + 128]
            if (h % 2) != (h // 4):
                chunk = pltpu.roll(chunk, 64, 1)
            rows.append(chunk)
        qhs.append(jnp.where(kv_half, jnp.concatenate(rows, axis=0), 0.0))
    qblk = jnp.concatenate([jnp.concatenate([qhs[b] if c == b else zero for c in range(bb)], axis=1)
                            for b in range(bb)], axis=0)
    kt = kt_ref[...].reshape(bb * 128, W_B).astype(BF16)
    vt = vt_ref[...].reshape(bb * 128, W_B).astype(BF16)
    tile8 = lambda ref: jnp.concatenate([ref[...]] * bb, axis=0)
    s = _dot(qblk.astype(BF16), kt) * scale + tile8(bias_ref)
    knew = jnp.concatenate([jnp.broadcast_to(kn_ref[b:b + 1, :], (8, 128)) for b in range(bb)], axis=0)
    vnew = jnp.concatenate([jnp.broadcast_to(vn_ref[b:b + 1, :], (8, 128)) for b in range(bb)], axis=0)
    qall = jnp.concatenate(qhs, axis=0)
    ln = jnp.sum(qall * knew, axis=-1, keepdims=True) * scale + tile8(b0_ref)[:, 0:1]
    sk = tile8(sk_ref)[:, 0:1]
    m = jnp.maximum(jnp.maximum(jnp.max(s, axis=-1, keepdims=True), ln), sk)
    p = jnp.exp(s - m)
    pn = jnp.exp(ln - m)
    den = jnp.sum(p, axis=-1, keepdims=True) + pn + jnp.exp(sk - m)
    rall = _dot_nt(p.astype(BF16), vt)
    for b in range(bb):
        r = (rall[8 * b:8 * b + 8, 128 * b:128 * b + 128] + pn[8 * b:8 * b + 8] * vnew[8 * b:8 * b + 8])
        r = jnp.where(kv_half, r / den[8 * b:8 * b + 8], 0.0)
        chunks = []
        for c2 in range(4):
            pair = []
            for h in (2 * c2, 2 * c2 + 1):
                piece = r[h:h + 1, :]
                if (h % 2) != (h // 4):
                    piece = pltpu.roll(piece, 64, 1)
                pair.append(piece)
            chunks.append(jnp.where(low, pair[0], pair[1]))
        o_ref[b:b + 1, :] = jnp.concatenate(chunks, axis=1)


def _samp_b_call(qb, kn, vn, cache_k, cache_v, bias_sb, s0b, skb):
    nb = qb.shape[0]
    bb = 8
    kt = jnp.transpose(cache_k, (0, 2, 3, 1)).reshape(nb, 128, W_B)
    vt = jnp.transpose(cache_v, (0, 2, 3, 1)).reshape(nb, 128, W_B)
    row = lambda w: pl.BlockSpec((bb, w), lambda i: (i, 0))
    cblk = pl.BlockSpec((bb, 128, W_B), lambda i: (i, 0, 0))
    return pl.pallas_call(
        _samp_b_kernel,
        grid=(nb // bb,),
        in_specs=[row(512), row(128), row(128), cblk, cblk, _const_spec((8, W_B)),
                  _const_spec((8, 128)), _const_spec((8, 128))],
        out_specs=row(512),
        out_shape=jax.ShapeDtypeStruct((nb, 512), F32),
        compiler_params=_cparams(("arbitrary",)),
        name="sample_swa_attn",
    )(qb, kn, vn, kt, vt, bias_sb, s0b, skb)


def _samp_c_kernel(q_ref, mk_ref, mv_ref, mask_ref, o_ref):
    scale = HD_C ** -0.5
    zeros = jnp.zeros((4, HD_C), F32)

    def body(b, c):
        q8 = jnp.concatenate([q_ref[b], zeros], axis=0).astype(BF16)
        s = _dot_nt(q8, mk_ref[b].astype(BF16)) * scale + mask_ref[...]
        m = jnp.max(s, axis=-1, keepdims=True)
        p = jnp.exp(s - m)
        den = jnp.sum(p, axis=-1, keepdims=True)
        r = _dot(p.astype(BF16), mv_ref[b].astype(BF16)) / den
        o_ref[b] = r[0:4, :]
        return c

    lax.fori_loop(0, q_ref.shape[0], body, 0, unroll=True)


def _samp_c_call(qc, cache_k, cache_v):
    nb = qc.shape[0]
    bb = 8
    rows = N_MEM * H_C
    head_of_row = np.arange(rows)[None, :] % H_C
    mask = np.where(head_of_row == np.arange(8)[:, None], 0.0, NEG).astype(np.float32)
    qblk = pl.BlockSpec((bb, H_C, HD_C), lambda i: (i, 0, 0))
    cblk = pl.BlockSpec((bb, rows, HD_C), lambda i: (i, 0, 0))
    out = pl.pallas_call(
        _samp_c_kernel,
        grid=(nb // bb,),
        in_specs=[qblk, cblk, cblk, _const_spec((8, rows))],
        out_specs=qblk,
        out_shape=jax.ShapeDtypeStruct((nb, H_C, HD_C), F32),
        compiler_params=_cparams(("arbitrary",)),
        name="sample_mem_attn",
    )(qc.reshape(nb, H_C, HD_C), cache_k.reshape(nb, rows, HD_C), cache_v.reshape(nb, rows, HD_C),
      jnp.asarray(mask))
    return out.reshape(nb, 512)


def kernel(x_prompt, x_sample, mem_prompt, cache_a_k, cache_a_v, cache_b_k, cache_b_v, cache_mem_k,
           cache_mem_v, rel_bias, ln_g, w_in, gq_a, gk_a, gq_b, gk_b, gq_c, gk_c, sinks_b, mem_ln_g,
           w_mem_kv, w_br_a, w_br_b, w_br_c, w_out):
    batch, seq, _ = x_prompt.shape
    nsamp = x_sample.shape[0]
    assert ln_g.shape[0] == 1 and x_sample.shape[1] == 1
    assert (batch, seq, nsamp) == (8, 2048, 128) and w_in.shape == (1, D_MODEL, C_END)
    assert cache_a_k.shape == (1, nsamp, W_A, H_A, HD_A) and cache_b_k.shape == (1, nsamp, W_B, KV_B, HD_B)

    w_bf = w_in[0].astype(BF16)
    wmem_bf = w_mem_kv[0].astype(BF16)
    wbr_bf = jnp.stack([w_br_a[0], w_br_b[0], w_br_c[0]]).astype(BF16)
    wout_bf = w_out[0].astype(BF16)
    lng = ln_g.reshape(1, D_MODEL)
    gains = jnp.stack([jnp.tile(gq_a[0], 8), jnp.tile(gk_a[0], 8), jnp.tile(gq_b[0], 8),
                       jnp.tile(gk_b[0], 8), jnp.tile(gq_c[0], 4)])
    gk_c4 = jnp.tile(gk_c[0], 4).reshape(1, 512)
    blockdiag = np.kron(np.eye(4, dtype=np.float32), np.ones((64, 64), np.float32))
    pmat = jnp.asarray(blockdiag, BF16)

    *tables, mult_a = _bucket_tables()
    bias_a, bias_b, bias_sa, bias_sb, s0a, s0b, skb = _bias_call(rel_bias, sinks_b.reshape(H_B), tables)

    xs2 = x_sample.reshape(nsamp, D_MODEL)
    sqa, ska, sva, sqb, skb_new, svb, sqc = _qkv_call(xs2, lng, w_bf, gains, pmat)
    riders = _rider_arrays(sqa, ska, sva, cache_a_k[0], cache_a_v[0], bias_sa, jnp.asarray(mult_a), s0a)
    out_tm = 256
    n_qkv = batch * seq // 512
    n_dil = batch * (H_A // 2)
    n_out = batch * seq // out_tm
    n_alone = nsamp - n_qkv - n_dil - n_out
    assert n_alone >= 0

    xp2 = x_prompt.reshape(batch * seq, D_MODEL)
    qa, ka, va, qb, kb, vb, qc, ka_t, va_t, kb_t, vb_t, soa_1 = _qkv_call(
        xp2, lng, w_bf, gains, pmat, (batch, seq), rider=(riders, 0))
    mk, mv = _memkv_call(mem_prompt.reshape(batch * N_MEM, D_MODEL), mem_ln_g.reshape(1, D_MODEL),
                         wmem_bf, gk_c4)
    oa, soa_2 = _dil_call(qa, ka, va, bias_a, batch, seq, rider=(riders, n_qkv))
    ob = _swa_call(sinks_b.reshape(H_B), qb, kb, vb, bias_b, batch, seq)
    oc = _mem_attn_call(qc, mk, mv, batch, seq)
    yp, soa_3 = _out_call(xp2, oa.reshape(-1, 512), ob.reshape(-1, 512), oc.reshape(-1, 512),
                          lng, w_bf, wbr_bf, wout_bf, tm=out_tm, rider=(riders, n_qkv + n_dil))
    yp = yp.reshape(batch, seq, D_MODEL)

    soa_parts = [soa_1, soa_2, soa_3]
    if n_alone:
        soa_parts.append(_samp_a_call(riders, n_qkv + n_dil + n_out, n_alone))
    soa = jnp.concatenate(soa_parts, axis=0).reshape(nsamp, 512)
    sob = _samp_b_call(sqb, skb_new, svb, cache_b_k[0], cache_b_v[0], bias_sb, s0b, skb)
    soc = _samp_c_call(sqc, cache_mem_k[0], cache_mem_v[0])
    ys, = _out_call(xs2, soa, sob, soc, lng, w_bf, wbr_bf, wout_bf)

    return (yp, ys.reshape(nsamp, 1, D_MODEL),
            jnp.transpose(ka_t.reshape(1, batch, H_A, HD_A, seq), (0, 1, 4, 2, 3)),
            jnp.transpose(va_t.reshape(1, batch, H_A, HD_A, seq), (0, 1, 4, 2, 3)),
            jnp.transpose(kb_t.reshape(1, batch, KV_B, HD_B, W_B), (0, 1, 4, 2, 3)),
            jnp.transpose(vb_t.reshape(1, batch, KV_B, HD_B, W_B), (0, 1, 4, 2, 3)),
            mk.reshape(1, batch, N_MEM, H_C, HD_C), mv.reshape(1, batch, N_MEM, H_C, HD_C),
            ska.reshape(1, nsamp, 1, H_A, HD_A), sva.reshape(1, nsamp, 1, H_A, HD_A),
            skb_new.reshape(1, nsamp, 1, KV_B, HD_B), svb.reshape(1, nsamp, 1, KV_B, HD_B))
```

```python
import functools
import math

import numpy as np
import jax
import jax.numpy as jnp
from jax import lax
from jax.experimental import pallas as pl
from jax.experimental.pallas import tpu as pltpu

F32 = jnp.float32
BF16 = jnp.bfloat16

D_MODEL = 1024
H_A, HD_A = 8, 64
A_DILATIONS = (1, 4, 16)
A_STEPS = 128
W_A = 2048
H_B, KV_B, HD_B, W_B = 8, 2, 64, 128
H_C, HD_C, N_MEM = 4, 128, 256
NUM_BUCKETS, MAX_DISTANCE = 32, 2048
EPS = 1e-6
NEG = -1e30
BLK = 128
UNROLL = 16

C_QA, C_KA, C_VA, C_ZA = 0, 512, 1024, 1536
C_QB, C_KB, C_VB, C_ZB = 2048, 2560, 2688, 2816
C_QC, C_ZC = 3328, 3840
C_GA, C_GB, C_GC, C_END = 4352, 5376, 6400, 7424

VMEM_LIMIT = 56 * 1024 * 1024


def _cparams(sem):
    return pltpu.CompilerParams(dimension_semantics=sem, vmem_limit_bytes=VMEM_LIMIT)


def _const_spec(shape):
    nd = len(shape)
    return pl.BlockSpec(shape, lambda *_: (0,) * nd, pipeline_mode=pl.Buffered(1))


def _t5_bucket_np(dist):
    n = np.maximum(dist, 0)
    max_exact = NUM_BUCKETS // 2
    nf = np.maximum(n, 1).astype(np.float32)
    large = max_exact + (np.log(nf / np.float32(max_exact))
                         / np.float32(math.log(MAX_DISTANCE / max_exact))
                         * np.float32(NUM_BUCKETS - max_exact)).astype(np.int32)
    return np.where(n < max_exact, n, np.minimum(large, NUM_BUCKETS - 1)).astype(np.int32)


def _a_multiplicity(dist):
    return sum(((dist % d == 0) & (dist <= A_STEPS * d)).astype(np.int32) for d in A_DILATIONS)


def _bucket_tables():
    i = np.arange(BLK)[:, None]
    j = np.arange(2 * BLK)[None, :]
    rel = i + BLK - j
    bkt_a = np.stack([np.where((rel >= 0) & (rel <= A_STEPS), _t5_bucket_np(rel * d), -1)
                      for d in A_DILATIONS]).astype(np.int32)
    bkt_b = np.where((rel >= 0) & (rel < W_B), _t5_bucket_np(rel), -1).astype(np.int32)
    dist_a = W_A - np.arange(W_A)
    mult_a = _a_multiplicity(dist_a)
    sb_a = np.broadcast_to(np.where(mult_a > 0, _t5_bucket_np(dist_a), -1), (8, W_A)).astype(np.int32)
    dist_b = W_B - np.arange(W_B)
    sb_b = np.broadcast_to(np.where(dist_b < W_B, _t5_bucket_np(dist_b), -1), (8, W_B)).astype(np.int32)
    return bkt_a, bkt_b, sb_a, sb_b, mult_a.astype(np.float32).reshape(1, W_A)


def _rms_rows(x, g):
    return x * lax.rsqrt(jnp.mean(x * x, axis=-1, keepdims=True) + EPS) * g


def _dot(a, b):
    return jnp.dot(a, b, preferred_element_type=F32)


def _dot_nt(a, b):
    return lax.dot_general(a, b, (((1,), (1,)), ((), ())), preferred_element_type=F32)


def _headnorm64(t, g, pmat):
    sq = (t * t).astype(BF16)
    width = t.shape[1]
    step = min(width, 256)
    pp = pmat[:step, :step]
    parts = [_dot(sq[:, c:c + step], pp) for c in range(0, width, step)]
    ss = parts[0] if len(parts) == 1 else jnp.concatenate(parts, axis=1)
    return t * lax.rsqrt(ss * (1.0 / 64.0) + EPS) * g


def _headnorm128(t, g):
    parts = []
    for c in range(0, t.shape[1], 128):
        seg = t[:, c:c + 128]
        parts.append(seg * lax.rsqrt(jnp.mean(seg * seg, axis=-1, keepdims=True) + EPS))
    return jnp.concatenate(parts, axis=1) * g


def _bias_kernel(tbl_ref, sink_ref, bkta_ref, bktb_ref, sba_ref, sbb_ref,
                 ba_ref, bb_ref, oa_ref, ob_ref, s0a_ref, s0b_ref, sk_ref, *, present):
    pa, pb, psa, psb = present
    for g in range(3):
        bk = bkta_ref[g]
        for h in range(H_A):
            ba_ref[g, h] = jnp.full((BLK, 2 * BLK), NEG, F32)
        for b in pa[g]:
            hit = bk == b
            for h in range(H_A):
                ba_ref[g, h] = jnp.where(hit, tbl_ref[b, h], ba_ref[g, h])
    bk = bktb_ref[...]
    for h in range(H_B):
        bb_ref[h] = jnp.full((BLK, 2 * BLK), NEG, F32)
    for b in pb:
        hit = bk == b
        for h in range(H_B):
            bb_ref[h] = jnp.where(hit, tbl_ref[b, H_A + h], bb_ref[h])
    for (src, dst, buckets, col0) in ((sba_ref, oa_ref, psa, 0), (sbb_ref, ob_ref, psb, H_A)):
        bk = src[...]
        row = lax.broadcasted_iota(jnp.int32, bk.shape, 0)
        acc = jnp.full(bk.shape, NEG, F32)
        for b in buckets:
            hit = bk == b
            for h in range(8):
                acc = jnp.where(hit & (row == h), tbl_ref[b, col0 + h], acc)
        dst[...] = acc
    row8 = lax.broadcasted_iota(jnp.int32, (8, 128), 0)
    a0 = jnp.zeros((8, 128), F32)
    b0 = jnp.zeros((8, 128), F32)
    sk = jnp.zeros((8, 128), F32)
    for h in range(8):
        a0 = jnp.where(row8 == h, tbl_ref[0, h], a0)
        b0 = jnp.where(row8 == h, tbl_ref[0, H_A + h], b0)
        sk = jnp.where(row8 == h, sink_ref[h], sk)
    s0a_ref[...] = a0
    s0b_ref[...] = b0
    sk_ref[...] = sk


def _bias_call(rel_bias, sinks, tables):
    bkt_a, bkt_b, sb_a, sb_b = tables
    uniq = lambda a: tuple(int(b) for b in np.unique(a) if b >= 0)
    present = (tuple(uniq(bkt_a[g]) for g in range(3)), uniq(bkt_b), uniq(sb_a), uniq(sb_b))
    smem = pl.BlockSpec(memory_space=pltpu.SMEM)
    vmem = pl.BlockSpec(memory_space=pltpu.VMEM)
    return pl.pallas_call(
        functools.partial(_bias_kernel, present=present),
        in_specs=[smem, smem, vmem, vmem, vmem, vmem],
        out_specs=[vmem] * 7,
        out_shape=[jax.ShapeDtypeStruct((3, H_A, BLK, 2 * BLK), F32),
                   jax.ShapeDtypeStruct((H_B, BLK, 2 * BLK), F32),
                   jax.ShapeDtypeStruct((8, W_A), F32),
                   jax.ShapeDtypeStruct((8, W_B), F32),
                   jax.ShapeDtypeStruct((8, 128), F32),
                   jax.ShapeDtypeStruct((8, 128), F32),
                   jax.ShapeDtypeStruct((8, 128), F32)],
        compiler_params=pltpu.CompilerParams(vmem_limit_bytes=VMEM_LIMIT),
        name="bias_expand",
    )(rel_bias, sinks, jnp.asarray(bkt_a), jnp.asarray(bkt_b), jnp.asarray(sb_a), jnp.asarray(sb_b))


def _qkv_kernel(x_ref, lng_ref, w_ref, g_ref, p_ref, *refs, rider=False, tiles_per_seq=1):
    if rider:
        _ride(refs[:N_RIDER_IN], refs[-1])
        refs = refs[N_RIDER_IN:-1]
    qa_ref, ka_ref, va_ref, qb_ref, kb_ref, vb_ref, qc_ref, *cache_refs = refs
    h = _rms_rows(x_ref[...], lng_ref[...]).astype(BF16)
    pmat = p_ref[...]

    def proj(c0, c1):
        return _dot(h, w_ref[:, c0:c1])

    qa_ref[...] = _headnorm64(proj(C_QA, C_KA), g_ref[0:1, :], pmat)
    ka = _headnorm64(proj(C_KA, C_VA), g_ref[1:2, :], pmat)
    va = proj(C_VA, C_ZA)
    ka_ref[...] = ka
    va_ref[...] = va
    kb = _headnorm64(proj(C_KB, C_VB), g_ref[3:4, 0:128], pmat)
    vb = proj(C_VB, C_ZB)
    if cache_refs:
        kt_ref, vt_ref, kbt_ref, vbt_ref = cache_refs
        kt_ref[...] = ka.T
        vt_ref[...] = va.T

        @pl.when((pl.program_id(0) + 1) % tiles_per_seq == 0)
        def _():
            tm = kb.shape[0]
            kbt_ref[...] = kb[tm - W_B:, :].T
            vbt_ref[...] = vb[tm - W_B:, :].T
    qb_ref[...] = _headnorm64(proj(C_QB, C_KB), g_ref[2:3, :], pmat).astype(qb_ref.dtype)
    kb_ref[...] = kb
    vb_ref[...] = vb
    qc_ref[...] = _headnorm128(proj(C_QC, C_ZC), g_ref[4:5, :]).astype(qc_ref.dtype)


def _qkv_call(x2, lng, w_bf, gains, pmat, cache_layout_for=None, rider=None):
    n = x2.shape[0]
    tm = min(512, n)
    steps = n // tm
    row = lambda w: pl.BlockSpec((tm, w), lambda i: (i, 0))
    widths = (512, 512, 512, 512, 128, 128, 512)
    in_specs = [row(D_MODEL), _const_spec((1, D_MODEL)), _const_spec((D_MODEL, C_END)),
                _const_spec((5, 512)), _const_spec((256, 256))]
    args = [x2, lng, w_bf, gains, pmat]
    out_specs = [row(w) for w in widths]
    q_dtype = F32 if cache_layout_for is None else BF16
    out_shape = [jax.ShapeDtypeStruct((n, w), q_dtype if i in (3, 6) else F32) for i, w in enumerate(widths)]
    per = 1
    if cache_layout_for is not None:
        batch, seq = cache_layout_for
        per = seq // tm
        out_specs += [pl.BlockSpec((None, 512, tm), lambda i: (i // per, 0, i % per))] * 2
        out_shape += [jax.ShapeDtypeStruct((batch, 512, seq), F32)] * 2
        out_specs += [pl.BlockSpec((None, KV_B * HD_B, W_B), lambda i: (i // per, 0, 0))] * 2
        out_shape += [jax.ShapeDtypeStruct((batch, KV_B * HD_B, W_B), F32)] * 2
    if rider is not None:
        r_in, r_out = _rider_specs(rider[1], lambda i: i)
        in_specs += r_in
        args += list(rider[0])
        out_specs.append(r_out)
        out_shape.append(jax.ShapeDtypeStruct((steps, 1, 512), F32))
    return pl.pallas_call(
        functools.partial(_qkv_kernel, rider=rider is not None, tiles_per_seq=per),
        grid=(steps,),
        in_specs=in_specs,
        out_specs=out_specs,
        out_shape=out_shape,
        compiler_params=_cparams(("arbitrary",)),
        name="qkv_proj",
    )(*args)


def _memkv_kernel(m_ref, g_ref, w_ref, gk_ref, mk_ref, mv_ref):
    h = _rms_rows(m_ref[...], g_ref[...]).astype(BF16)
    mk_ref[...] = _headnorm128(_dot(h, w_ref[:, 0:512]), gk_ref[...])
    mv_ref[...] = _dot(h, w_ref[:, 512:1024])


def _memkv_call(mem2, g, w_bf, gk):
    n = mem2.shape[0]
    tm = 512
    row = lambda w: pl.BlockSpec((tm, w), lambda i: (i, 0))
    return pl.pallas_call(
        _memkv_kernel,
        grid=(n // tm,),
        in_specs=[row(D_MODEL), _const_spec((1, D_MODEL)), _const_spec((D_MODEL, 2 * 512)),
                  _const_spec((1, 512))],
        out_specs=[row(512), row(512)],
        out_shape=[jax.ShapeDtypeStruct((n, 512), F32)] * 2,
        compiler_params=_cparams(("arbitrary",)),
        name="mem_kv_proj",
    )(mem2, g, w_bf, gk)


def _out_kernel(x_ref, oa_ref, ob_ref, oc_ref, lng_ref, w_ref, wbr_ref, wout_ref, *refs):
    if len(refs) > 1:
        _ride(refs[:N_RIDER_IN], refs[-1])
    y_ref = refs[-2] if len(refs) > 1 else refs[0]
    x = x_ref[...]
    h = _rms_rows(x, lng_ref[...]).astype(BF16)
    acc = None
    for br, (o_ref, cz, cg) in enumerate(((oa_ref, C_ZA, C_GA), (ob_ref, C_ZB, C_GB), (oc_ref, C_ZC, C_GC))):
        z = _dot(h, w_ref[:, cz:cz + 512])
        y = (o_ref[...] * (z * jax.nn.sigmoid(z))).astype(BF16)
        yb = _dot(y, wbr_ref[br])
        gate = jax.nn.sigmoid(_dot(h, w_ref[:, cg:cg + D_MODEL]))
        acc = gate * yb if acc is None else acc + gate * yb
    y_ref[...] = x + _dot(acc.astype(BF16), wout_ref[...])


def _out_call(x2, oa, ob, oc, lng, w_bf, wbr_bf, wout_bf, tm=512, rider=None):
    n = x2.shape[0]
    tm = min(tm, n)
    steps = n // tm
    row = lambda w: pl.BlockSpec((tm, w), lambda i: (i, 0))
    in_specs = [row(D_MODEL), row(512), row(512), row(512), _const_spec((1, D_MODEL)),
                _const_spec((D_MODEL, C_END)), _const_spec((3, 512, D_MODEL)), _const_spec((D_MODEL, D_MODEL))]
    args = [x2, oa, ob, oc, lng, w_bf, wbr_bf, wout_bf]
    out_specs = [row(D_MODEL)]
    out_shape = [jax.ShapeDtypeStruct((n, D_MODEL), F32)]
    if rider is not None:
        r_in, r_out = _rider_specs(rider[1], lambda i: i)
        in_specs += r_in
        args += list(rider[0])
        out_specs.append(r_out)
        out_shape.append(jax.ShapeDtypeStruct((steps, 1, 512), F32))
    return pl.pallas_call(
        _out_kernel,
        grid=(steps,),
        in_specs=in_specs,
        out_specs=out_specs,
        out_shape=out_shape,
        compiler_params=_cparams(("arbitrary",)),
        name="gated_out",
    )(*args)


PAD = BLK
CLS_PITCH = BLK + 8


def _low_lanes(rows):
    return lax.broadcasted_iota(jnp.int32, (rows, 128), 1) < 64


def _aligned(start):
    return start if isinstance(start, int) else pl.multiple_of(start, BLK)


def _pair_tile(q, k0, k1, v0, v1, b0, b1, sinks=None):
    nk = k0.shape[0]
    low = _low_lanes(BLK)
    s = _dot_nt(q, jnp.concatenate([k0, k1], axis=0)) + jnp.concatenate([b0, b1], axis=1)
    ms = [jnp.max(s[:, hh * nk:(hh + 1) * nk], axis=-1, keepdims=True) for hh in range(2)]
    if sinks is not None:
        ms = [jnp.maximum(ms[hh], sinks[hh]) for hh in range(2)]
    p = jnp.concatenate([jnp.exp(s[:, hh * nk:(hh + 1) * nk] - ms[hh]) for hh in range(2)], axis=1).astype(BF16)
    own = _low_lanes(nk)
    ones = [jnp.where(own, 1.0, 0.0).astype(BF16), jnp.where(own, 0.0, 1.0).astype(BF16)]
    vblk = jnp.concatenate([jnp.concatenate([v0, ones[0]], axis=1),
                            jnp.concatenate([v1, ones[1]], axis=1)], axis=0)
    up = _dot(p, vblk)
    return up[:, :128], jnp.where(low, ms[0], ms[1]), up[:, 128:]


def _first_bias(bias, also=None):
    gone = lax.broadcasted_iota(jnp.int32, bias.shape, 1) < BLK
    if also is not None:
        gone = gone & also
    return jnp.where(gone, NEG, bias)


def _dil_kernel(q_ref, k_ref, v_ref, bias_ref, *refs, rider=False):
    if rider:
        _ride(refs[:N_RIDER_IN], refs[N_RIDER_IN + 1])
        refs = refs[N_RIDER_IN:N_RIDER_IN + 1] + refs[N_RIDER_IN + 2:]
    o_ref, qs_ref, kc_ref, vc_ref, c4_ref, ru_ref, rm_ref, rl_ref = refs
    scale = HD_A ** -0.5
    seq = q_ref.shape[0]
    ntile = seq // BLK
    low = _low_lanes(BLK)
    d1, d2 = A_DILATIONS[1], A_DILATIONS[2]

    def src_rows(g, t):
        d = A_DILATIONS[g]
        if d == 1:
            return pl.ds(_aligned(t * BLK), BLK)
        nblk = ntile // d
        r, n = t >> (nblk.bit_length() - 1), t & (nblk - 1)
        return pl.ds(r + n * (d * BLK), BLK, stride=d)

    zero = jnp.zeros((PAD, 128), BF16)
    for g in range(3):
        for hh in range(2):
            kc_ref[g, hh, 0:PAD, :] = zero
            vc_ref[g, hh, 0:PAD, :] = zero

        def stage(t, c, g=g):
            dst = pl.ds(pl.multiple_of(t * BLK, BLK), BLK)
            dstp = pl.ds(pl.multiple_of(PAD + t * BLK, BLK), BLK)
            if g < 2:
                src = src_rows(g, t)
                q, k, v = q_ref[src, :], k_ref[src, :], v_ref[src, :]
            else:
                src = pl.ds((t & (d1 - 1)) * (seq // d1) + (t >> (d1.bit_length() - 1)), BLK, stride=d2 // d1)
                q, k, v = c4_ref[0, src, :], c4_ref[1, src, :], c4_ref[2, src, :]
            if g == 1:
                c4_ref[0, dst, :], c4_ref[1, dst, :], c4_ref[2, dst, :] = q, k, v
            qs_ref[g, dst, :] = (q * scale).astype(BF16)
            kc_ref[g, 0, dstp, :] = jnp.where(low, k, 0.0).astype(BF16)
            kc_ref[g, 1, dstp, :] = jnp.where(low, 0.0, k).astype(BF16)
            vc_ref[g, 0, dstp, :] = jnp.where(low, v, 0.0).astype(BF16)
            vc_ref[g, 1, dstp, :] = jnp.where(low, 0.0, v).astype(BF16)
            return c

        lax.fori_loop(0, ntile, stage, 0, unroll=4)

    def tiles(g, i):
        nblk = ntile // A_DILATIONS[g]
        res = []
        for j in range(UNROLL):
            t = i * UNROLL + j
            qrows = pl.ds(_aligned(t * BLK), BLK)
            krows = pl.ds(_aligned(t * BLK), 2 * BLK)
            bias = [bias_ref[g, hh] for hh in range(2)]
            if nblk <= UNROLL and j % nblk == 0:
                bias = [_first_bias(b) for b in bias]
            elif nblk > UNROLL and j == 0:
                bias = [_first_bias(b, i == 0) for b in bias]
            res.append(_pair_tile(qs_ref[g, qrows, :], kc_ref[g, 0, krows, :], kc_ref[g, 1, krows, :],
                                  vc_ref[g, 0, krows, :], vc_ref[g, 1, krows, :], bias[0], bias[1]))
        return res

    for g in range(3):
        def body(i, c, g=g):
            res = tiles(g, i)
            for j in range(UNROLL):
                t = i * UNROLL + j
                rows = pl.ds(pl.multiple_of(t * CLS_PITCH, 8), BLK) if g == 2 else src_rows(g, t)
                ru_ref[g, rows, :], rm_ref[g, rows, :], rl_ref[g, rows, :] = res[j]
            return c

        lax.fori_loop(0, ntile // UNROLL, body, 0)

    def natural_rows(ref, i):
        parts = [ref[2, pl.ds((v % 2) * 8 * CLS_PITCH + i * (BLK // d2) + v // 2, 8, stride=CLS_PITCH), :]
                 for v in range(BLK // 8)]
        return jnp.concatenate(parts, axis=0)

    def combine(i, c):
        rows = pl.ds(pl.multiple_of(i * BLK, BLK), BLK)
        ms = [rm_ref[0, rows, :], rm_ref[1, rows, :], natural_rows(rm_ref, i)]
        mx = jnp.maximum(jnp.maximum(ms[0], ms[1]), ms[2])
        ws = [jnp.exp(m - mx) for m in ms]
        num = ws[0] * ru_ref[0, rows, :] + ws[1] * ru_ref[1, rows, :] + ws[2] * natural_rows(ru_ref, i)
        den = ws[0] * rl_ref[0, rows, :] + ws[1] * rl_ref[1, rows, :] + ws[2] * natural_rows(rl_ref, i)
        o_ref[rows, :] = num / den
        return c

    lax.fori_loop(0, ntile, combine, 0, unroll=2)


def _dil_call(qa, ka, va, bias_a, batch, seq, rider=None):
    assert seq == A_DILATIONS[2] * BLK and UNROLL % (seq // (A_DILATIONS[1] * BLK)) == 0
    blk = pl.BlockSpec((None, seq, 128), lambda b, p: (b, 0, p))
    pairs = H_A // 2
    in_specs = [blk, blk, blk, pl.BlockSpec((3, 2, BLK, 2 * BLK), lambda b, p: (0, p, 0, 0))]
    args = [qa.reshape(batch, seq, 512), ka.reshape(batch, seq, 512), va.reshape(batch, seq, 512), bias_a]
    out_specs = [blk]
    out_shape = [jax.ShapeDtypeStruct((batch, seq, 512), F32)]
    if rider is not None:
        r_in, r_out = _rider_specs(rider[1], lambda b, p: b * pairs + p)
        in_specs += r_in
        args += list(rider[0])
        out_specs.append(r_out)
        out_shape.append(jax.ShapeDtypeStruct((batch * pairs, 1, 512), F32))
    return pl.pallas_call(
        functools.partial(_dil_kernel, rider=rider is not None),
        grid=(batch, pairs),
        in_specs=in_specs,
        out_specs=out_specs,
        out_shape=out_shape,
        scratch_shapes=[pltpu.VMEM((3, seq, 128), BF16), pltpu.VMEM((3, 2, PAD + seq, 128), BF16),
                        pltpu.VMEM((3, 2, PAD + seq, 128), BF16), pltpu.VMEM((3, seq, 128), F32)]
                       + [pltpu.VMEM((3, A_DILATIONS[2] * CLS_PITCH, 128), F32)] * 3,
        compiler_params=_cparams(("arbitrary", "arbitrary")),
        name="dilated_attn",
    )(*args)


def _swa_kernel(sink_ref, q_ref, k_ref, v_ref, bias_ref, *refs, rider=False):
    if rider:
        _ride(refs[:N_RIDER_IN], refs[N_RIDER_IN + 1])
        refs = refs[N_RIDER_IN:N_RIDER_IN + 1] + refs[N_RIDER_IN + 2:]
    o_ref, qs_ref, kc_ref, vc_ref = refs
    scale = HD_B ** -0.5
    seq = q_ref.shape[0]
    ntile = seq // BLK
    p = pl.program_id(1)
    kv = p // 2
    sinks = (sink_ref[2 * p], sink_ref[2 * p + 1])
    low = _low_lanes(BLK)
    own_half = (lax.broadcasted_iota(jnp.int32, (BLK, 128), 1) >= 64).astype(jnp.int32) == kv

    def dup(xf):
        return jnp.where(own_half, xf, pltpu.roll(xf, 64, 1))

    zero = jnp.zeros((PAD, 128), BF16)
    for hh in range(2):
        kc_ref[hh, 0:PAD, :] = zero
        vc_ref[hh, 0:PAD, :] = zero

    def stage(t, c):
        src = pl.ds(pl.multiple_of(t * BLK, BLK), BLK)
        dstp = pl.ds(pl.multiple_of(PAD + t * BLK, BLK), BLK)
        qs_ref[src, :] = (q_ref[src, :] * scale).astype(BF16)
        k = dup(k_ref[src, :])
        v = dup(v_ref[src, :])
        kc_ref[0, dstp, :] = jnp.where(low, k, 0.0).astype(BF16)
        kc_ref[1, dstp, :] = jnp.where(low, 0.0, k).astype(BF16)
        vc_ref[0, dstp, :] = jnp.where(low, v, 0.0).astype(BF16)
        vc_ref[1, dstp, :] = jnp.where(low, 0.0, v).astype(BF16)
        return c

    lax.fori_loop(0, ntile, stage, 0, unroll=4)

    def body(i, c):
        res = []
        for j in range(UNROLL):
            t = i * UNROLL + j
            qrows = pl.ds(pl.multiple_of(t * BLK, BLK), BLK)
            krows = pl.ds(pl.multiple_of(t * BLK, BLK), 2 * BLK)
            bias = [bias_ref[hh] for hh in range(2)]
            if j == 0:
                bias = [_first_bias(b, i == 0) for b in bias]
            u, m, l = _pair_tile(qs_ref[qrows, :], kc_ref[0, krows, :], kc_ref[1, krows, :],
                                 vc_ref[0, krows, :], vc_ref[1, krows, :], bias[0], bias[1], sinks)
            sk = jnp.where(low, sinks[0], sinks[1])
            res.append(u / (l + jnp.exp(sk - m)))
        for j in range(UNROLL):
            o_ref[pl.ds(pl.multiple_of((i * UNROLL + j) * BLK, BLK), BLK), :] = res[j]
        return c

    lax.fori_loop(0, ntile // UNROLL, body, 0)


def _swa_call(sinks, qb, kb, vb, bias_b, batch, seq, rider=None, per_step=1):
    qblk = pl.BlockSpec((None, seq, 128), lambda b, p: (b, 0, p))
    kblk = pl.BlockSpec((None, seq, 128), lambda b, p: (b, 0, 0))
    pairs = H_B // 2
    in_specs = [pl.BlockSpec(memory_space=pltpu.SMEM), qblk, kblk, kblk,
                pl.BlockSpec((2, BLK, 2 * BLK), lambda b, p: (p, 0, 0))]
    args = [sinks, qb.reshape(batch, seq, 512), kb.reshape(batch, seq, 128), vb.reshape(batch, seq, 128), bias_b]
    out_specs = [qblk]
    out_shape = [jax.ShapeDtypeStruct((batch, seq, 512), F32)]
    if rider is not None:
        r_in, r_out = _rider_specs(rider[1], lambda b, p: b * pairs + p, per_step)
        in_specs += r_in
        args += list(rider[0])
        out_specs.append(r_out)
        out_shape.append(jax.ShapeDtypeStruct((batch * pairs * per_step, 1, 512), F32))
    return pl.pallas_call(
        functools.partial(_swa_kernel, rider=rider is not None),
        grid=(batch, pairs),
        in_specs=in_specs,
        out_specs=out_specs,
        out_shape=out_shape,
        scratch_shapes=[pltpu.VMEM((seq, 128), BF16), pltpu.VMEM((2, PAD + seq, 128), BF16),
                        pltpu.VMEM((2, PAD + seq, 128), BF16)],
        compiler_params=_cparams(("arbitrary", "arbitrary")),
        name="swa_attn",
    )(*args)


def _mem_attn_kernel(q_ref, mk_ref, mv_ref, o_ref):
    scale = HD_C ** -0.5
    for h in range(H_C):
        cs = slice(h * HD_C, (h + 1) * HD_C)
        s = _dot_nt(q_ref[:, cs].astype(BF16), mk_ref[:, cs].astype(BF16)) * scale
        m = jnp.max(s, axis=-1, keepdims=True)
        p = jnp.exp(s - m)
        l = jnp.sum(p, axis=-1, keepdims=True)
        o_ref[:, cs] = _dot(p.astype(BF16), mv_ref[:, cs].astype(BF16)) / l


def _mem_attn_call(qc, mk, mv, batch, seq):
    tq = 512
    return pl.pallas_call(
        _mem_attn_kernel,
        grid=(batch, seq // tq),
        in_specs=[pl.BlockSpec((None, tq, 512), lambda b, i: (b, i, 0)),
                  pl.BlockSpec((None, N_MEM, 512), lambda b, i: (b, 0, 0)),
                  pl.BlockSpec((None, N_MEM, 512), lambda b, i: (b, 0, 0))],
        out_specs=pl.BlockSpec((None, tq, 512), lambda b, i: (b, i, 0)),
        out_shape=jax.ShapeDtypeStruct((batch, seq, 512), F32),
        compiler_params=_cparams(("arbitrary", "arbitrary")),
        name="mem_attn",
    )(qc.reshape(batch, seq, 512), mk.reshape(batch, N_MEM, 512), mv.reshape(batch, N_MEM, 512))


def _head_rows(row, width, head_dim):
    rid = lax.broadcasted_iota(jnp.int32, (8, width), 0)
    hid = lax.broadcasted_iota(jnp.int32, (8, width), 1) // head_dim
    return rid == hid, jnp.where(rid == hid, jnp.broadcast_to(row, (8, width)), 0.0)


def _samp_a_one(q, kn, vn, kt_ref, vt_ref, bias_ref, mult_ref, b0_ref):
    scale = HD_A ** -0.5
    nt = W_A // 128
    own, qh = _head_rows(q, 512, HD_A)
    qcols = jnp.transpose(jnp.broadcast_to(q, (128, 512)))
    s_heads = []
    for h in range(H_A):
        rows = slice(h * HD_A, (h + 1) * HD_A)
        qb = qcols[rows, :]
        s_heads.append(jnp.concatenate(
            [jnp.sum(kt_ref[rows, 128 * j:128 * (j + 1)] * qb, axis=0, keepdims=True) for j in range(nt)], axis=1))
    s = jnp.concatenate(s_heads, axis=0) * scale + bias_ref[...]
    ln = jnp.sum(qh * kn, axis=-1, keepdims=True) * scale + b0_ref[:, 0:1]
    m = jnp.maximum(jnp.max(s, axis=-1, keepdims=True), ln)
    p = jnp.exp(s - m) * mult_ref[...]
    pn = len(A_DILATIONS) * jnp.exp(ln - m)
    den = jnp.sum(p, axis=-1, keepdims=True) + pn
    accs = []
    for h in range(H_A):
        rows = slice(h * HD_A, (h + 1) * HD_A)
        acc = vt_ref[rows, 0:128] * p[h:h + 1, 0:128]
        for j in range(1, nt):
            acc = acc + vt_ref[rows, 128 * j:128 * (j + 1)] * p[h:h + 1, 128 * j:128 * (j + 1)]
        accs.append(acc)
    pv = jnp.sum(jnp.transpose(jnp.concatenate(accs, axis=0)), axis=0, keepdims=True)
    lanes = lambda col: jnp.sum(jnp.where(own, col, 0.0), axis=0, keepdims=True)
    return (pv + lanes(pn) * vn) / lanes(den)


N_RIDER_IN = 8


def _rider_arrays(qa, kn, vn, cache_k, cache_v, bias_sa, mult, s0a):
    nb = qa.shape[0]
    kt = jnp.transpose(cache_k, (0, 2, 3, 1)).reshape(nb, 512, W_A)
    vt = jnp.transpose(cache_v, (0, 2, 3, 1)).reshape(nb, 512, W_A)
    return (qa.reshape(nb, 1, 512), kn.reshape(nb, 1, 512), vn.reshape(nb, 1, 512), kt, vt, bias_sa, mult, s0a)


def _rider_specs(first, step_of, per_step=1):
    assert first % per_step == 0
    blk = lambda *tail: pl.BlockSpec((per_step,) + tail,
                                     lambda *ids: (first // per_step + step_of(*ids),) + (0,) * len(tail))
    ins = [blk(1, 512), blk(1, 512), blk(1, 512), blk(512, W_A), blk(512, W_A),
           _const_spec((8, W_A)), _const_spec((1, W_A)), _const_spec((8, 128))]
    return ins, pl.BlockSpec((per_step, 1, 512), lambda *ids: (step_of(*ids), 0, 0))


def _ride(in_refs, o_ref):
    q_ref, kn_ref, vn_ref, kt_ref, vt_ref, bias_ref, mult_ref, b0_ref = in_refs
    for j in range(q_ref.shape[0]):
        o_ref[j] = _samp_a_one(q_ref[j], kn_ref[j], vn_ref[j], kt_ref.at[j], vt_ref.at[j],
                               bias_ref, mult_ref, b0_ref)


def _samp_a_kernel(*refs):
    _ride(refs[:N_RIDER_IN], refs[-1])


def _samp_a_call(arrays, first, count):
    r_in, r_out = _rider_specs(first, lambda i: i)
    return pl.pallas_call(
        _samp_a_kernel,
        grid=(count,),
        in_specs=r_in,
        out_specs=r_out,
        out_shape=jax.ShapeDtypeStruct((count, 1, 512), F32),
        compiler_params=_cparams(("arbitrary",)),
        name="sample_dilated_attn",
    )(*arrays)


def _samp_b_kernel(q_ref, kn_ref, vn_ref, kt_ref, vt_ref, bias_ref, b0_ref, sk_ref, o_ref):
    scale = HD_B ** -0.5
    low = lax.broadcasted_iota(jnp.int32, (1, 128), 1) < 64
    rid = lax.broadcasted_iota(jnp.int32, (8, 128), 0)
    kv_half = (lax.broadcasted_iota(jnp.int32, (8, 128), 1) >= 64) == (rid >= 4)

    bb = q_ref.shape[0]
    zero = jnp.zeros((8, 128), F32)
    qhs = []
    for b in range(bb):
        q = q_ref[b:b + 1, :]
        rows = []
        for h in range(H_B):
            chunk = q[:, 128 * (h // 2):128 * (h // 2) + 128]
            if (h % 2) != (h // 4):
                chunk = pltpu.roll(chunk, 64, 1)
            rows.append(chunk)
        qhs.append(jnp.where(kv_half, jnp.concatenate(rows, axis=0), 0.0))
    qblk = jnp.concatenate([jnp.concatenate([qhs[b] if c == b else zero for c in range(bb)], axis=1)
                            for b in range(bb)], axis=0)
    kt = kt_ref[...].reshape(bb * 128, W_B).astype(BF16)
    vt = vt_ref[...].reshape(bb * 128, W_B).astype(BF16)
    tile8 = lambda ref: jnp.concatenate([ref[...]] * bb, axis=0)
    s = _dot(qblk.astype(BF16), kt) * scale + tile8(bias_ref)
    knew = jnp.concatenate([jnp.broadcast_to(kn_ref[b:b + 1, :], (8, 128)) for b in range(bb)], axis=0)
    vnew = jnp.concatenate([jnp.broadcast_to(vn_ref[b:b + 1, :], (8, 128)) for b in range(bb)], axis=0)
    qall = jnp.concatenate(qhs, axis=0)
    ln = jnp.sum(qall * knew, axis=-1, keepdims=True) * scale + tile8(b0_ref)[:, 0:1]
    sk = tile8(sk_ref)[:, 0:1]
    m = jnp.maximum(jnp.maximum(jnp.max(s, axis=-1, keepdims=True), ln), sk)
    p = jnp.exp(s - m)
    pn = jnp.exp(ln - m)
    den = jnp.sum(p, axis=-1, keepdims=True) + pn + jnp.exp(sk - m)
    rall = _dot_nt(p.astype(BF16), vt)
    for b in range(bb):
        r = (rall[8 * b:8 * b + 8, 128 * b:128 * b + 128] + pn[8 * b:8 * b + 8] * vnew[8 * b:8 * b + 8])
        r = jnp.where(kv_half, r / den[8 * b:8 * b + 8], 0.0)
        chunks = []
        for c2 in range(4):
            pair = []
            for h in (2 * c2, 2 * c2 + 1):
                piece = r[h:h + 1, :]
                if (h % 2) != (h // 4):
                    piece = pltpu.roll(piece, 64, 1)
                pair.append(piece)
            chunks.append(jnp.where(low, pair[0], pair[1]))
        o_ref[b:b + 1, :] = jnp.concatenate(chunks, axis=1)


def _samp_b_call(qb, kn, vn, cache_k, cache_v, bias_sb, s0b, skb):
    nb = qb.shape[0]
    bb = 8
    kt = jnp.transpose(cache_k, (0, 2, 3, 1)).reshape(nb, 128, W_B)
    vt = jnp.transpose(cache_v, (0, 2, 3, 1)).reshape(nb, 128, W_B)
    row = lambda w: pl.BlockSpec((bb, w), lambda i: (i, 0))
    cblk = pl.BlockSpec((bb, 128, W_B), lambda i: (i, 0, 0))
    return pl.pallas_call(
        _samp_b_kernel,
        grid=(nb // bb,),
        in_specs=[row(512), row(128), row(128), cblk, cblk, _const_spec((8, W_B)),
                  _const_spec((8, 128)), _const_spec((8, 128))],
        out_specs=row(512),
        out_shape=jax.ShapeDtypeStruct((nb, 512), F32),
        compiler_params=_cparams(("arbitrary",)),
        name="sample_swa_attn",
    )(qb, kn, vn, kt, vt, bias_sb, s0b, skb)


def _samp_c_kernel(q_ref, mk_ref, mv_ref, mask_ref, o_ref):
    scale = HD_C ** -0.5
    zeros = jnp.zeros((4, HD_C), F32)

    def body(b, c):
        q8 = jnp.concatenate([q_ref[b], zeros], axis=0).astype(BF16)
        s = _dot_nt(q8, mk_ref[b].astype(BF16)) * scale + mask_ref[...]
        m = jnp.max(s, axis=-1, keepdims=True)
        p = jnp.exp(s - m)
        den = jnp.sum(p, axis=-1, keepdims=True)
        r = _dot(p.astype(BF16), mv_ref[b].astype(BF16)) / den
        o_ref[b] = r[0:4, :]
        return c

    lax.fori_loop(0, q_ref.shape[0], body, 0, unroll=True)


def _samp_c_call(qc, cache_k, cache_v):
    nb = qc.shape[0]
    bb = 8
    rows = N_MEM * H_C
    head_of_row = np.arange(rows)[None, :] % H_C
    mask = np.where(head_of_row == np.arange(8)[:, None], 0.0, NEG).astype(np.float32)
    qblk = pl.BlockSpec((bb, H_C, HD_C), lambda i: (i, 0, 0))
    cblk = pl.BlockSpec((bb, rows, HD_C), lambda i: (i, 0, 0))
    out = pl.pallas_call(
        _samp_c_kernel,
        grid=(nb // bb,),
        in_specs=[qblk, cblk, cblk, _const_spec((8, rows))],
        out_specs=qblk,
        out_shape=jax.ShapeDtypeStruct((nb, H_C, HD_C), F32),
        compiler_params=_cparams(("arbitrary",)),
        name="sample_mem_attn",
    )(qc.reshape(nb, H_C, HD_C), cache_k.reshape(nb, rows, HD_C), cache_v.reshape(nb, rows, HD_C),
      jnp.asarray(mask))
    return out.reshape(nb, 512)


def kernel(x_prompt, x_sample, mem_prompt, cache_a_k, cache_a_v, cache_b_k, cache_b_v, cache_mem_k,
           cache_mem_v, rel_bias, ln_g, w_in, gq_a, gk_a, gq_b, gk_b, gq_c, gk_c, sinks_b, mem_ln_g,
           w_mem_kv, w_br_a, w_br_b, w_br_c, w_out):
    batch, seq, _ = x_prompt.shape
    nsamp = x_sample.shape[0]
    assert ln_g.shape[0] == 1 and x_sample.shape[1] == 1
    assert (batch, seq, nsamp) == (8, 2048, 128) and w_in.shape == (1, D_MODEL, C_END)
    assert cache_a_k.shape == (1, nsamp, W_A, H_A, HD_A) and cache_b_k.shape == (1, nsamp, W_B, KV_B, HD_B)

    w_bf = w_in[0].astype(BF16)
    wmem_bf = w_mem_kv[0].astype(BF16)
    wbr_bf = jnp.stack([w_br_a[0], w_br_b[0], w_br_c[0]]).astype(BF16)
    wout_bf = w_out[0].astype(BF16)
    lng = ln_g.reshape(1, D_MODEL)
    gains = jnp.stack([jnp.tile(gq_a[0], 8), jnp.tile(gk_a[0], 8), jnp.tile(gq_b[0], 8),
                       jnp.tile(gk_b[0], 8), jnp.tile(gq_c[0], 4)])
    gk_c4 = jnp.tile(gk_c[0], 4).reshape(1, 512)
    blockdiag = np.kron(np.eye(4, dtype=np.float32), np.ones((64, 64), np.float32))
    pmat = jnp.asarray(blockdiag, BF16)

    *tables, mult_a = _bucket_tables()
    bias_a, bias_b, bias_sa, bias_sb, s0a, s0b, skb = _bias_call(rel_bias, sinks_b.reshape(H_B), tables)

    xs2 = x_sample.reshape(nsamp, D_MODEL)
    sqa, ska, sva, sqb, skb_new, svb, sqc = _qkv_call(xs2, lng, w_bf, gains, pmat)
    riders = _rider_arrays(sqa, ska, sva, cache_a_k[0], cache_a_v[0], bias_sa, jnp.asarray(mult_a), s0a)
    n_qkv = batch * seq // 512
    n_dil = batch * (H_A // 2)
    swa_per_step = 2
    n_swa = batch * (H_B // 2) * swa_per_step
    n_alone = nsamp - n_qkv - n_dil - n_swa
    assert n_alone >= 0

    xp2 = x_prompt.reshape(batch * seq, D_MODEL)
    qa, ka, va, qb, kb, vb, qc, ka_t, va_t, kb_t, vb_t, soa_1 = _qkv_call(
        xp2, lng, w_bf, gains, pmat, (batch, seq), rider=(riders, 0))
    mk, mv = _memkv_call(mem_prompt.reshape(batch * N_MEM, D_MODEL), mem_ln_g.reshape(1, D_MODEL),
                         wmem_bf, gk_c4)
    oa, soa_2 = _dil_call(qa, ka, va, bias_a, batch, seq, rider=(riders, n_qkv))
    ob, soa_3 = _swa_call(sinks_b.reshape(H_B), qb, kb, vb, bias_b, batch, seq,
                          rider=(riders, n_qkv + n_dil), per_step=swa_per_step)
    oc = _mem_attn_call(qc, mk, mv, batch, seq)
    yp, = _out_call(xp2, oa.reshape(-1, 512), ob.reshape(-1, 512), oc.reshape(-1, 512),
                    lng, w_bf, wbr_bf, wout_bf)
    yp = yp.reshape(batch, seq, D_MODEL)

    soa_parts = [soa_1, soa_2, soa_3]
    if n_alone:
        soa_parts.append(_samp_a_call(riders, n_qkv + n_dil + n_swa, n_alone))
    soa = jnp.concatenate(soa_parts, axis=0).reshape(nsamp, 512)
    sob = _samp_b_call(sqb, skb_new, svb, cache_b_k[0], cache_b_v[0], bias_sb, s0b, skb)
    soc = _samp_c_call(sqc, cache_mem_k[0], cache_mem_v[0])
    ys, = _out_call(xs2, soa, sob, soc, lng, w_bf, wbr_bf, wout_bf)

    return (yp, ys.reshape(nsamp, 1, D_MODEL),
            jnp.transpose(ka_t.reshape(1, batch, H_A, HD_A, seq), (0, 1, 4, 2, 3)),
            jnp.transpose(va_t.reshape(1, batch, H_A, HD_A, seq), (0, 1, 4, 2, 3)),
            jnp.transpose(kb_t.reshape(1, batch, KV_B, HD_B, W_B), (0, 1, 4, 2, 3)),
            jnp.transpose(vb_t.reshape(1, batch, KV_B, HD_B, W_B), (0, 1, 4, 2, 3)),
            mk.reshape(1, batch, N_MEM, H_C, HD_C), mv.reshape(1, batch, N_MEM, H_C, HD_C),
            ska.reshape(1, nsamp, 1, H_A, HD_A), sva.reshape(1, nsamp, 1, H_A, HD_A),
            skb_new.reshape(1, nsamp, 1, KV_B, HD_B), svb.reshape(1, nsamp, 1, KV_B, HD_B))
```

```python
import functools
import math

import numpy as np
import jax
import jax.numpy as jnp
from jax import lax
from jax.experimental import pallas as pl
from jax.experimental.pallas import tpu as pltpu

F32 = jnp.float32
BF16 = jnp.bfloat16

D_MODEL = 1024
H_A, HD_A = 8, 64
A_DILATIONS = (1, 4, 16)
A_STEPS = 128
W_A = 2048
H_B, KV_B, HD_B, W_B = 8, 2, 64, 128
H_C, HD_C, N_MEM = 4, 128, 256
NUM_BUCKETS, MAX_DISTANCE = 32, 2048
EPS = 1e-6
NEG = -1e30
BLK = 128
UNROLL = 16

C_QA, C_KA, C_VA, C_ZA = 0, 512, 1024, 1536
C_QB, C_KB, C_VB, C_ZB = 2048, 2560, 2688, 2816
C_QC, C_ZC = 3328, 3840
C_GA, C_GB, C_GC, C_END = 4352, 5376, 6400, 7424

VMEM_LIMIT = 56 * 1024 * 1024


def _cparams(sem):
    return pltpu.CompilerParams(dimension_semantics=sem, vmem_limit_bytes=VMEM_LIMIT)


def _const_spec(shape):
    nd = len(shape)
    return pl.BlockSpec(shape, lambda *_: (0,) * nd, pipeline_mode=pl.Buffered(1))


def _t5_bucket_np(dist):
    n = np.maximum(dist, 0)
    max_exact = NUM_BUCKETS // 2
    nf = np.maximum(n, 1).astype(np.float32)
    large = max_exact + (np.log(nf / np.float32(max_exact))
                         / np.float32(math.log(MAX_DISTANCE / max_exact))
                         * np.float32(NUM_BUCKETS - max_exact)).astype(np.int32)
    return np.where(n < max_exact, n, np.minimum(large, NUM_BUCKETS - 1)).astype(np.int32)


def _a_multiplicity(dist):
    return sum(((dist % d == 0) & (dist <= A_STEPS * d)).astype(np.int32) for d in A_DILATIONS)


def _bucket_tables():
    i = np.arange(BLK)[:, None]
    j = np.arange(2 * BLK)[None, :]
    rel = i + BLK - j
    bkt_a = np.stack([np.where((rel >= 0) & (rel <= A_STEPS), _t5_bucket_np(rel * d), -1)
                      for d in A_DILATIONS]).astype(np.int32)
    bkt_b = np.where((rel >= 0) & (rel < W_B), _t5_bucket_np(rel), -1).astype(np.int32)
    dist_a = W_A - np.arange(W_A)
    mult_a = _a_multiplicity(dist_a)
    sb_a = np.broadcast_to(np.where(mult_a > 0, _t5_bucket_np(dist_a), -1), (8, W_A)).astype(np.int32)
    dist_b = W_B - np.arange(W_B)
    sb_b = np.broadcast_to(np.where(dist_b < W_B, _t5_bucket_np(dist_b), -1), (8, W_B)).astype(np.int32)
    return bkt_a, bkt_b, sb_a, sb_b, mult_a.astype(np.float32).reshape(1, W_A)


def _rms_rows(x, g):
    return x * lax.rsqrt(jnp.mean(x * x, axis=-1, keepdims=True) + EPS) * g


def _dot(a, b):
    return jnp.dot(a, b, preferred_element_type=F32)


def _dot_nt(a, b):
    return lax.dot_general(a, b, (((1,), (1,)), ((), ())), preferred_element_type=F32)


def _headnorm64(t, g, pmat):
    sq = (t * t).astype(BF16)
    width = t.shape[1]
    step = min(width, 256)
    pp = pmat[:step, :step]
    parts = [_dot(sq[:, c:c + step], pp) for c in range(0, width, step)]
    ss = parts[0] if len(parts) == 1 else jnp.concatenate(parts, axis=1)
    return t * lax.rsqrt(ss * (1.0 / 64.0) + EPS) * g


def _headnorm128(t, g):
    parts = []
    for c in range(0, t.shape[1], 128):
        seg = t[:, c:c + 128]
        parts.append(seg * lax.rsqrt(jnp.mean(seg * seg, axis=-1, keepdims=True) + EPS))
    return jnp.concatenate(parts, axis=1) * g


def _bias_kernel(tbl_ref, sink_ref, bkta_ref, bktb_ref, sba_ref, sbb_ref,
                 ba_ref, bb_ref, oa_ref, ob_ref, s0a_ref, s0b_ref, sk_ref, *, present):
    pa, pb, psa, psb = present
    for g in range(3):
        bk = bkta_ref[g]
        for h in range(H_A):
            ba_ref[g, h] = jnp.full((BLK, 2 * BLK), NEG, F32)
        for b in pa[g]:
            hit = bk == b
            for h in range(H_A):
                ba_ref[g, h] = jnp.where(hit, tbl_ref[b, h], ba_ref[g, h])
    bk = bktb_ref[...]
    for h in range(H_B):
        bb_ref[h] = jnp.full((BLK, 2 * BLK), NEG, F32)
    for b in pb:
        hit = bk == b
        for h in range(H_B):
            bb_ref[h] = jnp.where(hit, tbl_ref[b, H_A + h], bb_ref[h])
    for (src, dst, buckets, col0) in ((sba_ref, oa_ref, psa, 0), (sbb_ref, ob_ref, psb, H_A)):
        bk = src[...]
        row = lax.broadcasted_iota(jnp.int32, bk.shape, 0)
        acc = jnp.full(bk.shape, NEG, F32)
        for b in buckets:
            hit = bk == b
            for h in range(8):
                acc = jnp.where(hit & (row == h), tbl_ref[b, col0 + h], acc)
        dst[...] = acc
    row8 = lax.broadcasted_iota(jnp.int32, (8, 128), 0)
    a0 = jnp.zeros((8, 128), F32)
    b0 = jnp.zeros((8, 128), F32)
    sk = jnp.zeros((8, 128), F32)
    for h in range(8):
        a0 = jnp.where(row8 == h, tbl_ref[0, h], a0)
        b0 = jnp.where(row8 == h, tbl_ref[0, H_A + h], b0)
        sk = jnp.where(row8 == h, sink_ref[h], sk)
    s0a_ref[...] = a0
    s0b_ref[...] = b0
    sk_ref[...] = sk


def _bias_call(rel_bias, sinks, tables):
    bkt_a, bkt_b, sb_a, sb_b = tables
    uniq = lambda a: tuple(int(b) for b in np.unique(a) if b >= 0)
    present = (tuple(uniq(bkt_a[g]) for g in range(3)), uniq(bkt_b), uniq(sb_a), uniq(sb_b))
    smem = pl.BlockSpec(memory_space=pltpu.SMEM)
    vmem = pl.BlockSpec(memory_space=pltpu.VMEM)
    return pl.pallas_call(
        functools.partial(_bias_kernel, present=present),
        in_specs=[smem, smem, vmem, vmem, vmem, vmem],
        out_specs=[vmem] * 7,
        out_shape=[jax.ShapeDtypeStruct((3, H_A, BLK, 2 * BLK), F32),
                   jax.ShapeDtypeStruct((H_B, BLK, 2 * BLK), F32),
                   jax.ShapeDtypeStruct((8, W_A), F32),
                   jax.ShapeDtypeStruct((8, W_B), F32),
                   jax.ShapeDtypeStruct((8, 128), F32),
                   jax.ShapeDtypeStruct((8, 128), F32),
                   jax.ShapeDtypeStruct((8, 128), F32)],
        compiler_params=pltpu.CompilerParams(vmem_limit_bytes=VMEM_LIMIT),
        name="bias_expand",
    )(rel_bias, sinks, jnp.asarray(bkt_a), jnp.asarray(bkt_b), jnp.asarray(sb_a), jnp.asarray(sb_b))


def _qkv_kernel(x_ref, lng_ref, w_ref, g_ref, p_ref, *refs, rider=False, tiles_per_seq=1):
    if rider:
        _ride(refs[:N_RIDER_IN], refs[-1])
        refs = refs[N_RIDER_IN:-1]
    qa_ref, ka_ref, va_ref, qb_ref, kb_ref, vb_ref, qc_ref, *cache_refs = refs
    h = _rms_rows(x_ref[...], lng_ref[...]).astype(BF16)
    pmat = p_ref[...]

    def proj(c0, c1):
        return _dot(h, w_ref[:, c0:c1])

    qa_ref[...] = _headnorm64(proj(C_QA, C_KA), g_ref[0:1, :], pmat)
    ka = _headnorm64(proj(C_KA, C_VA), g_ref[1:2, :], pmat)
    va = proj(C_VA, C_ZA)
    ka_ref[...] = ka
    va_ref[...] = va
    kb = _headnorm64(proj(C_KB, C_VB), g_ref[3:4, 0:128], pmat)
    vb = proj(C_VB, C_ZB)
    if cache_refs:
        kt_ref, vt_ref, kbt_ref, vbt_ref = cache_refs
        kt_ref[...] = ka.T
        vt_ref[...] = va.T

        @pl.when((pl.program_id(0) + 1) % tiles_per_seq == 0)
        def _():
            tm = kb.shape[0]
            kbt_ref[...] = kb[tm - W_B:, :].T
            vbt_ref[...] = vb[tm - W_B:, :].T
    qb_ref[...] = _headnorm64(proj(C_QB, C_KB), g_ref[2:3, :], pmat).astype(qb_ref.dtype)
    kb_ref[...] = kb
    vb_ref[...] = vb
    qc_ref[...] = _headnorm128(proj(C_QC, C_ZC), g_ref[4:5, :]).astype(qc_ref.dtype)


def _qkv_call(x2, lng, w_bf, gains, pmat, cache_layout_for=None, rider=None):
    n = x2.shape[0]
    tm = min(512, n)
    steps = n // tm
    row = lambda w: pl.BlockSpec((tm, w), lambda i: (i, 0))
    widths = (512, 512, 512, 512, 128, 128, 512)
    in_specs = [row(D_MODEL), _const_spec((1, D_MODEL)), _const_spec((D_MODEL, C_END)),
                _const_spec((5, 512)), _const_spec((256, 256))]
    args = [x2, lng, w_bf, gains, pmat]
    out_specs = [row(w) for w in widths]
    q_dtype = F32 if cache_layout_for is None else BF16
    out_shape = [jax.ShapeDtypeStruct((n, w), q_dtype if i in (3, 6) else F32) for i, w in enumerate(widths)]
    per = 1
    if cache_layout_for is not None:
        batch, seq = cache_layout_for
        per = seq // tm
        out_specs += [pl.BlockSpec((None, 512, tm), lambda i: (i // per, 0, i % per))] * 2
        out_shape += [jax.ShapeDtypeStruct((batch, 512, seq), F32)] * 2
        out_specs += [pl.BlockSpec((None, KV_B * HD_B, W_B), lambda i: (i // per, 0, 0))] * 2
        out_shape += [jax.ShapeDtypeStruct((batch, KV_B * HD_B, W_B), F32)] * 2
    if rider is not None:
        r_in, r_out = _rider_specs(rider[1], lambda i: i)
        in_specs += r_in
        args += list(rider[0])
        out_specs.append(r_out)
        out_shape.append(jax.ShapeDtypeStruct((steps, 1, 512), F32))
    return pl.pallas_call(
        functools.partial(_qkv_kernel, rider=rider is not None, tiles_per_seq=per),
        grid=(steps,),
        in_specs=in_specs,
        out_specs=out_specs,
        out_shape=out_shape,
        compiler_params=_cparams(("arbitrary",)),
        name="qkv_proj",
    )(*args)


def _memkv_kernel(m_ref, g_ref, w_ref, gk_ref, mk_ref, mv_ref):
    h = _rms_rows(m_ref[...], g_ref[...]).astype(BF16)
    mk_ref[...] = _headnorm128(_dot(h, w_ref[:, 0:512]), gk_ref[...])
    mv_ref[...] = _dot(h, w_ref[:, 512:1024])


def _memkv_call(mem2, g, w_bf, gk):
    n = mem2.shape[0]
    tm = 512
    row = lambda w: pl.BlockSpec((tm, w), lambda i: (i, 0))
    return pl.pallas_call(
        _memkv_kernel,
        grid=(n // tm,),
        in_specs=[row(D_MODEL), _const_spec((1, D_MODEL)), _const_spec((D_MODEL, 2 * 512)),
                  _const_spec((1, 512))],
        out_specs=[row(512), row(512)],
        out_shape=[jax.ShapeDtypeStruct((n, 512), F32)] * 2,
        compiler_params=_cparams(("arbitrary",)),
        name="mem_kv_proj",
    )(mem2, g, w_bf, gk)


def _out_kernel(x_ref, oa_ref, ob_ref, oc_ref, lng_ref, w_ref, wbr_ref, wout_ref, *refs):
    if len(refs) > 1:
        _ride(refs[:N_RIDER_IN], refs[-1])
    y_ref = refs[-2] if len(refs) > 1 else refs[0]
    x = x_ref[...]
    h = _rms_rows(x, lng_ref[...]).astype(BF16)
    acc = None
    for br, (o_ref, cz, cg) in enumerate(((oa_ref, C_ZA, C_GA), (ob_ref, C_ZB, C_GB), (oc_ref, C_ZC, C_GC))):
        z = _dot(h, w_ref[:, cz:cz + 512])
        y = (o_ref[...] * (z * jax.nn.sigmoid(z))).astype(BF16)
        yb = _dot(y, wbr_ref[br])
        gate = jax.nn.sigmoid(_dot(h, w_ref[:, cg:cg + D_MODEL]))
        acc = gate * yb if acc is None else acc + gate * yb
    y_ref[...] = x + _dot(acc.astype(BF16), wout_ref[...])


def _out_call(x2, oa, ob, oc, lng, w_bf, wbr_bf, wout_bf, tm=512, rider=None):
    n = x2.shape[0]
    tm = min(tm, n)
    steps = n // tm
    row = lambda w: pl.BlockSpec((tm, w), lambda i: (i, 0))
    in_specs = [row(D_MODEL), row(512), row(512), row(512), _const_spec((1, D_MODEL)),
                _const_spec((D_MODEL, C_END)), _const_spec((3, 512, D_MODEL)), _const_spec((D_MODEL, D_MODEL))]
    args = [x2, oa, ob, oc, lng, w_bf, wbr_bf, wout_bf]
    out_specs = [row(D_MODEL)]
    out_shape = [jax.ShapeDtypeStruct((n, D_MODEL), F32)]
    if rider is not None:
        r_in, r_out = _rider_specs(rider[1], lambda i: i)
        in_specs += r_in
        args += list(rider[0])
        out_specs.append(r_out)
        out_shape.append(jax.ShapeDtypeStruct((steps, 1, 512), F32))
    return pl.pallas_call(
        _out_kernel,
        grid=(steps,),
        in_specs=in_specs,
        out_specs=out_specs,
        out_shape=out_shape,
        compiler_params=_cparams(("arbitrary",)),
        name="gated_out",
    )(*args)


PAD = BLK
CLS_PITCH = BLK + 8


def _low_lanes(rows):
    return lax.broadcasted_iota(jnp.int32, (rows, 128), 1) < 64


def _aligned(start):
    return start if isinstance(start, int) else pl.multiple_of(start, BLK)


def _pair_tile(q, k0, k1, v0, v1, b0, b1, sinks=None):
    nk = k0.shape[0]
    low = _low_lanes(BLK)
    s = _dot_nt(q, jnp.concatenate([k0, k1], axis=0)) + jnp.concatenate([b0, b1], axis=1)
    ms = [jnp.max(s[:, hh * nk:(hh + 1) * nk], axis=-1, keepdims=True) for hh in range(2)]
    if sinks is not None:
        ms = [jnp.maximum(ms[hh], sinks[hh]) for hh in range(2)]
    p = jnp.concatenate([jnp.exp(s[:, hh * nk:(hh + 1) * nk] - ms[hh]) for hh in range(2)], axis=1).astype(BF16)
    own = _low_lanes(nk)
    ones = [jnp.where(own, 1.0, 0.0).astype(BF16), jnp.where(own, 0.0, 1.0).astype(BF16)]
    vblk = jnp.concatenate([jnp.concatenate([v0, ones[0]], axis=1),
                            jnp.concatenate([v1, ones[1]], axis=1)], axis=0)
    up = _dot(p, vblk)
    return up[:, :128], jnp.where(low, ms[0], ms[1]), up[:, 128:]


def _first_bias(bias, also=None):
    gone = lax.broadcasted_iota(jnp.int32, bias.shape, 1) < BLK
    if also is not None:
        gone = gone & also
    return jnp.where(gone, NEG, bias)


def _dil_kernel(q_ref, k_ref, v_ref, bias_ref, *refs, rider=False):
    if rider:
        _ride(refs[:N_RIDER_IN], refs[N_RIDER_IN + 1])
        refs = refs[N_RIDER_IN:N_RIDER_IN + 1] + refs[N_RIDER_IN + 2:]
    o_ref, qs_ref, kc_ref, vc_ref, c4_ref, ru_ref, rm_ref, rl_ref = refs
    scale = HD_A ** -0.5
    seq = q_ref.shape[0]
    ntile = seq // BLK
    low = _low_lanes(BLK)
    d1, d2 = A_DILATIONS[1], A_DILATIONS[2]

    def src_rows(g, t):
        d = A_DILATIONS[g]
        if d == 1:
            return pl.ds(_aligned(t * BLK), BLK)
        nblk = ntile // d
        r, n = t >> (nblk.bit_length() - 1), t & (nblk - 1)
        return pl.ds(r + n * (d * BLK), BLK, stride=d)

    zero = jnp.zeros((PAD, 128), BF16)
    stagers = []
    for g in range(3):
        for hh in range(2):
            kc_ref[g, hh, 0:PAD, :] = zero
            vc_ref[g, hh, 0:PAD, :] = zero

        def stage(t, c, g=g):
            dst = pl.ds(_aligned(t * BLK), BLK)
            dstp = pl.ds(_aligned(PAD + t * BLK), BLK)
            if g < 2:
                src = src_rows(g, t)
                q, k, v = q_ref[src, :], k_ref[src, :], v_ref[src, :]
            else:
                src = pl.ds((t & (d1 - 1)) * (seq // d1) + (t >> (d1.bit_length() - 1)), BLK, stride=d2 // d1)
                q, k, v = c4_ref[0, src, :], c4_ref[1, src, :], c4_ref[2, src, :]
            if g == 1:
                c4_ref[0, dst, :], c4_ref[1, dst, :], c4_ref[2, dst, :] = q, k, v
            qs_ref[g, dst, :] = (q * scale).astype(BF16)
            kc_ref[g, 0, dstp, :] = jnp.where(low, k, 0.0).astype(BF16)
            kc_ref[g, 1, dstp, :] = jnp.where(low, 0.0, k).astype(BF16)
            vc_ref[g, 0, dstp, :] = jnp.where(low, v, 0.0).astype(BF16)
            vc_ref[g, 1, dstp, :] = jnp.where(low, 0.0, v).astype(BF16)
            return c

        stagers.append(stage)

    def tiles(g, i):
        nblk = ntile // A_DILATIONS[g]
        res = []
        for j in range(UNROLL):
            t = i * UNROLL + j
            qrows = pl.ds(_aligned(t * BLK), BLK)
            krows = pl.ds(_aligned(t * BLK), 2 * BLK)
            bias = [bias_ref[g, hh] for hh in range(2)]
            if nblk <= UNROLL and j % nblk == 0:
                bias = [_first_bias(b) for b in bias]
            elif nblk > UNROLL and j == 0:
                bias = [_first_bias(b, i == 0) for b in bias]
            res.append(_pair_tile(qs_ref[g, qrows, :], kc_ref[g, 0, krows, :], kc_ref[g, 1, krows, :],
                                  vc_ref[g, 0, krows, :], vc_ref[g, 1, krows, :], bias[0], bias[1]))
        return res

    assert UNROLL == ntile
    for g in range(3):
        for t in range(ntile):
            stagers[g](t, 0)
        res = tiles(g, 0)
        for t in range(ntile):
            rows = pl.ds(t * CLS_PITCH, BLK) if g == 2 else src_rows(g, t)
            ru_ref[g, rows, :], rm_ref[g, rows, :], rl_ref[g, rows, :] = res[t]

    def natural_rows(ref, i):
        parts = [ref[2, pl.ds((v % 2) * 8 * CLS_PITCH + i * (BLK // d2) + v // 2, 8, stride=CLS_PITCH), :]
                 for v in range(BLK // 8)]
        return jnp.concatenate(parts, axis=0)

    def combine(i, c):
        rows = pl.ds(_aligned(i * BLK), BLK)
        ms = [rm_ref[0, rows, :], rm_ref[1, rows, :], natural_rows(rm_ref, i)]
        mx = jnp.maximum(jnp.maximum(ms[0], ms[1]), ms[2])
        ws = [jnp.exp(m - mx) for m in ms]
        num = ws[0] * ru_ref[0, rows, :] + ws[1] * ru_ref[1, rows, :] + ws[2] * natural_rows(ru_ref, i)
        den = ws[0] * rl_ref[0, rows, :] + ws[1] * rl_ref[1, rows, :] + ws[2] * natural_rows(rl_ref, i)
        o_ref[rows, :] = num / den
        return c

    lax.fori_loop(0, ntile, combine, 0, unroll=2)


def _dil_call(qa, ka, va, bias_a, batch, seq, rider=None):
    assert seq == A_DILATIONS[2] * BLK and UNROLL % (seq // (A_DILATIONS[1] * BLK)) == 0
    blk = pl.BlockSpec((None, seq, 128), lambda b, p: (b, 0, p))
    pairs = H_A // 2
    in_specs = [blk, blk, blk, pl.BlockSpec((3, 2, BLK, 2 * BLK), lambda b, p: (0, p, 0, 0))]
    args = [qa.reshape(batch, seq, 512), ka.reshape(batch, seq, 512), va.reshape(batch, seq, 512), bias_a]
    out_specs = [blk]
    out_shape = [jax.ShapeDtypeStruct((batch, seq, 512), F32)]
    if rider is not None:
        r_in, r_out = _rider_specs(rider[1], lambda b, p: b * pairs + p)
        in_specs += r_in
        args += list(rider[0])
        out_specs.append(r_out)
        out_shape.append(jax.ShapeDtypeStruct((batch * pairs, 1, 512), F32))
    return pl.pallas_call(
        functools.partial(_dil_kernel, rider=rider is not None),
        grid=(batch, pairs),
        in_specs=in_specs,
        out_specs=out_specs,
        out_shape=out_shape,
        scratch_shapes=[pltpu.VMEM((3, seq, 128), BF16), pltpu.VMEM((3, 2, PAD + seq, 128), BF16),
                        pltpu.VMEM((3, 2, PAD + seq, 128), BF16), pltpu.VMEM((3, seq, 128), F32)]
                       + [pltpu.VMEM((3, A_DILATIONS[2] * CLS_PITCH, 128), F32)] * 3,
        compiler_params=_cparams(("arbitrary", "arbitrary")),
        name="dilated_attn",
    )(*args)


def _swa_kernel(sink_ref, q_ref, k_ref, v_ref, bias_ref, o_ref, qs_ref, kc_ref, vc_ref):
    scale = HD_B ** -0.5
    seq = q_ref.shape[0]
    ntile = seq // BLK
    p = pl.program_id(1)
    kv = p // 2
    sinks = (sink_ref[2 * p], sink_ref[2 * p + 1])
    low = _low_lanes(BLK)
    own_half = (lax.broadcasted_iota(jnp.int32, (BLK, 128), 1) >= 64).astype(jnp.int32) == kv

    def dup(xf):
        return jnp.where(own_half, xf, pltpu.roll(xf, 64, 1))

    zero = jnp.zeros((PAD, 128), BF16)
    for hh in range(2):
        kc_ref[hh, 0:PAD, :] = zero
        vc_ref[hh, 0:PAD, :] = zero

    def stage(t, c):
        src = pl.ds(pl.multiple_of(t * BLK, BLK), BLK)
        dstp = pl.ds(pl.multiple_of(PAD + t * BLK, BLK), BLK)
        qs_ref[src, :] = (q_ref[src, :] * scale).astype(BF16)
        k = dup(k_ref[src, :])
        v = dup(v_ref[src, :])
        kc_ref[0, dstp, :] = jnp.where(low, k, 0.0).astype(BF16)
        kc_ref[1, dstp, :] = jnp.where(low, 0.0, k).astype(BF16)
        vc_ref[0, dstp, :] = jnp.where(low, v, 0.0).astype(BF16)
        vc_ref[1, dstp, :] = jnp.where(low, 0.0, v).astype(BF16)
        return c

    lax.fori_loop(0, ntile, stage, 0, unroll=4)

    def body(i, c):
        res = []
        for j in range(UNROLL):
            t = i * UNROLL + j
            qrows = pl.ds(pl.multiple_of(t * BLK, BLK), BLK)
            krows = pl.ds(pl.multiple_of(t * BLK, BLK), 2 * BLK)
            bias = [bias_ref[hh] for hh in range(2)]
            if j == 0:
                bias = [_first_bias(b, i == 0) for b in bias]
            u, m, l = _pair_tile(qs_ref[qrows, :], kc_ref[0, krows, :], kc_ref[1, krows, :],
                                 vc_ref[0, krows, :], vc_ref[1, krows, :], bias[0], bias[1], sinks)
            sk = jnp.where(low, sinks[0], sinks[1])
            res.append(u / (l + jnp.exp(sk - m)))
        for j in range(UNROLL):
            o_ref[pl.ds(pl.multiple_of((i * UNROLL + j) * BLK, BLK), BLK), :] = res[j]
        return c

    lax.fori_loop(0, ntile // UNROLL, body, 0)


def _swa_call(sinks, qb, kb, vb, bias_b, batch, seq):
    qblk = pl.BlockSpec((None, seq, 128), lambda b, p: (b, 0, p))
    kblk = pl.BlockSpec((None, seq, 128), lambda b, p: (b, 0, 0))
    return pl.pallas_call(
        _swa_kernel,
        grid=(batch, H_B // 2),
        in_specs=[pl.BlockSpec(memory_space=pltpu.SMEM), qblk, kblk, kblk,
                  pl.BlockSpec((2, BLK, 2 * BLK), lambda b, p: (p, 0, 0))],
        out_specs=qblk,
        out_shape=jax.ShapeDtypeStruct((batch, seq, 512), F32),
        scratch_shapes=[pltpu.VMEM((seq, 128), BF16), pltpu.VMEM((2, PAD + seq, 128), BF16),
                        pltpu.VMEM((2, PAD + seq, 128), BF16)],
        compiler_params=_cparams(("arbitrary", "arbitrary")),
        name="swa_attn",
    )(sinks, qb.reshape(batch, seq, 512), kb.reshape(batch, seq, 128), vb.reshape(batch, seq, 128), bias_b)


def _mem_attn_kernel(q_ref, mk_ref, mv_ref, o_ref):
    scale = HD_C ** -0.5
    for h in range(H_C):
        cs = slice(h * HD_C, (h + 1) * HD_C)
        s = _dot_nt(q_ref[:, cs].astype(BF16), mk_ref[:, cs].astype(BF16)) * scale
        m = jnp.max(s, axis=-1, keepdims=True)
        p = jnp.exp(s - m)
        l = jnp.sum(p, axis=-1, keepdims=True)
        o_ref[:, cs] = _dot(p.astype(BF16), mv_ref[:, cs].astype(BF16)) / l


def _mem_attn_call(qc, mk, mv, batch, seq):
    tq = 512
    return pl.pallas_call(
        _mem_attn_kernel,
        grid=(batch, seq // tq),
        in_specs=[pl.BlockSpec((None, tq, 512), lambda b, i: (b, i, 0)),
                  pl.BlockSpec((None, N_MEM, 512), lambda b, i: (b, 0, 0)),
                  pl.BlockSpec((None, N_MEM, 512), lambda b, i: (b, 0, 0))],
        out_specs=pl.BlockSpec((None, tq, 512), lambda b, i: (b, i, 0)),
        out_shape=jax.ShapeDtypeStruct((batch, seq, 512), F32),
        compiler_params=_cparams(("arbitrary", "arbitrary")),
        name="mem_attn",
    )(qc.reshape(batch, seq, 512), mk.reshape(batch, N_MEM, 512), mv.reshape(batch, N_MEM, 512))


def _head_rows(row, width, head_dim):
    rid = lax.broadcasted_iota(jnp.int32, (8, width), 0)
    hid = lax.broadcasted_iota(jnp.int32, (8, width), 1) // head_dim
    return rid == hid, jnp.where(rid == hid, jnp.broadcast_to(row, (8, width)), 0.0)


def _samp_a_one(q, kn, vn, kt_ref, vt_ref, bias_ref, mult_ref, b0_ref):
    scale = HD_A ** -0.5
    nt = W_A // 128
    own, qh = _head_rows(q, 512, HD_A)
    qcols = jnp.transpose(jnp.broadcast_to(q, (128, 512)))
    s_heads = []
    for h in range(H_A):
        rows = slice(h * HD_A, (h + 1) * HD_A)
        qb = qcols[rows, :]
        s_heads.append(jnp.concatenate(
            [jnp.sum(kt_ref[rows, 128 * j:128 * (j + 1)] * qb, axis=0, keepdims=True) for j in range(nt)], axis=1))
    s = jnp.concatenate(s_heads, axis=0) * scale + bias_ref[...]
    ln = jnp.sum(qh * kn, axis=-1, keepdims=True) * scale + b0_ref[:, 0:1]
    m = jnp.maximum(jnp.max(s, axis=-1, keepdims=True), ln)
    p = jnp.exp(s - m) * mult_ref[...]
    pn = len(A_DILATIONS) * jnp.exp(ln - m)
    den = jnp.sum(p, axis=-1, keepdims=True) + pn
    accs = []
    for h in range(H_A):
        rows = slice(h * HD_A, (h + 1) * HD_A)
        acc = vt_ref[rows, 0:128] * p[h:h + 1, 0:128]
        for j in range(1, nt):
            acc = acc + vt_ref[rows, 128 * j:128 * (j + 1)] * p[h:h + 1, 128 * j:128 * (j + 1)]
        accs.append(acc)
    pv = jnp.sum(jnp.transpose(jnp.concatenate(accs, axis=0)), axis=0, keepdims=True)
    lanes = lambda col: jnp.sum(jnp.where(own, col, 0.0), axis=0, keepdims=True)
    return (pv + lanes(pn) * vn) / lanes(den)


N_RIDER_IN = 8


def _rider_arrays(qa, kn, vn, cache_k, cache_v, bias_sa, mult, s0a):
    nb = qa.shape[0]
    kt = jnp.transpose(cache_k, (0, 2, 3, 1)).reshape(nb, 512, W_A)
    vt = jnp.transpose(cache_v, (0, 2, 3, 1)).reshape(nb, 512, W_A)
    return (qa.reshape(nb, 1, 512), kn.reshape(nb, 1, 512), vn.reshape(nb, 1, 512), kt, vt, bias_sa, mult, s0a)


def _rider_specs(first, step_of):
    row = pl.BlockSpec((None, 1, 512), lambda *ids: (first + step_of(*ids), 0, 0))
    cblk = pl.BlockSpec((None, 512, W_A), lambda *ids: (first + step_of(*ids), 0, 0))
    ins = [row, row, row, cblk, cblk, _const_spec((8, W_A)), _const_spec((1, W_A)), _const_spec((8, 128))]
    return ins, pl.BlockSpec((None, 1, 512), lambda *ids: (step_of(*ids), 0, 0))


def _ride(in_refs, o_ref):
    q_ref, kn_ref, vn_ref, kt_ref, vt_ref, bias_ref, mult_ref, b0_ref = in_refs
    o_ref[...] = _samp_a_one(q_ref[...], kn_ref[...], vn_ref[...], kt_ref, vt_ref, bias_ref, mult_ref, b0_ref)


def _samp_a_kernel(*refs):
    _ride(refs[:N_RIDER_IN], refs[-1])


def _samp_a_call(arrays, first, count):
    r_in, r_out = _rider_specs(first, lambda i: i)
    return pl.pallas_call(
        _samp_a_kernel,
        grid=(count,),
        in_specs=r_in,
        out_specs=r_out,
        out_shape=jax.ShapeDtypeStruct((count, 1, 512), F32),
        compiler_params=_cparams(("arbitrary",)),
        name="sample_dilated_attn",
    )(*arrays)


def _samp_b_kernel(q_ref, kn_ref, vn_ref, kt_ref, vt_ref, bias_ref, b0_ref, sk_ref, o_ref):
    scale = HD_B ** -0.5
    low = lax.broadcasted_iota(jnp.int32, (1, 128), 1) < 64
    rid = lax.broadcasted_iota(jnp.int32, (8, 128), 0)
    kv_half = (lax.broadcasted_iota(jnp.int32, (8, 128), 1) >= 64) == (rid >= 4)

    bb = q_ref.shape[0]
    zero = jnp.zeros((8, 128), F32)
    qhs = []
    for b in range(bb):
        q = q_ref[b:b + 1, :]
        rows = []
        for h in range(H_B):
            chunk = q[:, 128 * (h // 2):128 * (h // 2) + 128]
            if (h % 2) != (h // 4):
                chunk = pltpu.roll(chunk, 64, 1)
            rows.append(chunk)
        qhs.append(jnp.where(kv_half, jnp.concatenate(rows, axis=0), 0.0))
    qblk = jnp.concatenate([jnp.concatenate([qhs[b] if c == b else zero for c in range(bb)], axis=1)
                            for b in range(bb)], axis=0)
    kt = kt_ref[...].reshape(bb * 128, W_B).astype(BF16)
    vt = vt_ref[...].reshape(bb * 128, W_B).astype(BF16)
    tile8 = lambda ref: jnp.concatenate([ref[...]] * bb, axis=0)
    s = _dot(qblk.astype(BF16), kt) * scale + tile8(bias_ref)
    knew = jnp.concatenate([jnp.broadcast_to(kn_ref[b:b + 1, :], (8, 128)) for b in range(bb)], axis=0)
    vnew = jnp.concatenate([jnp.broadcast_to(vn_ref[b:b + 1, :], (8, 128)) for b in range(bb)], axis=0)
    qall = jnp.concatenate(qhs, axis=0)
    ln = jnp.sum(qall * knew, axis=-1, keepdims=True) * scale + tile8(b0_ref)[:, 0:1]
    sk = tile8(sk_ref)[:, 0:1]
    m = jnp.maximum(jnp.maximum(jnp.max(s, axis=-1, keepdims=True), ln), sk)
    p = jnp.exp(s - m)
    pn = jnp.exp(ln - m)
    den = jnp.sum(p, axis=-1, keepdims=True) + pn + jnp.exp(sk - m)
    rall = _dot_nt(p.astype(BF16), vt)
    for b in range(bb):
        r = (rall[8 * b:8 * b + 8, 128 * b:128 * b + 128] + pn[8 * b:8 * b + 8] * vnew[8 * b:8 * b + 8])
        r = jnp.where(kv_half, r / den[8 * b:8 * b + 8], 0.0)
        chunks = []
        for c2 in range(4):
            pair = []
            for h in (2 * c2, 2 * c2 + 1):
                piece = r[h:h + 1, :]
                if (h % 2) != (h // 4):
                    piece = pltpu.roll(piece, 64, 1)
                pair.append(piece)
            chunks.append(jnp.where(low, pair[0], pair[1]))
        o_ref[b:b + 1, :] = jnp.concatenate(chunks, axis=1)


def _samp_b_call(qb, kn, vn, cache_k, cache_v, bias_sb, s0b, skb):
    nb = qb.shape[0]
    bb = 8
    kt = jnp.transpose(cache_k, (0, 2, 3, 1)).reshape(nb, 128, W_B)
    vt = jnp.transpose(cache_v, (0, 2, 3, 1)).reshape(nb, 128, W_B)
    row = lambda w: pl.BlockSpec((bb, w), lambda i: (i, 0))
    cblk = pl.BlockSpec((bb, 128, W_B), lambda i: (i, 0, 0))
    return pl.pallas_call(
        _samp_b_kernel,
        grid=(nb // bb,),
        in_specs=[row(512), row(128), row(128), cblk, cblk, _const_spec((8, W_B)),
                  _const_spec((8, 128)), _const_spec((8, 128))],
        out_specs=row(512),
        out_shape=jax.ShapeDtypeStruct((nb, 512), F32),
        compiler_params=_cparams(("arbitrary",)),
        name="sample_swa_attn",
    )(qb, kn, vn, kt, vt, bias_sb, s0b, skb)


def _samp_c_kernel(q_ref, mk_ref, mv_ref, mask_ref, o_ref):
    scale = HD_C ** -0.5
    zeros = jnp.zeros((4, HD_C), F32)

    def body(b, c):
        q8 = jnp.concatenate([q_ref[b], zeros], axis=0).astype(BF16)
        s = _dot_nt(q8, mk_ref[b].astype(BF16)) * scale + mask_ref[...]
        m = jnp.max(s, axis=-1, keepdims=True)
        p = jnp.exp(s - m)
        den = jnp.sum(p, axis=-1, keepdims=True)
        r = _dot(p.astype(BF16), mv_ref[b].astype(BF16)) / den
        o_ref[b] = r[0:4, :]
        return c

    lax.fori_loop(0, q_ref.shape[0], body, 0, unroll=True)


def _samp_c_call(qc, cache_k, cache_v):
    nb = qc.shape[0]
    bb = 8
    rows = N_MEM * H_C
    head_of_row = np.arange(rows)[None, :] % H_C
    mask = np.where(head_of_row == np.arange(8)[:, None], 0.0, NEG).astype(np.float32)
    qblk = pl.BlockSpec((bb, H_C, HD_C), lambda i: (i, 0, 0))
    cblk = pl.BlockSpec((bb, rows, HD_C), lambda i: (i, 0, 0))
    out = pl.pallas_call(
        _samp_c_kernel,
        grid=(nb // bb,),
        in_specs=[qblk, cblk, cblk, _const_spec((8, rows))],
        out_specs=qblk,
        out_shape=jax.ShapeDtypeStruct((nb, H_C, HD_C), F32),
        compiler_params=_cparams(("arbitrary",)),
        name="sample_mem_attn",
    )(qc.reshape(nb, H_C, HD_C), cache_k.reshape(nb, rows, HD_C), cache_v.reshape(nb, rows, HD_C),
      jnp.asarray(mask))
    return out.reshape(nb, 512)


def kernel(x_prompt, x_sample, mem_prompt, cache_a_k, cache_a_v, cache_b_k, cache_b_v, cache_mem_k,
           cache_mem_v, rel_bias, ln_g, w_in, gq_a, gk_a, gq_b, gk_b, gq_c, gk_c, sinks_b, mem_ln_g,
           w_mem_kv, w_br_a, w_br_b, w_br_c, w_out):
    batch, seq, _ = x_prompt.shape
    nsamp = x_sample.shape[0]
    assert ln_g.shape[0] == 1 and x_sample.shape[1] == 1
    assert (batch, seq, nsamp) == (8, 2048, 128) and w_in.shape == (1, D_MODEL, C_END)
    assert cache_a_k.shape == (1, nsamp, W_A, H_A, HD_A) and cache_b_k.shape == (1, nsamp, W_B, KV_B, HD_B)

    w_bf = w_in[0].astype(BF16)
    wmem_bf = w_mem_kv[0].astype(BF16)
    wbr_bf = jnp.stack([w_br_a[0], w_br_b[0], w_br_c[0]]).astype(BF16)
    wout_bf = w_out[0].astype(BF16)
    lng = ln_g.reshape(1, D_MODEL)
    gains = jnp.stack([jnp.tile(gq_a[0], 8), jnp.tile(gk_a[0], 8), jnp.tile(gq_b[0], 8),
                       jnp.tile(gk_b[0], 8), jnp.tile(gq_c[0], 4)])
    gk_c4 = jnp.tile(gk_c[0], 4).reshape(1, 512)
    blockdiag = np.kron(np.eye(4, dtype=np.float32), np.ones((64, 64), np.float32))
    pmat = jnp.asarray(blockdiag, BF16)

    *tables, mult_a = _bucket_tables()
    bias_a, bias_b, bias_sa, bias_sb, s0a, s0b, skb = _bias_call(rel_bias, sinks_b.reshape(H_B), tables)

    xs2 = x_sample.reshape(nsamp, D_MODEL)
    sqa, ska, sva, sqb, skb_new, svb, sqc = _qkv_call(xs2, lng, w_bf, gains, pmat)
    riders = _rider_arrays(sqa, ska, sva, cache_a_k[0], cache_a_v[0], bias_sa, jnp.asarray(mult_a), s0a)
    out_tm = 256
    n_qkv = batch * seq // 512
    n_dil = batch * (H_A // 2)
    n_out = batch * seq // out_tm
    n_alone = nsamp - n_qkv - n_dil - n_out
    assert n_alone >= 0

    xp2 = x_prompt.reshape(batch * seq, D_MODEL)
    qa, ka, va, qb, kb, vb, qc, ka_t, va_t, kb_t, vb_t, soa_1 = _qkv_call(
        xp2, lng, w_bf, gains, pmat, (batch, seq), rider=(riders, 0))
    mk, mv = _memkv_call(mem_prompt.reshape(batch * N_MEM, D_MODEL), mem_ln_g.reshape(1, D_MODEL),
                         wmem_bf, gk_c4)
    oa, soa_2 = _dil_call(qa, ka, va, bias_a, batch, seq, rider=(riders, n_qkv))
    ob = _swa_call(sinks_b.reshape(H_B), qb, kb, vb, bias_b, batch, seq)
    oc = _mem_attn_call(qc, mk, mv, batch, seq)
    yp, soa_3 = _out_call(xp2, oa.reshape(-1, 512), ob.reshape(-1, 512), oc.reshape(-1, 512),
                          lng, w_bf, wbr_bf, wout_bf, tm=out_tm, rider=(riders, n_qkv + n_dil))
    yp = yp.reshape(batch, seq, D_MODEL)

    soa_parts = [soa_1, soa_2, soa_3]
    if n_alone:
        soa_parts.append(_samp_a_call(riders, n_qkv + n_dil + n_out, n_alone))
    soa = jnp.concatenate(soa_parts, axis=0).reshape(nsamp, 512)
    sob = _samp_b_call(sqb, skb_new, svb, cache_b_k[0], cache_b_v[0], bias_sb, s0b, skb)
    soc = _samp_c_call(sqc, cache_mem_k[0], cache_mem_v[0])
    ys, = _out_call(xs2, soa, sob, soc, lng, w_bf, wbr_bf, wout_bf)

    return (yp, ys.reshape(nsamp, 1, D_MODEL),
            jnp.transpose(ka_t.reshape(1, batch, H_A, HD_A, seq), (0, 1, 4, 2, 3)),
            jnp.transpose(va_t.reshape(1, batch, H_A, HD_A, seq), (0, 1, 4, 2, 3)),
            jnp.transpose(kb_t.reshape(1, batch, KV_B, HD_B, W_B), (0, 1, 4, 2, 3)),
            jnp.transpose(vb_t.reshape(1, batch, KV_B, HD_B, W_B), (0, 1, 4, 2, 3)),
            mk.reshape(1, batch, N_MEM, H_C, HD_C), mv.reshape(1, batch, N_MEM, H_C, HD_C),
            ska.reshape(1, nsamp, 1, H_A, HD_A), sva.reshape(1, nsamp, 1, H_A, HD_A),
            skb_new.reshape(1, nsamp, 1, KV_B, HD_B), svb.reshape(1, nsamp, 1, KV_B, HD_B))
```

```python
import functools
import math

import numpy as np
import jax
import jax.numpy as jnp
from jax import lax
from jax.experimental import pallas as pl
from jax.experimental.pallas import tpu as pltpu

F32 = jnp.float32
BF16 = jnp.bfloat16

D_MODEL = 1024
H_A, HD_A = 8, 64
A_DILATIONS = (1, 4, 16)
A_STEPS = 128
W_A = 2048
H_B, KV_B, HD_B, W_B = 8, 2, 64, 128
H_C, HD_C, N_MEM = 4, 128, 256
NUM_BUCKETS, MAX_DISTANCE = 32, 2048
EPS = 1e-6
NEG = -1e30
BLK = 128
UNROLL = 16

IN_COLS = dict(qa=(0, 512), ka=(512, 512), va=(1024, 512), za=(1536, 512),
               qb=(2048, 512), kb=(2560, 128), vb=(2688, 128), zb=(2816, 512),
               qc=(3328, 512), zc=(3840, 512), ga=(4352, 1024), gb=(5376, 1024), gc=(6400, 1024))
C_END = 7424
QKV_PARTS = ("qa", "ka", "va", "qb", "kb", "vb", "qc")
OUT_PARTS = ("za", "zb", "zc", "ga", "gb", "gc")


def _packed_offsets(parts):
    offs, at = {}, 0
    for name in parts:
        offs[name] = at
        at += IN_COLS[name][1]
    return offs, at


QKV_OFF, QKV_WIDTH = _packed_offsets(QKV_PARTS)
OUT_OFF, OUT_WIDTH = _packed_offsets(OUT_PARTS)


def _pack_columns(w, parts):
    return jnp.concatenate([w[:, IN_COLS[p][0]:IN_COLS[p][0] + IN_COLS[p][1]] for p in parts], axis=1)

VMEM_LIMIT = 56 * 1024 * 1024


def _cparams(sem):
    return pltpu.CompilerParams(dimension_semantics=sem, vmem_limit_bytes=VMEM_LIMIT)


def _const_spec(shape):
    nd = len(shape)
    return pl.BlockSpec(shape, lambda *_: (0,) * nd, pipeline_mode=pl.Buffered(1))


def _t5_bucket_np(dist):
    n = np.maximum(dist, 0)
    max_exact = NUM_BUCKETS // 2
    nf = np.maximum(n, 1).astype(np.float32)
    large = max_exact + (np.log(nf / np.float32(max_exact))
                         / np.float32(math.log(MAX_DISTANCE / max_exact))
                         * np.float32(NUM_BUCKETS - max_exact)).astype(np.int32)
    return np.where(n < max_exact, n, np.minimum(large, NUM_BUCKETS - 1)).astype(np.int32)


def _a_multiplicity(dist):
    return sum(((dist % d == 0) & (dist <= A_STEPS * d)).astype(np.int32) for d in A_DILATIONS)


def _bucket_tables():
    i = np.arange(BLK)[:, None]
    j = np.arange(2 * BLK)[None, :]
    rel = i + BLK - j
    bkt_a = np.stack([np.where((rel >= 0) & (rel <= A_STEPS), _t5_bucket_np(rel * d), -1)
                      for d in A_DILATIONS]).astype(np.int32)
    bkt_b = np.where((rel >= 0) & (rel < W_B), _t5_bucket_np(rel), -1).astype(np.int32)
    dist_a = W_A - np.arange(W_A)
    mult_a = _a_multiplicity(dist_a)
    sb_a = np.broadcast_to(np.where(mult_a > 0, _t5_bucket_np(dist_a), -1), (8, W_A)).astype(np.int32)
    dist_b = W_B - np.arange(W_B)
    sb_b = np.broadcast_to(np.where(dist_b < W_B, _t5_bucket_np(dist_b), -1), (8, W_B)).astype(np.int32)
    return bkt_a, bkt_b, sb_a, sb_b, mult_a.astype(np.float32).reshape(1, W_A)


def _rms_rows(x, g):
    return x * lax.rsqrt(jnp.mean(x * x, axis=-1, keepdims=True) + EPS) * g


def _dot(a, b):
    return jnp.dot(a, b, preferred_element_type=F32)


def _dot_nt(a, b):
    return lax.dot_general(a, b, (((1,), (1,)), ((), ())), preferred_element_type=F32)


def _headnorm64(t, g, pmat):
    sq = (t * t).astype(BF16)
    width = t.shape[1]
    step = min(width, 256)
    pp = pmat[:step, :step]
    parts = [_dot(sq[:, c:c + step], pp) for c in range(0, width, step)]
    ss = parts[0] if len(parts) == 1 else jnp.concatenate(parts, axis=1)
    return t * lax.rsqrt(ss * (1.0 / 64.0) + EPS) * g


def _headnorm128(t, g):
    parts = []
    for c in range(0, t.shape[1], 128):
        seg = t[:, c:c + 128]
        parts.append(seg * lax.rsqrt(jnp.mean(seg * seg, axis=-1, keepdims=True) + EPS))
    return jnp.concatenate(parts, axis=1) * g


def _bias_kernel(tbl_ref, sink_ref, bkta_ref, bktb_ref, sba_ref, sbb_ref,
                 ba_ref, bb_ref, oa_ref, ob_ref, s0a_ref, s0b_ref, sk_ref, *, present):
    pa, pb, psa, psb = present
    for g in range(3):
        bk = bkta_ref[g]
        for h in range(H_A):
            ba_ref[g, h] = jnp.full((BLK, 2 * BLK), NEG, F32)
        for b in pa[g]:
            hit = bk == b
            for h in range(H_A):
                ba_ref[g, h] = jnp.where(hit, tbl_ref[b, h], ba_ref[g, h])
    bk = bktb_ref[...]
    for h in range(H_B):
        bb_ref[h] = jnp.full((BLK, 2 * BLK), NEG, F32)
    for b in pb:
        hit = bk == b
        for h in range(H_B):
            bb_ref[h] = jnp.where(hit, tbl_ref[b, H_A + h], bb_ref[h])
    for (src, dst, buckets, col0) in ((sba_ref, oa_ref, psa, 0), (sbb_ref, ob_ref, psb, H_A)):
        bk = src[...]
        row = lax.broadcasted_iota(jnp.int32, bk.shape, 0)
        acc = jnp.full(bk.shape, NEG, F32)
        for b in buckets:
            hit = bk == b
            for h in range(8):
                acc = jnp.where(hit & (row == h), tbl_ref[b, col0 + h], acc)
        dst[...] = acc
    row8 = lax.broadcasted_iota(jnp.int32, (8, 128), 0)
    a0 = jnp.zeros((8, 128), F32)
    b0 = jnp.zeros((8, 128), F32)
    sk = jnp.zeros((8, 128), F32)
    for h in range(8):
        a0 = jnp.where(row8 == h, tbl_ref[0, h], a0)
        b0 = jnp.where(row8 == h, tbl_ref[0, H_A + h], b0)
        sk = jnp.where(row8 == h, sink_ref[h], sk)
    s0a_ref[...] = a0
    s0b_ref[...] = b0
    sk_ref[...] = sk


def _bias_call(rel_bias, sinks, tables):
    bkt_a, bkt_b, sb_a, sb_b = tables
    uniq = lambda a: tuple(int(b) for b in np.unique(a) if b >= 0)
    present = (tuple(uniq(bkt_a[g]) for g in range(3)), uniq(bkt_b), uniq(sb_a), uniq(sb_b))
    smem = pl.BlockSpec(memory_space=pltpu.SMEM)
    vmem = pl.BlockSpec(memory_space=pltpu.VMEM)
    return pl.pallas_call(
        functools.partial(_bias_kernel, present=present),
        in_specs=[smem, smem, vmem, vmem, vmem, vmem],
        out_specs=[vmem] * 7,
        out_shape=[jax.ShapeDtypeStruct((3, H_A, BLK, 2 * BLK), F32),
                   jax.ShapeDtypeStruct((H_B, BLK, 2 * BLK), F32),
                   jax.ShapeDtypeStruct((8, W_A), F32),
                   jax.ShapeDtypeStruct((8, W_B), F32),
                   jax.ShapeDtypeStruct((8, 128), F32),
                   jax.ShapeDtypeStruct((8, 128), F32),
                   jax.ShapeDtypeStruct((8, 128), F32)],
        compiler_params=pltpu.CompilerParams(vmem_limit_bytes=VMEM_LIMIT),
        name="bias_expand",
    )(rel_bias, sinks, jnp.asarray(bkt_a), jnp.asarray(bkt_b), jnp.asarray(sb_a), jnp.asarray(sb_b))


def _qkv_kernel(x_ref, lng_ref, w_ref, g_ref, p_ref, *refs, rider=False, tiles_per_seq=1):
    if rider:
        _ride(refs[:N_RIDER_IN], refs[-1])
        refs = refs[N_RIDER_IN:-1]
    qa_ref, ka_ref, va_ref, qb_ref, kb_ref, vb_ref, qc_ref, *cache_refs = refs
    h = _rms_rows(x_ref[...], lng_ref[...]).astype(BF16)
    pmat = p_ref[...]

    def proj(part):
        c0 = QKV_OFF[part]
        return _dot(h, w_ref[:, c0:c0 + IN_COLS[part][1]])

    qa_ref[...] = _headnorm64(proj("qa"), g_ref[0:1, :], pmat)
    ka = _headnorm64(proj("ka"), g_ref[1:2, :], pmat)
    va = proj("va")
    ka_ref[...] = ka
    va_ref[...] = va
    kb = _headnorm64(proj("kb"), g_ref[3:4, 0:128], pmat)
    vb = proj("vb")
    if cache_refs:
        kt_ref, vt_ref, kbt_ref, vbt_ref = cache_refs
        kt_ref[...] = ka.T
        vt_ref[...] = va.T

        @pl.when((pl.program_id(0) + 1) % tiles_per_seq == 0)
        def _():
            tm = kb.shape[0]
            kbt_ref[...] = kb[tm - W_B:, :].T
            vbt_ref[...] = vb[tm - W_B:, :].T
    qb_ref[...] = _headnorm64(proj("qb"), g_ref[2:3, :], pmat).astype(qb_ref.dtype)
    kb_ref[...] = kb
    vb_ref[...] = vb
    qc_ref[...] = _headnorm128(proj("qc"), g_ref[4:5, :]).astype(qc_ref.dtype)


def _qkv_call(x2, lng, w_bf, gains, pmat, cache_layout_for=None, rider=None):
    n = x2.shape[0]
    tm = min(512, n)
    steps = n // tm
    row = lambda w: pl.BlockSpec((tm, w), lambda i: (i, 0))
    widths = (512, 512, 512, 512, 128, 128, 512)
    in_specs = [row(D_MODEL), _const_spec((1, D_MODEL)), _const_spec((D_MODEL, QKV_WIDTH)),
                _const_spec((5, 512)), _const_spec((256, 256))]
    args = [x2, lng, w_bf, gains, pmat]
    out_specs = [row(w) for w in widths]
    q_dtype = F32 if cache_layout_for is None else BF16
    out_shape = [jax.ShapeDtypeStruct((n, w), q_dtype if i in (3, 6) else F32) for i, w in enumerate(widths)]
    per = 1
    if cache_layout_for is not None:
        batch, seq = cache_layout_for
        per = seq // tm
        out_specs += [pl.BlockSpec((None, 512, tm), lambda i: (i // per, 0, i % per))] * 2
        out_shape += [jax.ShapeDtypeStruct((batch, 512, seq), F32)] * 2
        out_specs += [pl.BlockSpec((None, KV_B * HD_B, W_B), lambda i: (i // per, 0, 0))] * 2
        out_shape += [jax.ShapeDtypeStruct((batch, KV_B * HD_B, W_B), F32)] * 2
    if rider is not None:
        r_in, r_out = _rider_specs(rider[1], lambda i: i)
        in_specs += r_in
        args += list(rider[0])
        out_specs.append(r_out)
        out_shape.append(jax.ShapeDtypeStruct((steps, 1, 512), F32))
    return pl.pallas_call(
        functools.partial(_qkv_kernel, rider=rider is not None, tiles_per_seq=per),
        grid=(steps,),
        in_specs=in_specs,
        out_specs=out_specs,
        out_shape=out_shape,
        compiler_params=_cparams(("arbitrary",)),
        name="qkv_proj",
    )(*args)


def _memkv_kernel(m_ref, g_ref, w_ref, gk_ref, mk_ref, mv_ref):
    h = _rms_rows(m_ref[...], g_ref[...]).astype(BF16)
    mk_ref[...] = _headnorm128(_dot(h, w_ref[:, 0:512]), gk_ref[...])
    mv_ref[...] = _dot(h, w_ref[:, 512:1024])


def _memkv_call(mem2, g, w_bf, gk):
    n = mem2.shape[0]
    tm = 512
    row = lambda w: pl.BlockSpec((tm, w), lambda i: (i, 0))
    return pl.pallas_call(
        _memkv_kernel,
        grid=(n // tm,),
        in_specs=[row(D_MODEL), _const_spec((1, D_MODEL)), _const_spec((D_MODEL, 2 * 512)),
                  _const_spec((1, 512))],
        out_specs=[row(512), row(512)],
        out_shape=[jax.ShapeDtypeStruct((n, 512), F32)] * 2,
        compiler_params=_cparams(("arbitrary",)),
        name="mem_kv_proj",
    )(mem2, g, w_bf, gk)


def _out_kernel(x_ref, oa_ref, ob_ref, oc_ref, lng_ref, w_ref, wbr_ref, wout_ref, *refs):
    if len(refs) > 1:
        _ride(refs[:N_RIDER_IN], refs[-1])
    y_ref = refs[-2] if len(refs) > 1 else refs[0]
    x = x_ref[...]
    h = _rms_rows(x, lng_ref[...]).astype(BF16)
    acc = None
    for br, (o_ref, zpart, gpart) in enumerate(((oa_ref, "za", "ga"), (ob_ref, "zb", "gb"), (oc_ref, "zc", "gc"))):
        cz, cg = OUT_OFF[zpart], OUT_OFF[gpart]
        z = _dot(h, w_ref[:, cz:cz + 512])
        y = (o_ref[...] * (z * jax.nn.sigmoid(z))).astype(BF16)
        yb = _dot(y, wbr_ref[br])
        gate = jax.nn.sigmoid(_dot(h, w_ref[:, cg:cg + D_MODEL]))
        acc = gate * yb if acc is None else acc + gate * yb
    y_ref[...] = x + _dot(acc.astype(BF16), wout_ref[...])


def _out_call(x2, oa, ob, oc, lng, w_bf, wbr_bf, wout_bf, tm=512, rider=None):
    n = x2.shape[0]
    tm = min(tm, n)
    steps = n // tm
    row = lambda w: pl.BlockSpec((tm, w), lambda i: (i, 0))
    in_specs = [row(D_MODEL), row(512), row(512), row(512), _const_spec((1, D_MODEL)),
                _const_spec((D_MODEL, OUT_WIDTH)), _const_spec((3, 512, D_MODEL)), _const_spec((D_MODEL, D_MODEL))]
    args = [x2, oa, ob, oc, lng, w_bf, wbr_bf, wout_bf]
    out_specs = [row(D_MODEL)]
    out_shape = [jax.ShapeDtypeStruct((n, D_MODEL), F32)]
    if rider is not None:
        r_in, r_out = _rider_specs(rider[1], lambda i: i)
        in_specs += r_in
        args += list(rider[0])
        out_specs.append(r_out)
        out_shape.append(jax.ShapeDtypeStruct((steps, 1, 512), F32))
    return pl.pallas_call(
        _out_kernel,
        grid=(steps,),
        in_specs=in_specs,
        out_specs=out_specs,
        out_shape=out_shape,
        compiler_params=_cparams(("arbitrary",)),
        name="gated_out",
    )(*args)


PAD = BLK
CLS_PITCH = BLK + 8


def _low_lanes(rows):
    return lax.broadcasted_iota(jnp.int32, (rows, 128), 1) < 64


def _aligned(start):
    return start if isinstance(start, int) else pl.multiple_of(start, BLK)


def _pair_tile(q, k0, k1, v0, v1, b0, b1, sinks=None):
    nk = k0.shape[0]
    low = _low_lanes(BLK)
    s = _dot_nt(q, jnp.concatenate([k0, k1], axis=0)) + jnp.concatenate([b0, b1], axis=1)
    ms = [jnp.max(s[:, hh * nk:(hh + 1) * nk], axis=-1, keepdims=True) for hh in range(2)]
    if sinks is not None:
        ms = [jnp.maximum(ms[hh], sinks[hh]) for hh in range(2)]
    p = jnp.concatenate([jnp.exp(s[:, hh * nk:(hh + 1) * nk] - ms[hh]) for hh in range(2)], axis=1).astype(BF16)
    own = _low_lanes(nk)
    ones = [jnp.where(own, 1.0, 0.0).astype(BF16), jnp.where(own, 0.0, 1.0).astype(BF16)]
    vblk = jnp.concatenate([jnp.concatenate([v0, ones[0]], axis=1),
                            jnp.concatenate([v1, ones[1]], axis=1)], axis=0)
    up = _dot(p, vblk)
    return up[:, :128], jnp.where(low, ms[0], ms[1]), up[:, 128:]


def _first_bias(bias, also=None):
    gone = lax.broadcasted_iota(jnp.int32, bias.shape, 1) < BLK
    if also is not None:
        gone = gone & also
    return jnp.where(gone, NEG, bias)


def _dil_kernel(q_ref, k_ref, v_ref, bias_ref, *refs, rider=False):
    if rider:
        _ride(refs[:N_RIDER_IN], refs[N_RIDER_IN + 1])
        refs = refs[N_RIDER_IN:N_RIDER_IN + 1] + refs[N_RIDER_IN + 2:]
    o_ref, qs_ref, kc_ref, vc_ref, c4_ref, ru_ref, rm_ref, rl_ref = refs
    scale = HD_A ** -0.5
    seq = q_ref.shape[0]
    ntile = seq // BLK
    low = _low_lanes(BLK)
    d1, d2 = A_DILATIONS[1], A_DILATIONS[2]

    def src_rows(g, t):
        d = A_DILATIONS[g]
        if d == 1:
            return pl.ds(_aligned(t * BLK), BLK)
        nblk = ntile // d
        r, n = t >> (nblk.bit_length() - 1), t & (nblk - 1)
        return pl.ds(r + n * (d * BLK), BLK, stride=d)

    zero = jnp.zeros((PAD, 128), BF16)
    stagers = []
    for g in range(3):
        for hh in range(2):
            kc_ref[g, hh, 0:PAD, :] = zero
            vc_ref[g, hh, 0:PAD, :] = zero

        def stage(t, c, g=g):
            dst = pl.ds(_aligned(t * BLK), BLK)
            dstp = pl.ds(_aligned(PAD + t * BLK), BLK)
            if g < 2:
                src = src_rows(g, t)
                q, k, v = q_ref[src, :], k_ref[src, :], v_ref[src, :]
            else:
                src = pl.ds((t & (d1 - 1)) * (seq // d1) + (t >> (d1.bit_length() - 1)), BLK, stride=d2 // d1)
                q, k, v = c4_ref[0, src, :], c4_ref[1, src, :], c4_ref[2, src, :]
            if g == 1:
                c4_ref[0, dst, :], c4_ref[1, dst, :], c4_ref[2, dst, :] = q, k, v
            qs_ref[g, dst, :] = (q * scale).astype(BF16)
            kc_ref[g, 0, dstp, :] = jnp.where(low, k, 0.0).astype(BF16)
            kc_ref[g, 1, dstp, :] = jnp.where(low, 0.0, k).astype(BF16)
            vc_ref[g, 0, dstp, :] = jnp.where(low, v, 0.0).astype(BF16)
            vc_ref[g, 1, dstp, :] = jnp.where(low, 0.0, v).astype(BF16)
            return c

        stagers.append(stage)

    def tiles(g, i):
        nblk = ntile // A_DILATIONS[g]
        res = []
        for j in range(UNROLL):
            t = i * UNROLL + j
            qrows = pl.ds(_aligned(t * BLK), BLK)
            krows = pl.ds(_aligned(t * BLK), 2 * BLK)
            bias = [bias_ref[g, hh] for hh in range(2)]
            if nblk <= UNROLL and j % nblk == 0:
                bias = [_first_bias(b) for b in bias]
            elif nblk > UNROLL and j == 0:
                bias = [_first_bias(b, i == 0) for b in bias]
            res.append(_pair_tile(qs_ref[g, qrows, :], kc_ref[g, 0, krows, :], kc_ref[g, 1, krows, :],
                                  vc_ref[g, 0, krows, :], vc_ref[g, 1, krows, :], bias[0], bias[1]))
        return res

    assert UNROLL == ntile
    for g in range(3):
        for t in range(ntile):
            stagers[g](t, 0)
        res = tiles(g, 0)
        for t in range(ntile):
            rows = pl.ds(t * CLS_PITCH, BLK) if g == 2 else src_rows(g, t)
            ru_ref[g, rows, :], rm_ref[g, rows, :], rl_ref[g, rows, :] = res[t]

    def natural_rows(ref, i):
        parts = [ref[2, pl.ds((v % 2) * 8 * CLS_PITCH + i * (BLK // d2) + v // 2, 8, stride=CLS_PITCH), :]
                 for v in range(BLK // 8)]
        return jnp.concatenate(parts, axis=0)

    def combine(i, c):
        rows = pl.ds(_aligned(i * BLK), BLK)
        ms = [rm_ref[0, rows, :], rm_ref[1, rows, :], natural_rows(rm_ref, i)]
        mx = jnp.maximum(jnp.maximum(ms[0], ms[1]), ms[2])
        ws = [jnp.exp(m - mx) for m in ms]
        num = ws[0] * ru_ref[0, rows, :] + ws[1] * ru_ref[1, rows, :] + ws[2] * natural_rows(ru_ref, i)
        den = ws[0] * rl_ref[0, rows, :] + ws[1] * rl_ref[1, rows, :] + ws[2] * natural_rows(rl_ref, i)
        o_ref[rows, :] = num / den
        return c

    lax.fori_loop(0, ntile, combine, 0, unroll=2)


def _dil_call(qa, ka, va, bias_a, batch, seq, rider=None):
    assert seq == A_DILATIONS[2] * BLK and UNROLL % (seq // (A_DILATIONS[1] * BLK)) == 0
    blk = pl.BlockSpec((None, seq, 128), lambda b, p: (b, 0, p))
    pairs = H_A // 2
    in_specs = [blk, blk, blk, pl.BlockSpec((3, 2, BLK, 2 * BLK), lambda b, p: (0, p, 0, 0))]
    args = [qa.reshape(batch, seq, 512), ka.reshape(batch, seq, 512), va.reshape(batch, seq, 512), bias_a]
    out_specs = [blk]
    out_shape = [jax.ShapeDtypeStruct((batch, seq, 512), F32)]
    if rider is not None:
        r_in, r_out = _rider_specs(rider[1], lambda b, p: b * pairs + p)
        in_specs += r_in
        args += list(rider[0])
        out_specs.append(r_out)
        out_shape.append(jax.ShapeDtypeStruct((batch * pairs, 1, 512), F32))
    return pl.pallas_call(
        functools.partial(_dil_kernel, rider=rider is not None),
        grid=(batch, pairs),
        in_specs=in_specs,
        out_specs=out_specs,
        out_shape=out_shape,
        scratch_shapes=[pltpu.VMEM((3, seq, 128), BF16), pltpu.VMEM((3, 2, PAD + seq, 128), BF16),
                        pltpu.VMEM((3, 2, PAD + seq, 128), BF16), pltpu.VMEM((3, seq, 128), F32)]
                       + [pltpu.VMEM((3, A_DILATIONS[2] * CLS_PITCH, 128), F32)] * 3,
        compiler_params=_cparams(("arbitrary", "arbitrary")),
        name="dilated_attn",
    )(*args)


def _swa_kernel(sink_ref, q_ref, k_ref, v_ref, bias_ref, o_ref, qs_ref, kc_ref, vc_ref):
    scale = HD_B ** -0.5
    seq = q_ref.shape[0]
    ntile = seq // BLK
    p = pl.program_id(1)
    kv = p // 2
    sinks = (sink_ref[2 * p], sink_ref[2 * p + 1])
    low = _low_lanes(BLK)
    own_half = (lax.broadcasted_iota(jnp.int32, (BLK, 128), 1) >= 64).astype(jnp.int32) == kv

    def dup(xf):
        return jnp.where(own_half, xf, pltpu.roll(xf, 64, 1))

    zero = jnp.zeros((PAD, 128), BF16)
    for hh in range(2):
        kc_ref[hh, 0:PAD, :] = zero
        vc_ref[hh, 0:PAD, :] = zero

    def stage(t, c):
        src = pl.ds(pl.multiple_of(t * BLK, BLK), BLK)
        dstp = pl.ds(pl.multiple_of(PAD + t * BLK, BLK), BLK)
        qs_ref[src, :] = (q_ref[src, :] * scale).astype(BF16)
        k = dup(k_ref[src, :])
        v = dup(v_ref[src, :])
        kc_ref[0, dstp, :] = jnp.where(low, k, 0.0).astype(BF16)
        kc_ref[1, dstp, :] = jnp.where(low, 0.0, k).astype(BF16)
        vc_ref[0, dstp, :] = jnp.where(low, v, 0.0).astype(BF16)
        vc_ref[1, dstp, :] = jnp.where(low, 0.0, v).astype(BF16)
        return c

    lax.fori_loop(0, ntile, stage, 0, unroll=4)

    def body(i, c):
        res = []
        for j in range(UNROLL):
            t = i * UNROLL + j
            qrows = pl.ds(pl.multiple_of(t * BLK, BLK), BLK)
            krows = pl.ds(pl.multiple_of(t * BLK, BLK), 2 * BLK)
            bias = [bias_ref[hh] for hh in range(2)]
            if j == 0:
                bias = [_first_bias(b, i == 0) for b in bias]
            u, m, l = _pair_tile(qs_ref[qrows, :], kc_ref[0, krows, :], kc_ref[1, krows, :],
                                 vc_ref[0, krows, :], vc_ref[1, krows, :], bias[0], bias[1], sinks)
            sk = jnp.where(low, sinks[0], sinks[1])
            res.append(u / (l + jnp.exp(sk - m)))
        for j in range(UNROLL):
            o_ref[pl.ds(pl.multiple_of((i * UNROLL + j) * BLK, BLK), BLK), :] = res[j]
        return c

    lax.fori_loop(0, ntile // UNROLL, body, 0)


def _swa_call(sinks, qb, kb, vb, bias_b, batch, seq):
    qblk = pl.BlockSpec((None, seq, 128), lambda b, p: (b, 0, p))
    kblk = pl.BlockSpec((None, seq, 128), lambda b, p: (b, 0, 0))
    return pl.pallas_call(
        _swa_kernel,
        grid=(batch, H_B // 2),
        in_specs=[pl.BlockSpec(memory_space=pltpu.SMEM), qblk, kblk, kblk,
                  pl.BlockSpec((2, BLK, 2 * BLK), lambda b, p: (p, 0, 0))],
        out_specs=qblk,
        out_shape=jax.ShapeDtypeStruct((batch, seq, 512), F32),
        scratch_shapes=[pltpu.VMEM((seq, 128), BF16), pltpu.VMEM((2, PAD + seq, 128), BF16),
                        pltpu.VMEM((2, PAD + seq, 128), BF16)],
        compiler_params=_cparams(("arbitrary", "arbitrary")),
        name="swa_attn",
    )(sinks, qb.reshape(batch, seq, 512), kb.reshape(batch, seq, 128), vb.reshape(batch, seq, 128), bias_b)


def _mem_attn_kernel(q_ref, mk_ref, mv_ref, o_ref):
    scale = HD_C ** -0.5
    ones = jnp.ones((N_MEM, HD_C), BF16)
    for h in range(H_C):
        cs = slice(h * HD_C, (h + 1) * HD_C)
        s = _dot_nt(q_ref[:, cs].astype(BF16), mk_ref[:, cs].astype(BF16)) * scale
        p = jnp.exp(s - jnp.max(s, axis=-1, keepdims=True)).astype(BF16)
        up = _dot(p, jnp.concatenate([mv_ref[:, cs].astype(BF16), ones], axis=1))
        o_ref[:, cs] = up[:, :HD_C] / up[:, HD_C:]


def _mem_attn_call(qc, mk, mv, batch, seq):
    tq = 512
    return pl.pallas_call(
        _mem_attn_kernel,
        grid=(batch, seq // tq),
        in_specs=[pl.BlockSpec((None, tq, 512), lambda b, i: (b, i, 0)),
                  pl.BlockSpec((None, N_MEM, 512), lambda b, i: (b, 0, 0)),
                  pl.BlockSpec((None, N_MEM, 512), lambda b, i: (b, 0, 0))],
        out_specs=pl.BlockSpec((None, tq, 512), lambda b, i: (b, i, 0)),
        out_shape=jax.ShapeDtypeStruct((batch, seq, 512), F32),
        compiler_params=_cparams(("arbitrary", "arbitrary")),
        name="mem_attn",
    )(qc.reshape(batch, seq, 512), mk.reshape(batch, N_MEM, 512), mv.reshape(batch, N_MEM, 512))


def _head_rows(row, width, head_dim):
    rid = lax.broadcasted_iota(jnp.int32, (8, width), 0)
    hid = lax.broadcasted_iota(jnp.int32, (8, width), 1) // head_dim
    return rid == hid, jnp.where(rid == hid, jnp.broadcast_to(row, (8, width)), 0.0)


def _samp_a_one(q, kn, vn, kt_ref, vt_ref, bias_ref, mult_ref, b0_ref):
    scale = HD_A ** -0.5
    nt = W_A // 128
    own, qh = _head_rows(q, 512, HD_A)
    qcols = jnp.transpose(jnp.broadcast_to(q, (128, 512)))
    s_heads = []
    for h in range(H_A):
        rows = slice(h * HD_A, (h + 1) * HD_A)
        qb = qcols[rows, :]
        s_heads.append(jnp.concatenate(
            [jnp.sum(kt_ref[rows, 128 * j:128 * (j + 1)] * qb, axis=0, keepdims=True) for j in range(nt)], axis=1))
    s = jnp.concatenate(s_heads, axis=0) * scale + bias_ref[...]
    ln = jnp.sum(qh * kn, axis=-1, keepdims=True) * scale + b0_ref[:, 0:1]
    m = jnp.maximum(jnp.max(s, axis=-1, keepdims=True), ln)
    p = jnp.exp(s - m) * mult_ref[...]
    pn = len(A_DILATIONS) * jnp.exp(ln - m)
    den = jnp.sum(p, axis=-1, keepdims=True) + pn
    accs = []
    for h in range(H_A):
        rows = slice(h * HD_A, (h + 1) * HD_A)
        acc = vt_ref[rows, 0:128] * p[h:h + 1, 0:128]
        for j in range(1, nt):
            acc = acc + vt_ref[rows, 128 * j:128 * (j + 1)] * p[h:h + 1, 128 * j:128 * (j + 1)]
        accs.append(acc)
    pv = jnp.sum(jnp.transpose(jnp.concatenate(accs, axis=0)), axis=0, keepdims=True)
    lanes = lambda col: jnp.sum(jnp.where(own, col, 0.0), axis=0, keepdims=True)
    return (pv + lanes(pn) * vn) / lanes(den)


N_RIDER_IN = 8


def _rider_arrays(qa, kn, vn, cache_k, cache_v, bias_sa, mult, s0a):
    nb = qa.shape[0]
    kt = jnp.transpose(cache_k, (0, 2, 3, 1)).reshape(nb, 512, W_A)
    vt = jnp.transpose(cache_v, (0, 2, 3, 1)).reshape(nb, 512, W_A)
    return (qa.reshape(nb, 1, 512), kn.reshape(nb, 1, 512), vn.reshape(nb, 1, 512), kt, vt, bias_sa, mult, s0a)


def _rider_specs(first, step_of):
    row = pl.BlockSpec((None, 1, 512), lambda *ids: (first + step_of(*ids), 0, 0))
    cblk = pl.BlockSpec((None, 512, W_A), lambda *ids: (first + step_of(*ids), 0, 0))
    ins = [row, row, row, cblk, cblk, _const_spec((8, W_A)), _const_spec((1, W_A)), _const_spec((8, 128))]
    return ins, pl.BlockSpec((None, 1, 512), lambda *ids: (step_of(*ids), 0, 0))


def _ride(in_refs, o_ref):
    q_ref, kn_ref, vn_ref, kt_ref, vt_ref, bias_ref, mult_ref, b0_ref = in_refs
    o_ref[...] = _samp_a_one(q_ref[...], kn_ref[...], vn_ref[...], kt_ref, vt_ref, bias_ref, mult_ref, b0_ref)


def _samp_a_kernel(*refs):
    _ride(refs[:N_RIDER_IN], refs[-1])


def _samp_a_call(arrays, first, count):
    r_in, r_out = _rider_specs(first, lambda i: i)
    return pl.pallas_call(
        _samp_a_kernel,
        grid=(count,),
        in_specs=r_in,
        out_specs=r_out,
        out_shape=jax.ShapeDtypeStruct((count, 1, 512), F32),
        compiler_params=_cparams(("arbitrary",)),
        name="sample_dilated_attn",
    )(*arrays)


def _samp_b_kernel(q_ref, kn_ref, vn_ref, kt_ref, vt_ref, bias_ref, b0_ref, sk_ref, o_ref):
    scale = HD_B ** -0.5
    low = lax.broadcasted_iota(jnp.int32, (1, 128), 1) < 64
    rid = lax.broadcasted_iota(jnp.int32, (8, 128), 0)
    kv_half = (lax.broadcasted_iota(jnp.int32, (8, 128), 1) >= 64) == (rid >= 4)

    bb = q_ref.shape[0]
    zero = jnp.zeros((8, 128), F32)
    qhs = []
    for b in range(bb):
        q = q_ref[b:b + 1, :]
        rows = []
        for h in range(H_B):
            chunk = q[:, 128 * (h // 2):128 * (h // 2) + 128]
            if (h % 2) != (h // 4):
                chunk = pltpu.roll(chunk, 64, 1)
            rows.append(chunk)
        qhs.append(jnp.where(kv_half, jnp.concatenate(rows, axis=0), 0.0))
    qblk = jnp.concatenate([jnp.concatenate([qhs[b] if c == b else zero for c in range(bb)], axis=1)
                            for b in range(bb)], axis=0)
    kt = kt_ref[...].reshape(bb * 128, W_B).astype(BF16)
    vt = vt_ref[...].reshape(bb * 128, W_B).astype(BF16)
    tile8 = lambda ref: jnp.concatenate([ref[...]] * bb, axis=0)
    s = _dot(qblk.astype(BF16), kt) * scale + tile8(bias_ref)
    knew = jnp.concatenate([jnp.broadcast_to(kn_ref[b:b + 1, :], (8, 128)) for b in range(bb)], axis=0)
    vnew = jnp.concatenate([jnp.broadcast_to(vn_ref[b:b + 1, :], (8, 128)) for b in range(bb)], axis=0)
    qall = jnp.concatenate(qhs, axis=0)
    ln = jnp.sum(qall * knew, axis=-1, keepdims=True) * scale + tile8(b0_ref)[:, 0:1]
    sk = tile8(sk_ref)[:, 0:1]
    m = jnp.maximum(jnp.maximum(jnp.max(s, axis=-1, keepdims=True), ln), sk)
    p = jnp.exp(s - m)
    pn = jnp.exp(ln - m)
    den = jnp.sum(p, axis=-1, keepdims=True) + pn + jnp.exp(sk - m)
    rall = _dot_nt(p.astype(BF16), vt)
    for b in range(bb):
        r = (rall[8 * b:8 * b + 8, 128 * b:128 * b + 128] + pn[8 * b:8 * b + 8] * vnew[8 * b:8 * b + 8])
        r = jnp.where(kv_half, r / den[8 * b:8 * b + 8], 0.0)
        chunks = []
        for c2 in range(4):
            pair = []
            for h in (2 * c2, 2 * c2 + 1):
                piece = r[h:h + 1, :]
                if (h % 2) != (h // 4):
                    piece = pltpu.roll(piece, 64, 1)
                pair.append(piece)
            chunks.append(jnp.where(low, pair[0], pair[1]))
        o_ref[b:b + 1, :] = jnp.concatenate(chunks, axis=1)


def _samp_b_call(qb, kn, vn, cache_k, cache_v, bias_sb, s0b, skb):
    nb = qb.shape[0]
    bb = 8
    kt = jnp.transpose(cache_k, (0, 2, 3, 1)).reshape(nb, 128, W_B)
    vt = jnp.transpose(cache_v, (0, 2, 3, 1)).reshape(nb, 128, W_B)
    row = lambda w: pl.BlockSpec((bb, w), lambda i: (i, 0))
    cblk = pl.BlockSpec((bb, 128, W_B), lambda i: (i, 0, 0))
    return pl.pallas_call(
        _samp_b_kernel,
        grid=(nb // bb,),
        in_specs=[row(512), row(128), row(128), cblk, cblk, _const_spec((8, W_B)),
                  _const_spec((8, 128)), _const_spec((8, 128))],
        out_specs=row(512),
        out_shape=jax.ShapeDtypeStruct((nb, 512), F32),
        compiler_params=_cparams(("arbitrary",)),
        name="sample_swa_attn",
    )(qb, kn, vn, kt, vt, bias_sb, s0b, skb)


def _samp_c_kernel(q_ref, mk_ref, mv_ref, mask_ref, o_ref):
    scale = HD_C ** -0.5
    zeros = jnp.zeros((4, HD_C), F32)

    def body(b, c):
        q8 = jnp.concatenate([q_ref[b], zeros], axis=0).astype(BF16)
        s = _dot_nt(q8, mk_ref[b].astype(BF16)) * scale + mask_ref[...]
        m = jnp.max(s, axis=-1, keepdims=True)
        p = jnp.exp(s - m)
        den = jnp.sum(p, axis=-1, keepdims=True)
        r = _dot(p.astype(BF16), mv_ref[b].astype(BF16)) / den
        o_ref[b] = r[0:4, :]
        return c

    lax.fori_loop(0, q_ref.shape[0], body, 0, unroll=True)


def _samp_c_call(qc, cache_k, cache_v):
    nb = qc.shape[0]
    bb = 8
    rows = N_MEM * H_C
    head_of_row = np.arange(rows)[None, :] % H_C
    mask = np.where(head_of_row == np.arange(8)[:, None], 0.0, NEG).astype(np.float32)
    qblk = pl.BlockSpec((bb, H_C, HD_C), lambda i: (i, 0, 0))
    cblk = pl.BlockSpec((bb, rows, HD_C), lambda i: (i, 0, 0))
    out = pl.pallas_call(
        _samp_c_kernel,
        grid=(nb // bb,),
        in_specs=[qblk, cblk, cblk, _const_spec((8, rows))],
        out_specs=qblk,
        out_shape=jax.ShapeDtypeStruct((nb, H_C, HD_C), F32),
        compiler_params=_cparams(("arbitrary",)),
        name="sample_mem_attn",
    )(qc.reshape(nb, H_C, HD_C), cache_k.reshape(nb, rows, HD_C), cache_v.reshape(nb, rows, HD_C),
      jnp.asarray(mask))
    return out.reshape(nb, 512)


def kernel(x_prompt, x_sample, mem_prompt, cache_a_k, cache_a_v, cache_b_k, cache_b_v, cache_mem_k,
           cache_mem_v, rel_bias, ln_g, w_in, gq_a, gk_a, gq_b, gk_b, gq_c, gk_c, sinks_b, mem_ln_g,
           w_mem_kv, w_br_a, w_br_b, w_br_c, w_out):
    batch, seq, _ = x_prompt.shape
    nsamp = x_sample.shape[0]
    assert ln_g.shape[0] == 1 and x_sample.shape[1] == 1
    assert (batch, seq, nsamp) == (8, 2048, 128) and w_in.shape == (1, D_MODEL, C_END)
    assert cache_a_k.shape == (1, nsamp, W_A, H_A, HD_A) and cache_b_k.shape == (1, nsamp, W_B, KV_B, HD_B)

    wqkv_bf = _pack_columns(w_in[0], QKV_PARTS).astype(BF16)
    wzg_bf = _pack_columns(w_in[0], OUT_PARTS).astype(BF16)
    wmem_bf = w_mem_kv[0].astype(BF16)
    wbr_bf = jnp.stack([w_br_a[0], w_br_b[0], w_br_c[0]]).astype(BF16)
    wout_bf = w_out[0].astype(BF16)
    lng = ln_g.reshape(1, D_MODEL)
    gains = jnp.stack([jnp.tile(gq_a[0], 8), jnp.tile(gk_a[0], 8), jnp.tile(gq_b[0], 8),
                       jnp.tile(gk_b[0], 8), jnp.tile(gq_c[0], 4)])
    gk_c4 = jnp.tile(gk_c[0], 4).reshape(1, 512)
    blockdiag = np.kron(np.eye(4, dtype=np.float32), np.ones((64, 64), np.float32))
    pmat = jnp.asarray(blockdiag, BF16)

    *tables, mult_a = _bucket_tables()
    bias_a, bias_b, bias_sa, bias_sb, s0a, s0b, skb = _bias_call(rel_bias, sinks_b.reshape(H_B), tables)

    xs2 = x_sample.reshape(nsamp, D_MODEL)
    sqa, ska, sva, sqb, skb_new, svb, sqc = _qkv_call(xs2, lng, wqkv_bf, gains, pmat)
    riders = _rider_arrays(sqa, ska, sva, cache_a_k[0], cache_a_v[0], bias_sa, jnp.asarray(mult_a), s0a)
    out_tm = 256
    n_qkv = batch * seq // 512
    n_dil = batch * (H_A // 2)
    n_out = batch * seq // out_tm
    n_alone = nsamp - n_qkv - n_dil - n_out
    assert n_alone >= 0

    xp2 = x_prompt.reshape(batch * seq, D_MODEL)
    qa, ka, va, qb, kb, vb, qc, ka_t, va_t, kb_t, vb_t, soa_1 = _qkv_call(
        xp2, lng, wqkv_bf, gains, pmat, (batch, seq), rider=(riders, 0))
    mk, mv = _memkv_call(mem_prompt.reshape(batch * N_MEM, D_MODEL), mem_ln_g.reshape(1, D_MODEL),
                         wmem_bf, gk_c4)
    oa, soa_2 = _dil_call(qa, ka, va, bias_a, batch, seq, rider=(riders, n_qkv))
    ob = _swa_call(sinks_b.reshape(H_B), qb, kb, vb, bias_b, batch, seq)
    oc = _mem_attn_call(qc, mk, mv, batch, seq)
    yp, soa_3 = _out_call(xp2, oa.reshape(-1, 512), ob.reshape(-1, 512), oc.reshape(-1, 512),
                          lng, wzg_bf, wbr_bf, wout_bf, tm=out_tm, rider=(riders, n_qkv + n_dil))
    yp = yp.reshape(batch, seq, D_MODEL)

    soa_parts = [soa_1, soa_2, soa_3]
    if n_alone:
        soa_parts.append(_samp_a_call(riders, n_qkv + n_dil + n_out, n_alone))
    soa = jnp.concatenate(soa_parts, axis=0).reshape(nsamp, 512)
    sob = _samp_b_call(sqb, skb_new, svb, cache_b_k[0], cache_b_v[0], bias_sb, s0b, skb)
    soc = _samp_c_call(sqc, cache_mem_k[0], cache_mem_v[0])
    ys, = _out_call(xs2, soa, sob, soc, lng, wzg_bf, wbr_bf, wout_bf)

    return (yp, ys.reshape(nsamp, 1, D_MODEL),
            jnp.transpose(ka_t.reshape(1, batch, H_A, HD_A, seq), (0, 1, 4, 2, 3)),
            jnp.transpose(va_t.reshape(1, batch, H_A, HD_A, seq), (0, 1, 4, 2, 3)),
            jnp.transpose(kb_t.reshape(1, batch, KV_B, HD_B, W_B), (0, 1, 4, 2, 3)),
            jnp.transpose(vb_t.reshape(1, batch, KV_B, HD_B, W_B), (0, 1, 4, 2, 3)),
            mk.reshape(1, batch, N_MEM, H_C, HD_C), mv.reshape(1, batch, N_MEM, H_C, HD_C),
            ska.reshape(1, nsamp, 1, H_A, HD_A), sva.reshape(1, nsamp, 1, H_A, HD_A),
            skb_new.reshape(1, nsamp, 1, KV_B, HD_B), svb.reshape(1, nsamp, 1, KV_B, HD_B))
```

```python
import functools
import math

import numpy as np
import jax
import jax.numpy as jnp
from jax import lax
from jax.experimental import pallas as pl
from jax.experimental.pallas import tpu as pltpu

F32 = jnp.float32
BF16 = jnp.bfloat16

D_MODEL = 1024
H_A, HD_A = 8, 64
A_DILATIONS = (1, 4, 16)
A_STEPS = 128
W_A = 2048
H_B, KV_B, HD_B, W_B = 8, 2, 64, 128
H_C, HD_C, N_MEM = 4, 128, 256
NUM_BUCKETS, MAX_DISTANCE = 32, 2048
EPS = 1e-6
NEG = -1e30
BLK = 128
UNROLL = 16

IN_COLS = dict(qa=(0, 512), ka=(512, 512), va=(1024, 512), za=(1536, 512),
               qb=(2048, 512), kb=(2560, 128), vb=(2688, 128), zb=(2816, 512),
               qc=(3328, 512), zc=(3840, 512), ga=(4352, 1024), gb=(5376, 1024), gc=(6400, 1024))
C_END = 7424
QKV_PARTS = ("qa", "ka", "va", "qb", "kb", "vb", "qc")
OUT_PARTS = ("za", "zb", "zc", "ga", "gb", "gc")


def _packed_offsets(parts):
    offs, at = {}, 0
    for name in parts:
        offs[name] = at
        at += IN_COLS[name][1]
    return offs, at


QKV_OFF, QKV_WIDTH = _packed_offsets(QKV_PARTS)
OUT_OFF, OUT_WIDTH = _packed_offsets(OUT_PARTS)


def _pack_columns(w, parts):
    return jnp.concatenate([w[:, IN_COLS[p][0]:IN_COLS[p][0] + IN_COLS[p][1]] for p in parts], axis=1)

VMEM_LIMIT = 56 * 1024 * 1024


def _cparams(sem):
    return pltpu.CompilerParams(dimension_semantics=sem, vmem_limit_bytes=VMEM_LIMIT)


def _const_spec(shape):
    nd = len(shape)
    return pl.BlockSpec(shape, lambda *_: (0,) * nd, pipeline_mode=pl.Buffered(1))


def _t5_bucket_np(dist):
    n = np.maximum(dist, 0)
    max_exact = NUM_BUCKETS // 2
    nf = np.maximum(n, 1).astype(np.float32)
    large = max_exact + (np.log(nf / np.float32(max_exact))
                         / np.float32(math.log(MAX_DISTANCE / max_exact))
                         * np.float32(NUM_BUCKETS - max_exact)).astype(np.int32)
    return np.where(n < max_exact, n, np.minimum(large, NUM_BUCKETS - 1)).astype(np.int32)


def _a_multiplicity(dist):
    return sum(((dist % d == 0) & (dist <= A_STEPS * d)).astype(np.int32) for d in A_DILATIONS)


def _bucket_tables():
    i = np.arange(BLK)[:, None]
    j = np.arange(2 * BLK)[None, :]
    rel = i + BLK - j
    bkt_a = np.stack([np.where((rel >= 0) & (rel <= A_STEPS), _t5_bucket_np(rel * d), -1)
                      for d in A_DILATIONS]).astype(np.int32)
    bkt_b = np.where((rel >= 0) & (rel < W_B), _t5_bucket_np(rel), -1).astype(np.int32)
    dist_a = W_A - np.arange(W_A)
    mult_a = _a_multiplicity(dist_a)
    sb_a = np.broadcast_to(np.where(mult_a > 0, _t5_bucket_np(dist_a), -1), (8, W_A)).astype(np.int32)
    dist_b = W_B - np.arange(W_B)
    sb_b = np.broadcast_to(np.where(dist_b < W_B, _t5_bucket_np(dist_b), -1), (8, W_B)).astype(np.int32)
    return bkt_a, bkt_b, sb_a, sb_b, mult_a.astype(np.float32).reshape(1, W_A)


def _rms_rows(x, g):
    return x * lax.rsqrt(jnp.mean(x * x, axis=-1, keepdims=True) + EPS) * g


def _dot(a, b):
    return jnp.dot(a, b, preferred_element_type=F32)


def _dot_nt(a, b):
    return lax.dot_general(a, b, (((1,), (1,)), ((), ())), preferred_element_type=F32)


def _headnorm64(t, g, pmat):
    sq = (t * t).astype(BF16)
    width = t.shape[1]
    step = min(width, 256)
    pp = pmat[:step, :step]
    parts = [_dot(sq[:, c:c + step], pp) for c in range(0, width, step)]
    ss = parts[0] if len(parts) == 1 else jnp.concatenate(parts, axis=1)
    return t * lax.rsqrt(ss * (1.0 / 64.0) + EPS) * g


def _headnorm128(t, g):
    parts = []
    for c in range(0, t.shape[1], 128):
        seg = t[:, c:c + 128]
        parts.append(seg * lax.rsqrt(jnp.mean(seg * seg, axis=-1, keepdims=True) + EPS))
    return jnp.concatenate(parts, axis=1) * g


def _bias_kernel(tbl_ref, sink_ref, bkta_ref, bktb_ref, sba_ref, sbb_ref,
                 ba_ref, bb_ref, oa_ref, ob_ref, s0a_ref, s0b_ref, sk_ref, *, present):
    pa, pb, psa, psb = present
    for g in range(3):
        bk = bkta_ref[g]
        for h in range(H_A):
            ba_ref[g, h] = jnp.full((BLK, 2 * BLK), NEG, F32)
        for b in pa[g]:
            hit = bk == b
            for h in range(H_A):
                ba_ref[g, h] = jnp.where(hit, tbl_ref[b, h], ba_ref[g, h])
    bk = bktb_ref[...]
    for h in range(H_B):
        bb_ref[h] = jnp.full((BLK, 2 * BLK), NEG, F32)
    for b in pb:
        hit = bk == b
        for h in range(H_B):
            bb_ref[h] = jnp.where(hit, tbl_ref[b, H_A + h], bb_ref[h])
    for (src, dst, buckets, col0) in ((sba_ref, oa_ref, psa, 0), (sbb_ref, ob_ref, psb, H_A)):
        bk = src[...]
        row = lax.broadcasted_iota(jnp.int32, bk.shape, 0)
        acc = jnp.full(bk.shape, NEG, F32)
        for b in buckets:
            hit = bk == b
            for h in range(8):
                acc = jnp.where(hit & (row == h), tbl_ref[b, col0 + h], acc)
        dst[...] = acc
    row8 = lax.broadcasted_iota(jnp.int32, (8, 128), 0)
    a0 = jnp.zeros((8, 128), F32)
    b0 = jnp.zeros((8, 128), F32)
    sk = jnp.zeros((8, 128), F32)
    for h in range(8):
        a0 = jnp.where(row8 == h, tbl_ref[0, h], a0)
        b0 = jnp.where(row8 == h, tbl_ref[0, H_A + h], b0)
        sk = jnp.where(row8 == h, sink_ref[h], sk)
    s0a_ref[...] = a0
    s0b_ref[...] = b0
    sk_ref[...] = sk


def _bias_call(rel_bias, sinks, tables):
    bkt_a, bkt_b, sb_a, sb_b = tables
    uniq = lambda a: tuple(int(b) for b in np.unique(a) if b >= 0)
    present = (tuple(uniq(bkt_a[g]) for g in range(3)), uniq(bkt_b), uniq(sb_a), uniq(sb_b))
    smem = pl.BlockSpec(memory_space=pltpu.SMEM)
    vmem = pl.BlockSpec(memory_space=pltpu.VMEM)
    return pl.pallas_call(
        functools.partial(_bias_kernel, present=present),
        in_specs=[smem, smem, vmem, vmem, vmem, vmem],
        out_specs=[vmem] * 7,
        out_shape=[jax.ShapeDtypeStruct((3, H_A, BLK, 2 * BLK), F32),
                   jax.ShapeDtypeStruct((H_B, BLK, 2 * BLK), F32),
                   jax.ShapeDtypeStruct((8, W_A), F32),
                   jax.ShapeDtypeStruct((8, W_B), F32),
                   jax.ShapeDtypeStruct((8, 128), F32),
                   jax.ShapeDtypeStruct((8, 128), F32),
                   jax.ShapeDtypeStruct((8, 128), F32)],
        compiler_params=pltpu.CompilerParams(vmem_limit_bytes=VMEM_LIMIT),
        name="bias_expand",
    )(rel_bias, sinks, jnp.asarray(bkt_a), jnp.asarray(bkt_b), jnp.asarray(sb_a), jnp.asarray(sb_b))


def _qkv_kernel(x_ref, lng_ref, w_ref, g_ref, p_ref, *refs, rider=False, tiles_per_seq=1):
    if rider:
        _ride(refs[:N_RIDER_IN], refs[-1])
        refs = refs[N_RIDER_IN:-1]
    qa_ref, ka_ref, va_ref, qb_ref, kb_ref, vb_ref, qc_ref, *cache_refs = refs
    h = _rms_rows(x_ref[...], lng_ref[...]).astype(BF16)
    pmat = p_ref[...]

    def proj(part):
        c0 = QKV_OFF[part]
        return _dot(h, w_ref[:, c0:c0 + IN_COLS[part][1]])

    qa_ref[...] = _headnorm64(proj("qa"), g_ref[0:1, :], pmat)
    ka = _headnorm64(proj("ka"), g_ref[1:2, :], pmat)
    va = proj("va")
    ka_ref[...] = ka
    va_ref[...] = va
    kb = _headnorm64(proj("kb"), g_ref[3:4, 0:128], pmat)
    vb = proj("vb")
    if cache_refs:
        kt_ref, vt_ref, kbt_ref, vbt_ref = cache_refs
        kt_ref[...] = ka.T
        vt_ref[...] = va.T

        @pl.when((pl.program_id(0) + 1) % tiles_per_seq == 0)
        def _():
            tm = kb.shape[0]
            kbt_ref[...] = kb[tm - W_B:, :].T
            vbt_ref[...] = vb[tm - W_B:, :].T
    qb_ref[...] = _headnorm64(proj("qb"), g_ref[2:3, :], pmat).astype(qb_ref.dtype)
    kb_ref[...] = kb
    vb_ref[...] = vb
    qc_ref[...] = _headnorm128(proj("qc"), g_ref[4:5, :]).astype(qc_ref.dtype)


def _qkv_call(x2, lng, w_bf, gains, pmat, cache_layout_for=None, rider=None):
    n = x2.shape[0]
    tm = min(512, n)
    steps = n // tm
    row = lambda w: pl.BlockSpec((tm, w), lambda i: (i, 0))
    widths = (512, 512, 512, 512, 128, 128, 512)
    in_specs = [row(D_MODEL), _const_spec((1, D_MODEL)), _const_spec((D_MODEL, QKV_WIDTH)),
                _const_spec((5, 512)), _const_spec((256, 256))]
    args = [x2, lng, w_bf, gains, pmat]
    out_specs = [row(w) for w in widths]
    q_dtype = F32 if cache_layout_for is None else BF16
    out_shape = [jax.ShapeDtypeStruct((n, w), q_dtype if i in (3, 6) else F32) for i, w in enumerate(widths)]
    per = 1
    if cache_layout_for is not None:
        batch, seq = cache_layout_for
        per = seq // tm
        out_specs += [pl.BlockSpec((None, 512, tm), lambda i: (i // per, 0, i % per))] * 2
        out_shape += [jax.ShapeDtypeStruct((batch, 512, seq), F32)] * 2
        out_specs += [pl.BlockSpec((None, KV_B * HD_B, W_B), lambda i: (i // per, 0, 0))] * 2
        out_shape += [jax.ShapeDtypeStruct((batch, KV_B * HD_B, W_B), F32)] * 2
    if rider is not None:
        r_in, r_out = _rider_specs(rider[1], lambda i: i)
        in_specs += r_in
        args += list(rider[0])
        out_specs.append(r_out)
        out_shape.append(jax.ShapeDtypeStruct((steps, 1, 512), F32))
    return pl.pallas_call(
        functools.partial(_qkv_kernel, rider=rider is not None, tiles_per_seq=per),
        grid=(steps,),
        in_specs=in_specs,
        out_specs=out_specs,
        out_shape=out_shape,
        compiler_params=_cparams(("arbitrary",)),
        name="qkv_proj",
    )(*args)


def _memkv_kernel(m_ref, g_ref, w_ref, gk_ref, mk_ref, mv_ref):
    h = _rms_rows(m_ref[...], g_ref[...]).astype(BF16)
    mk_ref[...] = _headnorm128(_dot(h, w_ref[:, 0:512]), gk_ref[...])
    mv_ref[...] = _dot(h, w_ref[:, 512:1024])


def _memkv_call(mem2, g, w_bf, gk):
    n = mem2.shape[0]
    tm = 512
    row = lambda w: pl.BlockSpec((tm, w), lambda i: (i, 0))
    return pl.pallas_call(
        _memkv_kernel,
        grid=(n // tm,),
        in_specs=[row(D_MODEL), _const_spec((1, D_MODEL)), _const_spec((D_MODEL, 2 * 512)),
                  _const_spec((1, 512))],
        out_specs=[row(512), row(512)],
        out_shape=[jax.ShapeDtypeStruct((n, 512), F32)] * 2,
        compiler_params=_cparams(("arbitrary",)),
        name="mem_kv_proj",
    )(mem2, g, w_bf, gk)


def _out_kernel(x_ref, oa_ref, ob_ref, oc_ref, lng_ref, w_ref, wbr_ref, wout_ref, *refs):
    if len(refs) > 1:
        _ride(refs[:N_RIDER_IN], refs[-1])
    y_ref = refs[-2] if len(refs) > 1 else refs[0]
    x = x_ref[...]
    h = _rms_rows(x, lng_ref[...]).astype(BF16)
    acc = None
    for br, (o_ref, zpart, gpart) in enumerate(((oa_ref, "za", "ga"), (ob_ref, "zb", "gb"), (oc_ref, "zc", "gc"))):
        cz, cg = OUT_OFF[zpart], OUT_OFF[gpart]
        z = _dot(h, w_ref[:, cz:cz + 512])
        y = (o_ref[...] * (z * jax.nn.sigmoid(z))).astype(BF16)
        yb = _dot(y, wbr_ref[br])
        gate = jax.nn.sigmoid(_dot(h, w_ref[:, cg:cg + D_MODEL]))
        acc = gate * yb if acc is None else acc + gate * yb
    y_ref[...] = x + _dot(acc.astype(BF16), wout_ref[...])


def _out_call(x2, oa, ob, oc, lng, w_bf, wbr_bf, wout_bf, tm=512, rider=None):
    n = x2.shape[0]
    tm = min(tm, n)
    steps = n // tm
    row = lambda w: pl.BlockSpec((tm, w), lambda i: (i, 0))
    in_specs = [row(D_MODEL), row(512), row(512), row(512), _const_spec((1, D_MODEL)),
                _const_spec((D_MODEL, OUT_WIDTH)), _const_spec((3, 512, D_MODEL)), _const_spec((D_MODEL, D_MODEL))]
    args = [x2, oa, ob, oc, lng, w_bf, wbr_bf, wout_bf]
    out_specs = [row(D_MODEL)]
    out_shape = [jax.ShapeDtypeStruct((n, D_MODEL), F32)]
    if rider is not None:
        r_in, r_out = _rider_specs(rider[1], lambda i: i)
        in_specs += r_in
        args += list(rider[0])
        out_specs.append(r_out)
        out_shape.append(jax.ShapeDtypeStruct((steps, 1, 512), F32))
    return pl.pallas_call(
        _out_kernel,
        grid=(steps,),
        in_specs=in_specs,
        out_specs=out_specs,
        out_shape=out_shape,
        compiler_params=_cparams(("arbitrary",)),
        name="gated_out",
    )(*args)


PAD = BLK
CLS_PITCH = BLK + 8


def _low_lanes(rows):
    return lax.broadcasted_iota(jnp.int32, (rows, 128), 1) < 64


def _aligned(start):
    return start if isinstance(start, int) else pl.multiple_of(start, BLK)


def _pair_tile(q, k0, k1, v0, v1, b0, b1, sinks=None):
    nk = k0.shape[0]
    low = _low_lanes(BLK)
    s = _dot_nt(q, jnp.concatenate([k0, k1], axis=0)) + jnp.concatenate([b0, b1], axis=1)
    ms = [jnp.max(s[:, hh * nk:(hh + 1) * nk], axis=-1, keepdims=True) for hh in range(2)]
    if sinks is not None:
        ms = [jnp.maximum(ms[hh], sinks[hh]) for hh in range(2)]
    p = jnp.concatenate([jnp.exp(s[:, hh * nk:(hh + 1) * nk] - ms[hh]) for hh in range(2)], axis=1).astype(BF16)
    own = _low_lanes(nk)
    ones = [jnp.where(own, 1.0, 0.0).astype(BF16), jnp.where(own, 0.0, 1.0).astype(BF16)]
    vblk = jnp.concatenate([jnp.concatenate([v0, ones[0]], axis=1),
                            jnp.concatenate([v1, ones[1]], axis=1)], axis=0)
    up = _dot(p, vblk)
    return up[:, :128], jnp.where(low, ms[0], ms[1]), up[:, 128:]


def _first_bias(bias, also=None):
    gone = lax.broadcasted_iota(jnp.int32, bias.shape, 1) < BLK
    if also is not None:
        gone = gone & also
    return jnp.where(gone, NEG, bias)


def _dil_kernel(q_ref, k_ref, v_ref, bias_ref, *refs, rider=False):
    if rider:
        _ride(refs[:N_RIDER_IN], refs[N_RIDER_IN + 1])
        refs = refs[N_RIDER_IN:N_RIDER_IN + 1] + refs[N_RIDER_IN + 2:]
    o_ref, qs_ref, kc_ref, vc_ref, c4_ref, ru_ref, rm_ref, rl_ref = refs
    scale = HD_A ** -0.5
    seq = q_ref.shape[0]
    ntile = seq // BLK
    low = _low_lanes(BLK)
    d1, d2 = A_DILATIONS[1], A_DILATIONS[2]

    def src_rows(g, t):
        d = A_DILATIONS[g]
        if d == 1:
            return pl.ds(_aligned(t * BLK), BLK)
        nblk = ntile // d
        r, n = t >> (nblk.bit_length() - 1), t & (nblk - 1)
        return pl.ds(r + n * (d * BLK), BLK, stride=d)

    zero = jnp.zeros((PAD, 128), BF16)
    stagers = []
    for g in range(3):
        for hh in range(2):
            kc_ref[g, hh, 0:PAD, :] = zero
            vc_ref[g, hh, 0:PAD, :] = zero

        def stage(t, c, g=g):
            dst = pl.ds(_aligned(t * BLK), BLK)
            dstp = pl.ds(_aligned(PAD + t * BLK), BLK)
            if g < 2:
                src = src_rows(g, t)
                q, k, v = q_ref[src, :], k_ref[src, :], v_ref[src, :]
            else:
                src = pl.ds((t & (d1 - 1)) * (seq // d1) + (t >> (d1.bit_length() - 1)), BLK, stride=d2 // d1)
                q, k, v = c4_ref[0, src, :], c4_ref[1, src, :], c4_ref[2, src, :]
            if g == 1:
                c4_ref[0, dst, :], c4_ref[1, dst, :], c4_ref[2, dst, :] = q, k, v
            qs_ref[g, dst, :] = (q * scale).astype(BF16)
            kc_ref[g, 0, dstp, :] = jnp.where(low, k, 0.0).astype(BF16)
            kc_ref[g, 1, dstp, :] = jnp.where(low, 0.0, k).astype(BF16)
            vc_ref[g, 0, dstp, :] = jnp.where(low, v, 0.0).astype(BF16)
            vc_ref[g, 1, dstp, :] = jnp.where(low, 0.0, v).astype(BF16)
            return c

        stagers.append(stage)

    def tiles(g, i):
        nblk = ntile // A_DILATIONS[g]
        res = []
        for j in range(UNROLL):
            t = i * UNROLL + j
            qrows = pl.ds(_aligned(t * BLK), BLK)
            krows = pl.ds(_aligned(t * BLK), 2 * BLK)
            bias = [bias_ref[g, hh] for hh in range(2)]
            if nblk <= UNROLL and j % nblk == 0:
                bias = [_first_bias(b) for b in bias]
            elif nblk > UNROLL and j == 0:
                bias = [_first_bias(b, i == 0) for b in bias]
            res.append(_pair_tile(qs_ref[g, qrows, :], kc_ref[g, 0, krows, :], kc_ref[g, 1, krows, :],
                                  vc_ref[g, 0, krows, :], vc_ref[g, 1, krows, :], bias[0], bias[1]))
        return res

    assert UNROLL == ntile
    for g in range(3):
        for t in range(ntile):
            stagers[g](t, 0)
        res = tiles(g, 0)
        for t in range(ntile):
            rows = pl.ds(t * CLS_PITCH, BLK) if g == 2 else src_rows(g, t)
            ru_ref[g, rows, :], rm_ref[g, rows, :], rl_ref[g, rows, :] = res[t]

    def natural_rows(ref, i):
        parts = [ref[2, pl.ds((v % 2) * 8 * CLS_PITCH + i * (BLK // d2) + v // 2, 8, stride=CLS_PITCH), :]
                 for v in range(BLK // 8)]
        return jnp.concatenate(parts, axis=0)

    def combine(i, c):
        rows = pl.ds(_aligned(i * BLK), BLK)
        ms = [rm_ref[0, rows, :], rm_ref[1, rows, :], natural_rows(rm_ref, i)]
        mx = jnp.maximum(jnp.maximum(ms[0], ms[1]), ms[2])
        ws = [jnp.exp(m - mx) for m in ms]
        num = ws[0] * ru_ref[0, rows, :] + ws[1] * ru_ref[1, rows, :] + ws[2] * natural_rows(ru_ref, i)
        den = ws[0] * rl_ref[0, rows, :] + ws[1] * rl_ref[1, rows, :] + ws[2] * natural_rows(rl_ref, i)
        o_ref[rows, :] = num / den
        return c

    lax.fori_loop(0, ntile, combine, 0, unroll=2)


def _dil_call(qa, ka, va, bias_a, batch, seq, rider=None):
    assert seq == A_DILATIONS[2] * BLK and UNROLL % (seq // (A_DILATIONS[1] * BLK)) == 0
    blk = pl.BlockSpec((None, seq, 128), lambda b, p: (b, 0, p))
    pairs = H_A // 2
    in_specs = [blk, blk, blk, pl.BlockSpec((3, 2, BLK, 2 * BLK), lambda b, p: (0, p, 0, 0))]
    args = [qa.reshape(batch, seq, 512), ka.reshape(batch, seq, 512), va.reshape(batch, seq, 512), bias_a]
    out_specs = [blk]
    out_shape = [jax.ShapeDtypeStruct((batch, seq, 512), F32)]
    if rider is not None:
        r_in, r_out = _rider_specs(rider[1], lambda b, p: b * pairs + p)
        in_specs += r_in
        args += list(rider[0])
        out_specs.append(r_out)
        out_shape.append(jax.ShapeDtypeStruct((batch * pairs, 1, 512), F32))
    return pl.pallas_call(
        functools.partial(_dil_kernel, rider=rider is not None),
        grid=(batch, pairs),
        in_specs=in_specs,
        out_specs=out_specs,
        out_shape=out_shape,
        scratch_shapes=[pltpu.VMEM((3, seq, 128), BF16), pltpu.VMEM((3, 2, PAD + seq, 128), BF16),
                        pltpu.VMEM((3, 2, PAD + seq, 128), BF16), pltpu.VMEM((3, seq, 128), F32)]
                       + [pltpu.VMEM((3, A_DILATIONS[2] * CLS_PITCH, 128), F32)] * 3,
        compiler_params=_cparams(("arbitrary", "arbitrary")),
        name="dilated_attn",
    )(*args)


def _swa_kernel(sink_ref, q_ref, k_ref, v_ref, bias_ref, o_ref, qs_ref, kc_ref, vc_ref):
    scale = HD_B ** -0.5
    seq = q_ref.shape[0]
    ntile = seq // BLK
    p = pl.program_id(1)
    kv = p // 2
    sinks = (sink_ref[2 * p], sink_ref[2 * p + 1])
    low = _low_lanes(BLK)
    own_half = (lax.broadcasted_iota(jnp.int32, (BLK, 128), 1) >= 64).astype(jnp.int32) == kv

    def dup(xf):
        return jnp.where(own_half, xf, pltpu.roll(xf, 64, 1))

    zero = jnp.zeros((PAD, 128), BF16)
    for hh in range(2):
        kc_ref[hh, 0:PAD, :] = zero
        vc_ref[hh, 0:PAD, :] = zero

    def stage(t, c):
        src = pl.ds(pl.multiple_of(t * BLK, BLK), BLK)
        dstp = pl.ds(pl.multiple_of(PAD + t * BLK, BLK), BLK)
        qs_ref[src, :] = (q_ref[src, :] * scale).astype(BF16)
        k = dup(k_ref[src, :])
        v = dup(v_ref[src, :])
        kc_ref[0, dstp, :] = jnp.where(low, k, 0.0).astype(BF16)
        kc_ref[1, dstp, :] = jnp.where(low, 0.0, k).astype(BF16)
        vc_ref[0, dstp, :] = jnp.where(low, v, 0.0).astype(BF16)
        vc_ref[1, dstp, :] = jnp.where(low, 0.0, v).astype(BF16)
        return c

    lax.fori_loop(0, ntile, stage, 0, unroll=4)

    def body(i, c):
        res = []
        for j in range(UNROLL):
            t = i * UNROLL + j
            qrows = pl.ds(pl.multiple_of(t * BLK, BLK), BLK)
            krows = pl.ds(pl.multiple_of(t * BLK, BLK), 2 * BLK)
            bias = [bias_ref[hh] for hh in range(2)]
            if j == 0:
                bias = [_first_bias(b, i == 0) for b in bias]
            u, m, l = _pair_tile(qs_ref[qrows, :], kc_ref[0, krows, :], kc_ref[1, krows, :],
                                 vc_ref[0, krows, :], vc_ref[1, krows, :], bias[0], bias[1], sinks)
            sk = jnp.where(low, sinks[0], sinks[1])
            res.append(u / (l + jnp.exp(sk - m)))
        for j in range(UNROLL):
            o_ref[pl.ds(pl.multiple_of((i * UNROLL + j) * BLK, BLK), BLK), :] = res[j]
        return c

    lax.fori_loop(0, ntile // UNROLL, body, 0)


def _swa_call(sinks, qb, kb, vb, bias_b, batch, seq):
    qblk = pl.BlockSpec((None, seq, 128), lambda b, p: (b, 0, p))
    kblk = pl.BlockSpec((None, seq, 128), lambda b, p: (b, 0, 0))
    return pl.pallas_call(
        _swa_kernel,
        grid=(batch, H_B // 2),
        in_specs=[pl.BlockSpec(memory_space=pltpu.SMEM), qblk, kblk, kblk,
                  pl.BlockSpec((2, BLK, 2 * BLK), lambda b, p: (p, 0, 0))],
        out_specs=qblk,
        out_shape=jax.ShapeDtypeStruct((batch, seq, 512), F32),
        scratch_shapes=[pltpu.VMEM((seq, 128), BF16), pltpu.VMEM((2, PAD + seq, 128), BF16),
                        pltpu.VMEM((2, PAD + seq, 128), BF16)],
        compiler_params=_cparams(("arbitrary", "arbitrary")),
        name="swa_attn",
    )(sinks, qb.reshape(batch, seq, 512), kb.reshape(batch, seq, 128), vb.reshape(batch, seq, 128), bias_b)


def _mem_attn_kernel(q_ref, mk_ref, mv_ref, o_ref):
    scale = HD_C ** -0.5
    ones = jnp.ones((N_MEM, HD_C), BF16)
    for h in range(H_C):
        cs = slice(h * HD_C, (h + 1) * HD_C)
        s = _dot_nt(q_ref[:, cs].astype(BF16), mk_ref[:, cs].astype(BF16)) * scale
        p = jnp.exp(s - jnp.max(s, axis=-1, keepdims=True)).astype(BF16)
        up = _dot(p, jnp.concatenate([mv_ref[:, cs].astype(BF16), ones], axis=1))
        o_ref[:, cs] = up[:, :HD_C] / up[:, HD_C:]


def _mem_attn_call(qc, mk, mv, batch, seq):
    tq = 512
    return pl.pallas_call(
        _mem_attn_kernel,
        grid=(batch, seq // tq),
        in_specs=[pl.BlockSpec((None, tq, 512), lambda b, i: (b, i, 0)),
                  pl.BlockSpec((None, N_MEM, 512), lambda b, i: (b, 0, 0)),
                  pl.BlockSpec((None, N_MEM, 512), lambda b, i: (b, 0, 0))],
        out_specs=pl.BlockSpec((None, tq, 512), lambda b, i: (b, i, 0)),
        out_shape=jax.ShapeDtypeStruct((batch, seq, 512), F32),
        compiler_params=_cparams(("arbitrary", "arbitrary")),
        name="mem_attn",
    )(qc.reshape(batch, seq, 512), mk.reshape(batch, N_MEM, 512), mv.reshape(batch, N_MEM, 512))


def _head_rows(row, width, head_dim):
    rid = lax.broadcasted_iota(jnp.int32, (8, width), 0)
    hid = lax.broadcasted_iota(jnp.int32, (8, width), 1) // head_dim
    return rid == hid, jnp.where(rid == hid, jnp.broadcast_to(row, (8, width)), 0.0)


def _samp_a_one(q, kn, vn, kt_ref, vt_ref, bias_ref, mult_ref, b0_ref):
    scale = HD_A ** -0.5
    nt = W_A // 128
    own, qh = _head_rows(q, 512, HD_A)
    qcols = jnp.transpose(jnp.broadcast_to(q, (128, 512)))
    s_heads = []
    for h in range(H_A):
        rows = slice(h * HD_A, (h + 1) * HD_A)
        qb = qcols[rows, :]
        s_heads.append(jnp.concatenate(
            [jnp.sum(kt_ref[rows, 128 * j:128 * (j + 1)] * qb, axis=0, keepdims=True) for j in range(nt)], axis=1))
    s = jnp.concatenate(s_heads, axis=0) * scale + bias_ref[...]
    ln = jnp.sum(qh * kn, axis=-1, keepdims=True) * scale + b0_ref[:, 0:1]
    m = jnp.maximum(jnp.max(s, axis=-1, keepdims=True), ln)
    p = jnp.exp(s - m) * mult_ref[...]
    pn = len(A_DILATIONS) * jnp.exp(ln - m)
    den = jnp.sum(p, axis=-1, keepdims=True) + pn
    accs = []
    for h in range(H_A):
        rows = slice(h * HD_A, (h + 1) * HD_A)
        acc = vt_ref[rows, 0:128] * p[h:h + 1, 0:128]
        for j in range(1, nt):
            acc = acc + vt_ref[rows, 128 * j:128 * (j + 1)] * p[h:h + 1, 128 * j:128 * (j + 1)]
        accs.append(acc)
    pv = jnp.sum(jnp.transpose(jnp.concatenate(accs, axis=0)), axis=0, keepdims=True)
    lanes = lambda col: jnp.sum(jnp.where(own, col, 0.0), axis=0, keepdims=True)
    return (pv + lanes(pn) * vn) / lanes(den)


N_RIDER_IN = 8


def _rider_arrays(qa, kn, vn, cache_k, cache_v, bias_sa, mult, s0a):
    nb = qa.shape[0]
    kt = jnp.transpose(cache_k, (0, 2, 3, 1)).reshape(nb, 512, W_A)
    vt = jnp.transpose(cache_v, (0, 2, 3, 1)).reshape(nb, 512, W_A)
    return (qa.reshape(nb, 1, 512), kn.reshape(nb, 1, 512), vn.reshape(nb, 1, 512), kt, vt, bias_sa, mult, s0a)


def _rider_specs(first, step_of):
    row = pl.BlockSpec((None, 1, 512), lambda *ids: (first + step_of(*ids), 0, 0))
    cblk = pl.BlockSpec((None, 512, W_A), lambda *ids: (first + step_of(*ids), 0, 0))
    ins = [row, row, row, cblk, cblk, _const_spec((8, W_A)), _const_spec((1, W_A)), _const_spec((8, 128))]
    return ins, pl.BlockSpec((None, 1, 512), lambda *ids: (step_of(*ids), 0, 0))


def _ride(in_refs, o_ref):
    q_ref, kn_ref, vn_ref, kt_ref, vt_ref, bias_ref, mult_ref, b0_ref = in_refs
    o_ref[...] = _samp_a_one(q_ref[...], kn_ref[...], vn_ref[...], kt_ref, vt_ref, bias_ref, mult_ref, b0_ref)


def _samp_a_kernel(*refs):
    _ride(refs[:N_RIDER_IN], refs[-1])


def _samp_a_call(arrays, first, count):
    r_in, r_out = _rider_specs(first, lambda i: i)
    return pl.pallas_call(
        _samp_a_kernel,
        grid=(count,),
        in_specs=r_in,
        out_specs=r_out,
        out_shape=jax.ShapeDtypeStruct((count, 1, 512), F32),
        compiler_params=_cparams(("arbitrary",)),
        name="sample_dilated_attn",
    )(*arrays)


def _samp_b_kernel(q_ref, kn_ref, vn_ref, kt_ref, vt_ref, bias_ref, b0_ref, sk_ref, o_ref):
    scale = HD_B ** -0.5
    low = lax.broadcasted_iota(jnp.int32, (1, 128), 1) < 64
    rid = lax.broadcasted_iota(jnp.int32, (8, 128), 0)
    kv_half = (lax.broadcasted_iota(jnp.int32, (8, 128), 1) >= 64) == (rid >= 4)

    bb = q_ref.shape[0]
    zero = jnp.zeros((8, 128), F32)
    qhs = []
    for b in range(bb):
        q = q_ref[b:b + 1, :]
        rows = []
        for h in range(H_B):
            chunk = q[:, 128 * (h // 2):128 * (h // 2) + 128]
            if (h % 2) != (h // 4):
                chunk = pltpu.roll(chunk, 64, 1)
            rows.append(chunk)
        qhs.append(jnp.where(kv_half, jnp.concatenate(rows, axis=0), 0.0))
    qblk = jnp.concatenate([jnp.concatenate([qhs[b] if c == b else zero for c in range(bb)], axis=1)
                            for b in range(bb)], axis=0)
    kt = kt_ref[...].reshape(bb * 128, W_B).astype(BF16)
    vt = vt_ref[...].reshape(bb * 128, W_B).astype(BF16)
    tile8 = lambda ref: jnp.concatenate([ref[...]] * bb, axis=0)
    s = _dot(qblk.astype(BF16), kt) * scale + tile8(bias_ref)
    knew = jnp.concatenate([jnp.broadcast_to(kn_ref[b:b + 1, :], (8, 128)) for b in range(bb)], axis=0)
    vnew = jnp.concatenate([jnp.broadcast_to(vn_ref[b:b + 1, :], (8, 128)) for b in range(bb)], axis=0)
    qall = jnp.concatenate(qhs, axis=0)
    ln = jnp.sum(qall * knew, axis=-1, keepdims=True) * scale + tile8(b0_ref)[:, 0:1]
    sk = tile8(sk_ref)[:, 0:1]
    m = jnp.maximum(jnp.maximum(jnp.max(s, axis=-1, keepdims=True), ln), sk)
    p = jnp.exp(s - m)
    pn = jnp.exp(ln - m)
    den = jnp.sum(p, axis=-1, keepdims=True) + pn + jnp.exp(sk - m)
    rall = _dot_nt(p.astype(BF16), vt)
    for b in range(bb):
        r = (rall[8 * b:8 * b + 8, 128 * b:128 * b + 128] + pn[8 * b:8 * b + 8] * vnew[8 * b:8 * b + 8])
        r = jnp.where(kv_half, r / den[8 * b:8 * b + 8], 0.0)
        chunks = []
        for c2 in range(4):
            pair = []
            for h in (2 * c2, 2 * c2 + 1):
                piece = r[h:h + 1, :]
                if (h % 2) != (h // 4):
                    piece = pltpu.roll(piece, 64, 1)
                pair.append(piece)
            chunks.append(jnp.where(low, pair[0], pair[1]))
        o_ref[b:b + 1, :] = jnp.concatenate(chunks, axis=1)


def _samp_b_call(qb, kn, vn, cache_k, cache_v, bias_sb, s0b, skb):
    nb = qb.shape[0]
    bb = 8
    kt = jnp.transpose(cache_k, (0, 2, 3, 1)).reshape(nb, 128, W_B)
    vt = jnp.transpose(cache_v, (0, 2, 3, 1)).reshape(nb, 128, W_B)
    row = lambda w: pl.BlockSpec((bb, w), lambda i: (i, 0))
    cblk = pl.BlockSpec((bb, 128, W_B), lambda i: (i, 0, 0))
    return pl.pallas_call(
        _samp_b_kernel,
        grid=(nb // bb,),
        in_specs=[row(512), row(128), row(128), cblk, cblk, _const_spec((8, W_B)),
                  _const_spec((8, 128)), _const_spec((8, 128))],
        out_specs=row(512),
        out_shape=jax.ShapeDtypeStruct((nb, 512), F32),
        compiler_params=_cparams(("arbitrary",)),
        name="sample_swa_attn",
    )(qb, kn, vn, kt, vt, bias_sb, s0b, skb)


def _samp_c_kernel(q_ref, mk_ref, mv_ref, mask_ref, o_ref):
    scale = HD_C ** -0.5
    bb = q_ref.shape[0]
    zero = jnp.zeros((H_C, HD_C), F32)
    qblk = jnp.concatenate([jnp.concatenate([q_ref[b] if c == b else zero for c in range(bb)], axis=1)
                            for b in range(bb)], axis=0)
    kcat = jnp.concatenate([mk_ref[b] for b in range(bb)], axis=1).astype(BF16)
    vcat = jnp.concatenate([mv_ref[b] for b in range(bb)], axis=1).astype(BF16)
    s = _dot_nt(qblk.astype(BF16), kcat) * scale + mask_ref[...]
    p = jnp.exp(s - jnp.max(s, axis=-1, keepdims=True))
    den = jnp.sum(p, axis=-1, keepdims=True)
    rall = _dot(p.astype(BF16), vcat)
    for b in range(bb):
        o_ref[b] = rall[H_C * b:H_C * (b + 1), HD_C * b:HD_C * (b + 1)] / den[H_C * b:H_C * (b + 1)]


def _samp_c_call(qc, cache_k, cache_v):
    nb = qc.shape[0]
    bb = 8
    rows = N_MEM * H_C
    head_of_row = np.arange(rows)[None, :] % H_C
    head_of_query = np.arange(bb * H_C)[:, None] % H_C
    mask = np.where(head_of_row == head_of_query, 0.0, NEG).astype(np.float32)
    qblk = pl.BlockSpec((bb, H_C, HD_C), lambda i: (i, 0, 0))
    cblk = pl.BlockSpec((bb, rows, HD_C), lambda i: (i, 0, 0))
    out = pl.pallas_call(
        _samp_c_kernel,
        grid=(nb // bb,),
        in_specs=[qblk, cblk, cblk, _const_spec((bb * H_C, rows))],
        out_specs=qblk,
        out_shape=jax.ShapeDtypeStruct((nb, H_C, HD_C), F32),
        compiler_params=_cparams(("arbitrary",)),
        name="sample_mem_attn",
    )(qc.reshape(nb, H_C, HD_C), cache_k.reshape(nb, rows, HD_C), cache_v.reshape(nb, rows, HD_C),
      jnp.asarray(mask))
    return out.reshape(nb, 512)


def kernel(x_prompt, x_sample, mem_prompt, cache_a_k, cache_a_v, cache_b_k, cache_b_v, cache_mem_k,
           cache_mem_v, rel_bias, ln_g, w_in, gq_a, gk_a, gq_b, gk_b, gq_c, gk_c, sinks_b, mem_ln_g,
           w_mem_kv, w_br_a, w_br_b, w_br_c, w_out):
    batch, seq, _ = x_prompt.shape
    nsamp = x_sample.shape[0]
    assert ln_g.shape[0] == 1 and x_sample.shape[1] == 1
    assert (batch, seq, nsamp) == (8, 2048, 128) and w_in.shape == (1, D_MODEL, C_END)
    assert cache_a_k.shape == (1, nsamp, W_A, H_A, HD_A) and cache_b_k.shape == (1, nsamp, W_B, KV_B, HD_B)

    wqkv_bf = _pack_columns(w_in[0], QKV_PARTS).astype(BF16)
    wzg_bf = _pack_columns(w_in[0], OUT_PARTS).astype(BF16)
    wmem_bf = w_mem_kv[0].astype(BF16)
    wbr_bf = jnp.stack([w_br_a[0], w_br_b[0], w_br_c[0]]).astype(BF16)
    wout_bf = w_out[0].astype(BF16)
    lng = ln_g.reshape(1, D_MODEL)
    gains = jnp.stack([jnp.tile(gq_a[0], 8), jnp.tile(gk_a[0], 8), jnp.tile(gq_b[0], 8),
                       jnp.tile(gk_b[0], 8), jnp.tile(gq_c[0], 4)])
    gk_c4 = jnp.tile(gk_c[0], 4).reshape(1, 512)
    blockdiag = np.kron(np.eye(4, dtype=np.float32), np.ones((64, 64), np.float32))
    pmat = jnp.asarray(blockdiag, BF16)

    *tables, mult_a = _bucket_tables()
    bias_a, bias_b, bias_sa, bias_sb, s0a, s0b, skb = _bias_call(rel_bias, sinks_b.reshape(H_B), tables)

    xs2 = x_sample.reshape(nsamp, D_MODEL)
    sqa, ska, sva, sqb, skb_new, svb, sqc = _qkv_call(xs2, lng, wqkv_bf, gains, pmat)
    riders = _rider_arrays(sqa, ska, sva, cache_a_k[0], cache_a_v[0], bias_sa, jnp.asarray(mult_a), s0a)
    out_tm = 256
    n_qkv = batch * seq // 512
    n_dil = batch * (H_A // 2)
    n_out = batch * seq // out_tm
    n_alone = nsamp - n_qkv - n_dil - n_out
    assert n_alone >= 0

    xp2 = x_prompt.reshape(batch * seq, D_MODEL)
    qa, ka, va, qb, kb, vb, qc, ka_t, va_t, kb_t, vb_t, soa_1 = _qkv_call(
        xp2, lng, wqkv_bf, gains, pmat, (batch, seq), rider=(riders, 0))
    mk, mv = _memkv_call(mem_prompt.reshape(batch * N_MEM, D_MODEL), mem_ln_g.reshape(1, D_MODEL),
                         wmem_bf, gk_c4)
    oa, soa_2 = _dil_call(qa, ka, va, bias_a, batch, seq, rider=(riders, n_qkv))
    ob = _swa_call(sinks_b.reshape(H_B), qb, kb, vb, bias_b, batch, seq)
    oc = _mem_attn_call(qc, mk, mv, batch, seq)
    yp, soa_3 = _out_call(xp2, oa.reshape(-1, 512), ob.reshape(-1, 512), oc.reshape(-1, 512),
                          lng, wzg_bf, wbr_bf, wout_bf, tm=out_tm, rider=(riders, n_qkv + n_dil))
    yp = yp.reshape(batch, seq, D_MODEL)

    soa_parts = [soa_1, soa_2, soa_3]
    if n_alone:
        soa_parts.append(_samp_a_call(riders, n_qkv + n_dil + n_out, n_alone))
    soa = jnp.concatenate(soa_parts, axis=0).reshape(nsamp, 512)
    sob = _samp_b_call(sqb, skb_new, svb, cache_b_k[0], cache_b_v[0], bias_sb, s0b, skb)
    soc = _samp_c_call(sqc, cache_mem_k[0], cache_mem_v[0])
    ys, = _out_call(xs2, soa, sob, soc, lng, wzg_bf, wbr_bf, wout_bf)

    return (yp, ys.reshape(nsamp, 1, D_MODEL),
            jnp.transpose(ka_t.reshape(1, batch, H_A, HD_A, seq), (0, 1, 4, 2, 3)),
            jnp.transpose(va_t.reshape(1, batch, H_A, HD_A, seq), (0, 1, 4, 2, 3)),
            jnp.transpose(kb_t.reshape(1, batch, KV_B, HD_B, W_B), (0, 1, 4, 2, 3)),
            jnp.transpose(vb_t.reshape(1, batch, KV_B, HD_B, W_B), (0, 1, 4, 2, 3)),
            mk.reshape(1, batch, N_MEM, H_C, HD_C), mv.reshape(1, batch, N_MEM, H_C, HD_C),
            ska.reshape(1, nsamp, 1, H_A, HD_A), sva.reshape(1, nsamp, 1, H_A, HD_A),
            skb_new.reshape(1, nsamp, 1, KV_B, HD_B), svb.reshape(1, nsamp, 1, KV_B, HD_B))
```

```python
import functools
import math

import numpy as np
import jax
import jax.numpy as jnp
from jax import lax
from jax.experimental import pallas as pl
from jax.experimental.pallas import tpu as pltpu

F32 = jnp.float32
BF16 = jnp.bfloat16

D_MODEL = 1024
H_A, HD_A = 8, 64
A_DILATIONS = (1, 4, 16)
A_STEPS = 128
W_A = 2048
H_B, KV_B, HD_B, W_B = 8, 2, 64, 128
H_C, HD_C, N_MEM = 4, 128, 256
NUM_BUCKETS, MAX_DISTANCE = 32, 2048
EPS = 1e-6
NEG = -1e30
BLK = 128
UNROLL = 16

IN_COLS = dict(qa=(0, 512), ka=(512, 512), va=(1024, 512), za=(1536, 512),
               qb=(2048, 512), kb=(2560, 128), vb=(2688, 128), zb=(2816, 512),
               qc=(3328, 512), zc=(3840, 512), ga=(4352, 1024), gb=(5376, 1024), gc=(6400, 1024))
C_END = 7424
QKV_PARTS = ("qa", "ka", "va", "qb", "kb", "vb", "qc")
OUT_PARTS = ("za", "zb", "zc", "ga", "gb", "gc")


def _packed_offsets(parts):
    offs, at = {}, 0
    for name in parts:
        offs[name] = at
        at += IN_COLS[name][1]
    return offs, at


QKV_OFF, QKV_WIDTH = _packed_offsets(QKV_PARTS)
OUT_OFF, OUT_WIDTH = _packed_offsets(OUT_PARTS)


def _pack_columns(w, parts):
    return jnp.concatenate([w[:, IN_COLS[p][0]:IN_COLS[p][0] + IN_COLS[p][1]] for p in parts], axis=1)

VMEM_LIMIT = 56 * 1024 * 1024


def _cparams(sem):
    return pltpu.CompilerParams(dimension_semantics=sem, vmem_limit_bytes=VMEM_LIMIT)


def _const_spec(shape):
    nd = len(shape)
    return pl.BlockSpec(shape, lambda *_: (0,) * nd, pipeline_mode=pl.Buffered(1))


def _t5_bucket_np(dist):
    n = np.maximum(dist, 0)
    max_exact = NUM_BUCKETS // 2
    nf = np.maximum(n, 1).astype(np.float32)
    large = max_exact + (np.log(nf / np.float32(max_exact))
                         / np.float32(math.log(MAX_DISTANCE / max_exact))
                         * np.float32(NUM_BUCKETS - max_exact)).astype(np.int32)
    return np.where(n < max_exact, n, np.minimum(large, NUM_BUCKETS - 1)).astype(np.int32)


def _a_multiplicity(dist):
    return sum(((dist % d == 0) & (dist <= A_STEPS * d)).astype(np.int32) for d in A_DILATIONS)


def _bucket_tables():
    i = np.arange(BLK)[:, None]
    j = np.arange(2 * BLK)[None, :]
    rel = i + BLK - j
    bkt_a = np.stack([np.where((rel >= 0) & (rel <= A_STEPS), _t5_bucket_np(rel * d), -1)
                      for d in A_DILATIONS]).astype(np.int32)
    bkt_b = np.where((rel >= 0) & (rel < W_B), _t5_bucket_np(rel), -1).astype(np.int32)
    dist_a = W_A - np.arange(W_A)
    mult_a = _a_multiplicity(dist_a)
    sb_a = np.broadcast_to(np.where(mult_a > 0, _t5_bucket_np(dist_a), -1), (8, W_A)).astype(np.int32)
    dist_b = W_B - np.arange(W_B)
    sb_b = np.broadcast_to(np.where(dist_b < W_B, _t5_bucket_np(dist_b), -1), (8, W_B)).astype(np.int32)
    return bkt_a, bkt_b, sb_a, sb_b, mult_a.astype(np.float32).reshape(1, W_A)


def _rms_rows(x, g):
    return x * lax.rsqrt(jnp.mean(x * x, axis=-1, keepdims=True) + EPS) * g


def _dot(a, b):
    return jnp.dot(a, b, preferred_element_type=F32)


def _dot_nt(a, b):
    return lax.dot_general(a, b, (((1,), (1,)), ((), ())), preferred_element_type=F32)


def _headnorm64(t, g, pmat):
    sq = (t * t).astype(BF16)
    width = t.shape[1]
    step = min(width, 256)
    pp = pmat[:step, :step]
    parts = [_dot(sq[:, c:c + step], pp) for c in range(0, width, step)]
    ss = parts[0] if len(parts) == 1 else jnp.concatenate(parts, axis=1)
    return t * lax.rsqrt(ss * (1.0 / 64.0) + EPS) * g


def _headnorm128(t, g):
    parts = []
    for c in range(0, t.shape[1], 128):
        seg = t[:, c:c + 128]
        parts.append(seg * lax.rsqrt(jnp.mean(seg * seg, axis=-1, keepdims=True) + EPS))
    return jnp.concatenate(parts, axis=1) * g


def _bias_kernel(tbl_ref, sink_ref, bkta_ref, bktb_ref, sba_ref, sbb_ref,
                 ba_ref, bb_ref, oa_ref, ob_ref, s0a_ref, s0b_ref, sk_ref, *, present):
    pa, pb, psa, psb = present
    for g in range(3):
        bk = bkta_ref[g]
        for h in range(H_A):
            ba_ref[g, h] = jnp.full((BLK, 2 * BLK), NEG, F32)
        for b in pa[g]:
            hit = bk == b
            for h in range(H_A):
                ba_ref[g, h] = jnp.where(hit, tbl_ref[b, h], ba_ref[g, h])
    bk = bktb_ref[...]
    for h in range(H_B):
        bb_ref[h] = jnp.full((BLK, 2 * BLK), NEG, F32)
    for b in pb:
        hit = bk == b
        for h in range(H_B):
            bb_ref[h] = jnp.where(hit, tbl_ref[b, H_A + h], bb_ref[h])
    for (src, dst, buckets, col0) in ((sba_ref, oa_ref, psa, 0), (sbb_ref, ob_ref, psb, H_A)):
        bk = src[...]
        row = lax.broadcasted_iota(jnp.int32, bk.shape, 0)
        acc = jnp.full(bk.shape, NEG, F32)
        for b in buckets:
            hit = bk == b
            for h in range(8):
                acc = jnp.where(hit & (row == h), tbl_ref[b, col0 + h], acc)
        dst[...] = acc
    row8 = lax.broadcasted_iota(jnp.int32, (8, 128), 0)
    a0 = jnp.zeros((8, 128), F32)
    b0 = jnp.zeros((8, 128), F32)
    sk = jnp.zeros((8, 128), F32)
    for h in range(8):
        a0 = jnp.where(row8 == h, tbl_ref[0, h], a0)
        b0 = jnp.where(row8 == h, tbl_ref[0, H_A + h], b0)
        sk = jnp.where(row8 == h, sink_ref[h], sk)
    s0a_ref[...] = a0
    s0b_ref[...] = b0
    sk_ref[...] = sk


def _bias_call(rel_bias, sinks, tables):
    bkt_a, bkt_b, sb_a, sb_b = tables
    uniq = lambda a: tuple(int(b) for b in np.unique(a) if b >= 0)
    present = (tuple(uniq(bkt_a[g]) for g in range(3)), uniq(bkt_b), uniq(sb_a), uniq(sb_b))
    smem = pl.BlockSpec(memory_space=pltpu.SMEM)
    vmem = pl.BlockSpec(memory_space=pltpu.VMEM)
    return pl.pallas_call(
        functools.partial(_bias_kernel, present=present),
        in_specs=[smem, smem, vmem, vmem, vmem, vmem],
        out_specs=[vmem] * 7,
        out_shape=[jax.ShapeDtypeStruct((3, H_A, BLK, 2 * BLK), F32),
                   jax.ShapeDtypeStruct((H_B, BLK, 2 * BLK), F32),
                   jax.ShapeDtypeStruct((8, W_A), F32),
                   jax.ShapeDtypeStruct((8, W_B), F32),
                   jax.ShapeDtypeStruct((8, 128), F32),
                   jax.ShapeDtypeStruct((8, 128), F32),
                   jax.ShapeDtypeStruct((8, 128), F32)],
        compiler_params=pltpu.CompilerParams(vmem_limit_bytes=VMEM_LIMIT),
        name="bias_expand",
    )(rel_bias, sinks, jnp.asarray(bkt_a), jnp.asarray(bkt_b), jnp.asarray(sb_a), jnp.asarray(sb_b))


def _qkv_kernel(x_ref, lng_ref, w_ref, g_ref, p_ref, *refs, rider=False, tiles_per_seq=1):
    if rider:
        _ride(refs[:N_RIDER_IN], refs[-1])
        refs = refs[N_RIDER_IN:-1]
    qa_ref, ka_ref, va_ref, qb_ref, kb_ref, vb_ref, qc_ref, *cache_refs = refs
    h = _rms_rows(x_ref[...], lng_ref[...]).astype(BF16)
    pmat = p_ref[...]

    def proj(part):
        c0 = QKV_OFF[part]
        return _dot(h, w_ref[:, c0:c0 + IN_COLS[part][1]])

    qa_ref[...] = _headnorm64(proj("qa"), g_ref[0:1, :], pmat)
    ka = _headnorm64(proj("ka"), g_ref[1:2, :], pmat)
    va = proj("va")
    ka_ref[...] = ka
    va_ref[...] = va
    kb = _headnorm64(proj("kb"), g_ref[3:4, 0:128], pmat)
    vb = proj("vb")
    if cache_refs:
        kt_ref, vt_ref, kbt_ref, vbt_ref = cache_refs
        kt_ref[...] = ka.T
        vt_ref[...] = va.T

        @pl.when((pl.program_id(0) + 1) % tiles_per_seq == 0)
        def _():
            tm = kb.shape[0]
            kbt_ref[...] = kb[tm - W_B:, :].T
            vbt_ref[...] = vb[tm - W_B:, :].T
    qb_ref[...] = _headnorm64(proj("qb"), g_ref[2:3, :], pmat).astype(qb_ref.dtype)
    kb_ref[...] = kb
    vb_ref[...] = vb
    qc_ref[...] = _headnorm128(proj("qc"), g_ref[4:5, :]).astype(qc_ref.dtype)


def _qkv_call(x2, lng, w_bf, gains, pmat, cache_layout_for=None, rider=None):
    n = x2.shape[0]
    tm = min(512, n)
    steps = n // tm
    row = lambda w: pl.BlockSpec((tm, w), lambda i: (i, 0))
    widths = (512, 512, 512, 512, 128, 128, 512)
    in_specs = [row(D_MODEL), _const_spec((1, D_MODEL)), _const_spec((D_MODEL, QKV_WIDTH)),
                _const_spec((5, 512)), _const_spec((256, 256))]
    args = [x2, lng, w_bf, gains, pmat]
    out_specs = [row(w) for w in widths]
    q_dtype = F32 if cache_layout_for is None else BF16
    out_shape = [jax.ShapeDtypeStruct((n, w), q_dtype if i in (3, 6) else F32) for i, w in enumerate(widths)]
    per = 1
    if cache_layout_for is not None:
        batch, seq = cache_layout_for
        per = seq // tm
        out_specs += [pl.BlockSpec((None, 512, tm), lambda i: (i // per, 0, i % per))] * 2
        out_shape += [jax.ShapeDtypeStruct((batch, 512, seq), F32)] * 2
        out_specs += [pl.BlockSpec((None, KV_B * HD_B, W_B), lambda i: (i // per, 0, 0))] * 2
        out_shape += [jax.ShapeDtypeStruct((batch, KV_B * HD_B, W_B), F32)] * 2
    if rider is not None:
        r_in, r_out = _rider_specs(rider[1], lambda i: i)
        in_specs += r_in
        args += list(rider[0])
        out_specs.append(r_out)
        out_shape.append(jax.ShapeDtypeStruct((steps, 1, 512), F32))
    return pl.pallas_call(
        functools.partial(_qkv_kernel, rider=rider is not None, tiles_per_seq=per),
        grid=(steps,),
        in_specs=in_specs,
        out_specs=out_specs,
        out_shape=out_shape,
        compiler_params=_cparams(("arbitrary",)),
        name="qkv_proj",
    )(*args)


def _memkv_kernel(m_ref, g_ref, w_ref, gk_ref, mk_ref, mv_ref):
    h = _rms_rows(m_ref[...], g_ref[...]).astype(BF16)
    mk_ref[...] = _headnorm128(_dot(h, w_ref[:, 0:512]), gk_ref[...])
    mv_ref[...] = _dot(h, w_ref[:, 512:1024])


def _memkv_call(mem2, g, w_bf, gk):
    n = mem2.shape[0]
    tm = 512
    row = lambda w: pl.BlockSpec((tm, w), lambda i: (i, 0))
    return pl.pallas_call(
        _memkv_kernel,
        grid=(n // tm,),
        in_specs=[row(D_MODEL), _const_spec((1, D_MODEL)), _const_spec((D_MODEL, 2 * 512)),
                  _const_spec((1, 512))],
        out_specs=[row(512), row(512)],
        out_shape=[jax.ShapeDtypeStruct((n, 512), F32)] * 2,
        compiler_params=_cparams(("arbitrary",)),
        name="mem_kv_proj",
    )(mem2, g, w_bf, gk)


def _out_kernel(x_ref, oa_ref, ob_ref, oc_ref, lng_ref, w_ref, wbr_ref, wout_ref, *refs):
    if len(refs) > 1:
        _ride(refs[:N_RIDER_IN], refs[-1])
    y_ref = refs[-2] if len(refs) > 1 else refs[0]
    x = x_ref[...]
    h = _rms_rows(x, lng_ref[...]).astype(BF16)
    acc = None
    for br, (o_ref, zpart, gpart) in enumerate(((oa_ref, "za", "ga"), (ob_ref, "zb", "gb"), (oc_ref, "zc", "gc"))):
        cz, cg = OUT_OFF[zpart], OUT_OFF[gpart]
        z = _dot(h, w_ref[:, cz:cz + 512])
        y = (o_ref[...] * (z * jax.nn.sigmoid(z))).astype(BF16)
        yb = _dot(y, wbr_ref[br])
        gate = jax.nn.sigmoid(_dot(h, w_ref[:, cg:cg + D_MODEL]))
        acc = gate * yb if acc is None else acc + gate * yb
    y_ref[...] = x + _dot(acc.astype(BF16), wout_ref[...])


def _out_call(x2, oa, ob, oc, lng, w_bf, wbr_bf, wout_bf, tm=512, rider=None):
    n = x2.shape[0]
    tm = min(tm, n)
    steps = n // tm
    row = lambda w: pl.BlockSpec((tm, w), lambda i: (i, 0))
    in_specs = [row(D_MODEL), row(512), row(512), row(512), _const_spec((1, D_MODEL)),
                _const_spec((D_MODEL, OUT_WIDTH)), _const_spec((3, 512, D_MODEL)), _const_spec((D_MODEL, D_MODEL))]
    args = [x2, oa, ob, oc, lng, w_bf, wbr_bf, wout_bf]
    out_specs = [row(D_MODEL)]
    out_shape = [jax.ShapeDtypeStruct((n, D_MODEL), F32)]
    if rider is not None:
        r_in, r_out = _rider_specs(rider[1], lambda i: i)
        in_specs += r_in
        args += list(rider[0])
        out_specs.append(r_out)
        out_shape.append(jax.ShapeDtypeStruct((steps, 1, 512), F32))
    return pl.pallas_call(
        _out_kernel,
        grid=(steps,),
        in_specs=in_specs,
        out_specs=out_specs,
        out_shape=out_shape,
        compiler_params=_cparams(("arbitrary",)),
        name="gated_out",
    )(*args)


PAD = BLK
CLS_PITCH = BLK + 8


def _low_lanes(rows):
    return lax.broadcasted_iota(jnp.int32, (rows, 128), 1) < 64


def _aligned(start):
    return start if isinstance(start, int) else pl.multiple_of(start, BLK)


def _pair_tile(q, k0, k1, v0, v1, b0, b1, sinks=None):
    nk = k0.shape[0]
    low = _low_lanes(BLK)
    s = _dot_nt(q, jnp.concatenate([k0, k1], axis=0)) + jnp.concatenate([b0, b1], axis=1)
    ms = [jnp.max(s[:, hh * nk:(hh + 1) * nk], axis=-1, keepdims=True) for hh in range(2)]
    if sinks is not None:
        ms = [jnp.maximum(ms[hh], sinks[hh]) for hh in range(2)]
    p = jnp.concatenate([jnp.exp(s[:, hh * nk:(hh + 1) * nk] - ms[hh]) for hh in range(2)], axis=1).astype(BF16)
    own = _low_lanes(nk)
    ones = [jnp.where(own, 1.0, 0.0).astype(BF16), jnp.where(own, 0.0, 1.0).astype(BF16)]
    vblk = jnp.concatenate([jnp.concatenate([v0, ones[0]], axis=1),
                            jnp.concatenate([v1, ones[1]], axis=1)], axis=0)
    up = _dot(p, vblk)
    return up[:, :128], jnp.where(low, ms[0], ms[1]), up[:, 128:]


def _first_bias(bias, also=None):
    gone = lax.broadcasted_iota(jnp.int32, bias.shape, 1) < BLK
    if also is not None:
        gone = gone & also
    return jnp.where(gone, NEG, bias)


def _dil_kernel(q_ref, k_ref, v_ref, bias_ref, *refs, rider=False):
    if rider:
        _ride(refs[:N_RIDER_IN], refs[N_RIDER_IN + 1])
        refs = refs[N_RIDER_IN:N_RIDER_IN + 1] + refs[N_RIDER_IN + 2:]
    o_ref, qs_ref, kc_ref, vc_ref, c4_ref, ru_ref, rm_ref, rl_ref = refs
    scale = HD_A ** -0.5
    seq = q_ref.shape[0]
    ntile = seq // BLK
    low = _low_lanes(BLK)
    d1, d2 = A_DILATIONS[1], A_DILATIONS[2]

    def src_rows(g, t):
        d = A_DILATIONS[g]
        if d == 1:
            return pl.ds(_aligned(t * BLK), BLK)
        nblk = ntile // d
        r, n = t >> (nblk.bit_length() - 1), t & (nblk - 1)
        return pl.ds(r + n * (d * BLK), BLK, stride=d)

    zero = jnp.zeros((PAD, 128), BF16)
    stagers = []
    for g in range(3):
        for hh in range(2):
            kc_ref[g, hh, 0:PAD, :] = zero
            vc_ref[g, hh, 0:PAD, :] = zero

        def stage(t, c, g=g):
            dst = pl.ds(_aligned(t * BLK), BLK)
            dstp = pl.ds(_aligned(PAD + t * BLK), BLK)
            if g < 2:
                src = src_rows(g, t)
                q, k, v = q_ref[src, :], k_ref[src, :], v_ref[src, :]
            else:
                src = pl.ds((t & (d1 - 1)) * (seq // d1) + (t >> (d1.bit_length() - 1)), BLK, stride=d2 // d1)
                q, k, v = c4_ref[0, src, :], c4_ref[1, src, :], c4_ref[2, src, :]
            if g == 1:
                c4_ref[0, dst, :], c4_ref[1, dst, :], c4_ref[2, dst, :] = q, k, v
            qs_ref[g, dst, :] = (q * scale).astype(BF16)
            kc_ref[g, 0, dstp, :] = jnp.where(low, k, 0.0).astype(BF16)
            kc_ref[g, 1, dstp, :] = jnp.where(low, 0.0, k).astype(BF16)
            vc_ref[g, 0, dstp, :] = jnp.where(low, v, 0.0).astype(BF16)
            vc_ref[g, 1, dstp, :] = jnp.where(low, 0.0, v).astype(BF16)
            return c

        stagers.append(stage)

    def tiles(g, i):
        nblk = ntile // A_DILATIONS[g]
        res = []
        for j in range(UNROLL):
            t = i * UNROLL + j
            qrows = pl.ds(_aligned(t * BLK), BLK)
            krows = pl.ds(_aligned(t * BLK), 2 * BLK)
            bias = [bias_ref[g, hh] for hh in range(2)]
            if nblk <= UNROLL and j % nblk == 0:
                bias = [_first_bias(b) for b in bias]
            elif nblk > UNROLL and j == 0:
                bias = [_first_bias(b, i == 0) for b in bias]
            res.append(_pair_tile(qs_ref[g, qrows, :], kc_ref[g, 0, krows, :], kc_ref[g, 1, krows, :],
                                  vc_ref[g, 0, krows, :], vc_ref[g, 1, krows, :], bias[0], bias[1]))
        return res

    assert UNROLL == ntile
    for g in range(3):
        for t in range(ntile):
            stagers[g](t, 0)
        res = tiles(g, 0)
        for t in range(ntile):
            rows = pl.ds(t * CLS_PITCH, BLK) if g == 2 else src_rows(g, t)
            ru_ref[g, rows, :], rm_ref[g, rows, :], rl_ref[g, rows, :] = res[t]

    def natural_rows(ref, i):
        parts = [ref[2, pl.ds((v % 2) * 8 * CLS_PITCH + i * (BLK // d2) + v // 2, 8, stride=CLS_PITCH), :]
                 for v in range(BLK // 8)]
        return jnp.concatenate(parts, axis=0)

    def combine(i, c):
        rows = pl.ds(_aligned(i * BLK), BLK)
        ms = [rm_ref[0, rows, :], rm_ref[1, rows, :], natural_rows(rm_ref, i)]
        mx = jnp.maximum(jnp.maximum(ms[0], ms[1]), ms[2])
        ws = [jnp.exp(m - mx) for m in ms]
        num = ws[0] * ru_ref[0, rows, :] + ws[1] * ru_ref[1, rows, :] + ws[2] * natural_rows(ru_ref, i)
        den = ws[0] * rl_ref[0, rows, :] + ws[1] * rl_ref[1, rows, :] + ws[2] * natural_rows(rl_ref, i)
        o_ref[rows, :] = num / den
        return c

    lax.fori_loop(0, ntile, combine, 0, unroll=2)


def _dil_call(qa, ka, va, bias_a, batch, seq, rider=None):
    assert seq == A_DILATIONS[2] * BLK and UNROLL % (seq // (A_DILATIONS[1] * BLK)) == 0
    blk = pl.BlockSpec((None, seq, 128), lambda b, p: (b, 0, p))
    pairs = H_A // 2
    in_specs = [blk, blk, blk, pl.BlockSpec((3, 2, BLK, 2 * BLK), lambda b, p: (0, p, 0, 0))]
    args = [qa.reshape(batch, seq, 512), ka.reshape(batch, seq, 512), va.reshape(batch, seq, 512), bias_a]
    out_specs = [blk]
    out_shape = [jax.ShapeDtypeStruct((batch, seq, 512), F32)]
    if rider is not None:
        r_in, r_out = _rider_specs(rider[1], lambda b, p: b * pairs + p)
        in_specs += r_in
        args += list(rider[0])
        out_specs.append(r_out)
        out_shape.append(jax.ShapeDtypeStruct((batch * pairs, 1, 512), F32))
    return pl.pallas_call(
        functools.partial(_dil_kernel, rider=rider is not None),
        grid=(batch, pairs),
        in_specs=in_specs,
        out_specs=out_specs,
        out_shape=out_shape,
        scratch_shapes=[pltpu.VMEM((3, seq, 128), BF16), pltpu.VMEM((3, 2, PAD + seq, 128), BF16),
                        pltpu.VMEM((3, 2, PAD + seq, 128), BF16), pltpu.VMEM((3, seq, 128), F32)]
                       + [pltpu.VMEM((3, A_DILATIONS[2] * CLS_PITCH, 128), F32)] * 3,
        compiler_params=_cparams(("arbitrary", "arbitrary")),
        name="dilated_attn",
    )(*args)


def _swa_kernel(sink_ref, q_ref, k_ref, v_ref, bias_ref, o_ref, qs_ref, kc_ref, vc_ref):
    scale = HD_B ** -0.5
    seq = q_ref.shape[0]
    ntile = seq // BLK
    pairs = q_ref.shape[1] // 128
    kv = pl.program_id(1)
    low = _low_lanes(BLK)
    own_half = (lax.broadcasted_iota(jnp.int32, (BLK, 128), 1) >= 64).astype(jnp.int32) == kv

    def dup(xf):
        return jnp.where(own_half, xf, pltpu.roll(xf, 64, 1))

    zero = jnp.zeros((PAD, 128), BF16)
    for hh in range(2):
        kc_ref[hh, 0:PAD, :] = zero
        vc_ref[hh, 0:PAD, :] = zero

    def stage(t, c):
        src = pl.ds(pl.multiple_of(t * BLK, BLK), BLK)
        dstp = pl.ds(pl.multiple_of(PAD + t * BLK, BLK), BLK)
        for pp in range(pairs):
            qs_ref[pp, src, :] = (q_ref[src, 128 * pp:128 * (pp + 1)] * scale).astype(BF16)
        k = dup(k_ref[src, :])
        v = dup(v_ref[src, :])
        kc_ref[0, dstp, :] = jnp.where(low, k, 0.0).astype(BF16)
        kc_ref[1, dstp, :] = jnp.where(low, 0.0, k).astype(BF16)
        vc_ref[0, dstp, :] = jnp.where(low, v, 0.0).astype(BF16)
        vc_ref[1, dstp, :] = jnp.where(low, 0.0, v).astype(BF16)
        return c

    lax.fori_loop(0, ntile, stage, 0, unroll=4)

    for pp in range(pairs):
        head = (kv * pairs + pp) * 2
        sinks = (sink_ref[head], sink_ref[head + 1])
        sk = jnp.where(low, sinks[0], sinks[1])

        def body(i, c, pp=pp, sinks=sinks, sk=sk):
            res = []
            for j in range(UNROLL):
                t = i * UNROLL + j
                qrows = pl.ds(pl.multiple_of(t * BLK, BLK), BLK)
                krows = pl.ds(pl.multiple_of(t * BLK, BLK), 2 * BLK)
                bias = [bias_ref[2 * pp + hh] for hh in range(2)]
                if j == 0:
                    bias = [_first_bias(b, i == 0) for b in bias]
                u, m, l = _pair_tile(qs_ref[pp, qrows, :], kc_ref[0, krows, :], kc_ref[1, krows, :],
                                     vc_ref[0, krows, :], vc_ref[1, krows, :], bias[0], bias[1], sinks)
                res.append(u / (l + jnp.exp(sk - m)))
            for j in range(UNROLL):
                rows = pl.ds(pl.multiple_of((i * UNROLL + j) * BLK, BLK), BLK)
                o_ref[rows, 128 * pp:128 * (pp + 1)] = res[j]
            return c

        lax.fori_loop(0, ntile // UNROLL, body, 0)


def _swa_call(sinks, qb, kb, vb, bias_b, batch, seq):
    per_kv = H_B // KV_B
    qblk = pl.BlockSpec((None, seq, per_kv * HD_B), lambda b, kv: (b, 0, kv))
    kblk = pl.BlockSpec((None, seq, 128), lambda b, kv: (b, 0, 0))
    return pl.pallas_call(
        _swa_kernel,
        grid=(batch, KV_B),
        in_specs=[pl.BlockSpec(memory_space=pltpu.SMEM), qblk, kblk, kblk,
                  pl.BlockSpec((per_kv, BLK, 2 * BLK), lambda b, kv: (kv, 0, 0))],
        out_specs=qblk,
        out_shape=jax.ShapeDtypeStruct((batch, seq, 512), F32),
        scratch_shapes=[pltpu.VMEM((per_kv // 2, seq, 128), BF16), pltpu.VMEM((2, PAD + seq, 128), BF16),
                        pltpu.VMEM((2, PAD + seq, 128), BF16)],
        compiler_params=_cparams(("arbitrary", "arbitrary")),
        name="swa_attn",
    )(sinks, qb.reshape(batch, seq, 512), kb.reshape(batch, seq, 128), vb.reshape(batch, seq, 128), bias_b)


def _mem_attn_kernel(q_ref, mk_ref, mv_ref, o_ref):
    scale = HD_C ** -0.5
    ones = jnp.ones((N_MEM, HD_C), BF16)
    tq = 512
    for h in range(H_C):
        cs = slice(h * HD_C, (h + 1) * HD_C)
        k = mk_ref[:, cs].astype(BF16)
        v = jnp.concatenate([mv_ref[:, cs].astype(BF16), ones], axis=1)
        for r in range(0, q_ref.shape[0], tq):
            s = _dot_nt(q_ref[r:r + tq, cs].astype(BF16), k) * scale
            p = jnp.exp(s - jnp.max(s, axis=-1, keepdims=True)).astype(BF16)
            up = _dot(p, v)
            o_ref[r:r + tq, cs] = up[:, :HD_C] / up[:, HD_C:]


def _mem_attn_call(qc, mk, mv, batch, seq):
    tq = 1024
    return pl.pallas_call(
        _mem_attn_kernel,
        grid=(batch, seq // tq),
        in_specs=[pl.BlockSpec((None, tq, 512), lambda b, i: (b, i, 0)),
                  pl.BlockSpec((None, N_MEM, 512), lambda b, i: (b, 0, 0)),
                  pl.BlockSpec((None, N_MEM, 512), lambda b, i: (b, 0, 0))],
        out_specs=pl.BlockSpec((None, tq, 512), lambda b, i: (b, i, 0)),
        out_shape=jax.ShapeDtypeStruct((batch, seq, 512), F32),
        compiler_params=_cparams(("arbitrary", "arbitrary")),
        name="mem_attn",
    )(qc.reshape(batch, seq, 512), mk.reshape(batch, N_MEM, 512), mv.reshape(batch, N_MEM, 512))


def _head_rows(row, width, head_dim):
    rid = lax.broadcasted_iota(jnp.int32, (8, width), 0)
    hid = lax.broadcasted_iota(jnp.int32, (8, width), 1) // head_dim
    return rid == hid, jnp.where(rid == hid, jnp.broadcast_to(row, (8, width)), 0.0)


def _samp_a_one(q, kn, vn, kt_ref, vt_ref, bias_ref, mult_ref, b0_ref):
    scale = HD_A ** -0.5
    nt = W_A // 128
    own, qh = _head_rows(q, 512, HD_A)
    qcols = jnp.transpose(jnp.broadcast_to(q, (128, 512)))
    s_heads = []
    for h in range(H_A):
        rows = slice(h * HD_A, (h + 1) * HD_A)
        qb = qcols[rows, :]
        s_heads.append(jnp.concatenate(
            [jnp.sum(kt_ref[rows, 128 * j:128 * (j + 1)] * qb, axis=0, keepdims=True) for j in range(nt)], axis=1))
    s = jnp.concatenate(s_heads, axis=0) * scale + bias_ref[...]
    ln = jnp.sum(qh * kn, axis=-1, keepdims=True) * scale + b0_ref[:, 0:1]
    m = jnp.maximum(jnp.max(s, axis=-1, keepdims=True), ln)
    p = jnp.exp(s - m) * mult_ref[...]
    pn = len(A_DILATIONS) * jnp.exp(ln - m)
    den = jnp.sum(p, axis=-1, keepdims=True) + pn
    accs = []
    for h in range(H_A):
        rows = slice(h * HD_A, (h + 1) * HD_A)
        acc = vt_ref[rows, 0:128] * p[h:h + 1, 0:128]
        for j in range(1, nt):
            acc = acc + vt_ref[rows, 128 * j:128 * (j + 1)] * p[h:h + 1, 128 * j:128 * (j + 1)]
        accs.append(acc)
    pv = jnp.sum(jnp.transpose(jnp.concatenate(accs, axis=0)), axis=0, keepdims=True)
    lanes = lambda col: jnp.sum(jnp.where(own, col, 0.0), axis=0, keepdims=True)
    return (pv + lanes(pn) * vn) / lanes(den)


N_RIDER_IN = 8


def _rider_arrays(qa, kn, vn, cache_k, cache_v, bias_sa, mult, s0a):
    nb = qa.shape[0]
    kt = jnp.transpose(cache_k, (0, 2, 3, 1)).reshape(nb, 512, W_A)
    vt = jnp.transpose(cache_v, (0, 2, 3, 1)).reshape(nb, 512, W_A)
    return (qa.reshape(nb, 1, 512), kn.reshape(nb, 1, 512), vn.reshape(nb, 1, 512), kt, vt, bias_sa, mult, s0a)


def _rider_specs(first, step_of):
    row = pl.BlockSpec((None, 1, 512), lambda *ids: (first + step_of(*ids), 0, 0))
    cblk = pl.BlockSpec((None, 512, W_A), lambda *ids: (first + step_of(*ids), 0, 0))
    ins = [row, row, row, cblk, cblk, _const_spec((8, W_A)), _const_spec((1, W_A)), _const_spec((8, 128))]
    return ins, pl.BlockSpec((None, 1, 512), lambda *ids: (step_of(*ids), 0, 0))


def _ride(in_refs, o_ref):
    q_ref, kn_ref, vn_ref, kt_ref, vt_ref, bias_ref, mult_ref, b0_ref = in_refs
    o_ref[...] = _samp_a_one(q_ref[...], kn_ref[...], vn_ref[...], kt_ref, vt_ref, bias_ref, mult_ref, b0_ref)


def _samp_a_kernel(*refs):
    _ride(refs[:N_RIDER_IN], refs[-1])


def _samp_a_call(arrays, first, count):
    r_in, r_out = _rider_specs(first, lambda i: i)
    return pl.pallas_call(
        _samp_a_kernel,
        grid=(count,),
        in_specs=r_in,
        out_specs=r_out,
        out_shape=jax.ShapeDtypeStruct((count, 1, 512), F32),
        compiler_params=_cparams(("arbitrary",)),
        name="sample_dilated_attn",
    )(*arrays)


def _samp_b_kernel(q_ref, kn_ref, vn_ref, kt_ref, vt_ref, bias_ref, b0_ref, sk_ref, o_ref):
    scale = HD_B ** -0.5
    low = lax.broadcasted_iota(jnp.int32, (1, 128), 1) < 64
    rid = lax.broadcasted_iota(jnp.int32, (8, 128), 0)
    kv_half = (lax.broadcasted_iota(jnp.int32, (8, 128), 1) >= 64) == (rid >= 4)

    bb = q_ref.shape[0]
    zero = jnp.zeros((8, 128), F32)
    qhs = []
    for b in range(bb):
        q = q_ref[b:b + 1, :]
        rows = []
        for h in range(H_B):
            chunk = q[:, 128 * (h // 2):128 * (h // 2) + 128]
            if (h % 2) != (h // 4):
                chunk = pltpu.roll(chunk, 64, 1)
            rows.append(chunk)
        qhs.append(jnp.where(kv_half, jnp.concatenate(rows, axis=0), 0.0))
    qblk = jnp.concatenate([jnp.concatenate([qhs[b] if c == b else zero for c in range(bb)], axis=1)
                            for b in range(bb)], axis=0)
    kt = kt_ref[...].reshape(bb * 128, W_B).astype(BF16)
    vt = vt_ref[...].reshape(bb * 128, W_B).astype(BF16)
    tile8 = lambda ref: jnp.concatenate([ref[...]] * bb, axis=0)
    s = _dot(qblk.astype(BF16), kt) * scale + tile8(bias_ref)
    knew = jnp.concatenate([jnp.broadcast_to(kn_ref[b:b + 1, :], (8, 128)) for b in range(bb)], axis=0)
    vnew = jnp.concatenate([jnp.broadcast_to(vn_ref[b:b + 1, :], (8, 128)) for b in range(bb)], axis=0)
    qall = jnp.concatenate(qhs, axis=0)
    ln = jnp.sum(qall * knew, axis=-1, keepdims=True) * scale + tile8(b0_ref)[:, 0:1]
    sk = tile8(sk_ref)[:, 0:1]
    m = jnp.maximum(jnp.maximum(jnp.max(s, axis=-1, keepdims=True), ln), sk)
    p = jnp.exp(s - m)
    pn = jnp.exp(ln - m)
    den = jnp.sum(p, axis=-1, keepdims=True) + pn + jnp.exp(sk - m)
    rall = _dot_nt(p.astype(BF16), vt)
    for b in range(bb):
        r = (rall[8 * b:8 * b + 8, 128 * b:128 * b + 128] + pn[8 * b:8 * b + 8] * vnew[8 * b:8 * b + 8])
        r = jnp.where(kv_half, r / den[8 * b:8 * b + 8], 0.0)
        chunks = []
        for c2 in range(4):
            pair = []
            for h in (2 * c2, 2 * c2 + 1):
                piece = r[h:h + 1, :]
                if (h % 2) != (h // 4):
                    piece = pltpu.roll(piece, 64, 1)
                pair.append(piece)
            chunks.append(jnp.where(low, pair[0], pair[1]))
        o_ref[b:b + 1, :] = jnp.concatenate(chunks, axis=1)


def _samp_b_call(qb, kn, vn, cache_k, cache_v, bias_sb, s0b, skb):
    nb = qb.shape[0]
    bb = 8
    kt = jnp.transpose(cache_k, (0, 2, 3, 1)).reshape(nb, 128, W_B)
    vt = jnp.transpose(cache_v, (0, 2, 3, 1)).reshape(nb, 128, W_B)
    row = lambda w: pl.BlockSpec((bb, w), lambda i: (i, 0))
    cblk = pl.BlockSpec((bb, 128, W_B), lambda i: (i, 0, 0))
    return pl.pallas_call(
        _samp_b_kernel,
        grid=(nb // bb,),
        in_specs=[row(512), row(128), row(128), cblk, cblk, _const_spec((8, W_B)),
                  _const_spec((8, 128)), _const_spec((8, 128))],
        out_specs=row(512),
        out_shape=jax.ShapeDtypeStruct((nb, 512), F32),
        compiler_params=_cparams(("arbitrary",)),
        name="sample_swa_attn",
    )(qb, kn, vn, kt, vt, bias_sb, s0b, skb)


def _samp_c_kernel(q_ref, mk_ref, mv_ref, mask_ref, o_ref):
    scale = HD_C ** -0.5
    bb = q_ref.shape[0]
    zero = jnp.zeros((H_C, HD_C), F32)
    qblk = jnp.concatenate([jnp.concatenate([q_ref[b] if c == b else zero for c in range(bb)], axis=1)
                            for b in range(bb)], axis=0)
    kcat = jnp.concatenate([mk_ref[b] for b in range(bb)], axis=1).astype(BF16)
    vcat = jnp.concatenate([mv_ref[b] for b in range(bb)], axis=1).astype(BF16)
    s = _dot_nt(qblk.astype(BF16), kcat) * scale + mask_ref[...]
    p = jnp.exp(s - jnp.max(s, axis=-1, keepdims=True))
    den = jnp.sum(p, axis=-1, keepdims=True)
    rall = _dot(p.astype(BF16), vcat)
    for b in range(bb):
        o_ref[b] = rall[H_C * b:H_C * (b + 1), HD_C * b:HD_C * (b + 1)] / den[H_C * b:H_C * (b + 1)]


def _samp_c_call(qc, cache_k, cache_v):
    nb = qc.shape[0]
    bb = 8
    rows = N_MEM * H_C
    head_of_row = np.arange(rows)[None, :] % H_C
    head_of_query = np.arange(bb * H_C)[:, None] % H_C
    mask = np.where(head_of_row == head_of_query, 0.0, NEG).astype(np.float32)
    qblk = pl.BlockSpec((bb, H_C, HD_C), lambda i: (i, 0, 0))
    cblk = pl.BlockSpec((bb, rows, HD_C), lambda i: (i, 0, 0))
    out = pl.pallas_call(
        _samp_c_kernel,
        grid=(nb // bb,),
        in_specs=[qblk, cblk, cblk, _const_spec((bb * H_C, rows))],
        out_specs=qblk,
        out_shape=jax.ShapeDtypeStruct((nb, H_C, HD_C), F32),
        compiler_params=_cparams(("arbitrary",)),
        name="sample_mem_attn",
    )(qc.reshape(nb, H_C, HD_C), cache_k.reshape(nb, rows, HD_C), cache_v.reshape(nb, rows, HD_C),
      jnp.asarray(mask))
    return out.reshape(nb, 512)


def kernel(x_prompt, x_sample, mem_prompt, cache_a_k, cache_a_v, cache_b_k, cache_b_v, cache_mem_k,
           cache_mem_v, rel_bias, ln_g, w_in, gq_a, gk_a, gq_b, gk_b, gq_c, gk_c, sinks_b, mem_ln_g,
           w_mem_kv, w_br_a, w_br_b, w_br_c, w_out):
    batch, seq, _ = x_prompt.shape
    nsamp = x_sample.shape[0]
    assert ln_g.shape[0] == 1 and x_sample.shape[1] == 1
    assert (batch, seq, nsamp) == (8, 2048, 128) and w_in.shape == (1, D_MODEL, C_END)
    assert cache_a_k.shape == (1, nsamp, W_A, H_A, HD_A) and cache_b_k.shape == (1, nsamp, W_B, KV_B, HD_B)

    wqkv_bf = _pack_columns(w_in[0], QKV_PARTS).astype(BF16)
    wzg_bf = _pack_columns(w_in[0], OUT_PARTS).astype(BF16)
    wmem_bf = w_mem_kv[0].astype(BF16)
    wbr_bf = jnp.stack([w_br_a[0], w_br_b[0], w_br_c[0]]).astype(BF16)
    wout_bf = w_out[0].astype(BF16)
    lng = ln_g.reshape(1, D_MODEL)
    gains = jnp.stack([jnp.tile(gq_a[0], 8), jnp.tile(gk_a[0], 8), jnp.tile(gq_b[0], 8),
                       jnp.tile(gk_b[0], 8), jnp.tile(gq_c[0], 4)])
    gk_c4 = jnp.tile(gk_c[0], 4).reshape(1, 512)
    blockdiag = np.kron(np.eye(4, dtype=np.float32), np.ones((64, 64), np.float32))
    pmat = jnp.asarray(blockdiag, BF16)

    *tables, mult_a = _bucket_tables()
    bias_a, bias_b, bias_sa, bias_sb, s0a, s0b, skb = _bias_call(rel_bias, sinks_b.reshape(H_B), tables)

    xs2 = x_sample.reshape(nsamp, D_MODEL)
    sqa, ska, sva, sqb, skb_new, svb, sqc = _qkv_call(xs2, lng, wqkv_bf, gains, pmat)
    riders = _rider_arrays(sqa, ska, sva, cache_a_k[0], cache_a_v[0], bias_sa, jnp.asarray(mult_a), s0a)
    out_tm = 256
    n_qkv = batch * seq // 512
    n_dil = batch * (H_A // 2)
    n_out = batch * seq // out_tm
    n_alone = nsamp - n_qkv - n_dil - n_out
    assert n_alone >= 0

    xp2 = x_prompt.reshape(batch * seq, D_MODEL)
    qa, ka, va, qb, kb, vb, qc, ka_t, va_t, kb_t, vb_t, soa_1 = _qkv_call(
        xp2, lng, wqkv_bf, gains, pmat, (batch, seq), rider=(riders, 0))
    mk, mv = _memkv_call(mem_prompt.reshape(batch * N_MEM, D_MODEL), mem_ln_g.reshape(1, D_MODEL),
                         wmem_bf, gk_c4)
    oa, soa_2 = _dil_call(qa, ka, va, bias_a, batch, seq, rider=(riders, n_qkv))
    ob = _swa_call(sinks_b.reshape(H_B), qb, kb, vb, bias_b, batch, seq)
    oc = _mem_attn_call(qc, mk, mv, batch, seq)
    yp, soa_3 = _out_call(xp2, oa.reshape(-1, 512), ob.reshape(-1, 512), oc.reshape(-1, 512),
                          lng, wzg_bf, wbr_bf, wout_bf, tm=out_tm, rider=(riders, n_qkv + n_dil))
    yp = yp.reshape(batch, seq, D_MODEL)

    soa_parts = [soa_1, soa_2, soa_3]
    if n_alone:
        soa_parts.append(_samp_a_call(riders, n_qkv + n_dil + n_out, n_alone))
    soa = jnp.concatenate(soa_parts, axis=0).reshape(nsamp, 512)
    sob = _samp_b_call(sqb, skb_new, svb, cache_b_k[0], cache_b_v[0], bias_sb, s0b, skb)
    soc = _samp_c_call(sqc, cache_mem_k[0], cache_mem_v[0])
    ys, = _out_call(xs2, soa, sob, soc, lng, wzg_bf, wbr_bf, wout_bf)

    return (yp, ys.reshape(nsamp, 1, D_MODEL),
            jnp.transpose(ka_t.reshape(1, batch, H_A, HD_A, seq), (0, 1, 4, 2, 3)),
            jnp.transpose(va_t.reshape(1, batch, H_A, HD_A, seq), (0, 1, 4, 2, 3)),
            jnp.transpose(kb_t.reshape(1, batch, KV_B, HD_B, W_B), (0, 1, 4, 2, 3)),
            jnp.transpose(vb_t.reshape(1, batch, KV_B, HD_B, W_B), (0, 1, 4, 2, 3)),
            mk.reshape(1, batch, N_MEM, H_C, HD_C), mv.reshape(1, batch, N_MEM, H_C, HD_C),
            ska.reshape(1, nsamp, 1, H_A, HD_A), sva.reshape(1, nsamp, 1, H_A, HD_A),
            skb_new.reshape(1, nsamp, 1, KV_B, HD_B), svb.reshape(1, nsamp, 1, KV_B, HD_B))
```

```python
import functools
import math

import numpy as np
import jax
import jax.numpy as jnp
from jax import lax
from jax.experimental import pallas as pl
from jax.experimental.pallas import tpu as pltpu

F32 = jnp.float32
BF16 = jnp.bfloat16

D_MODEL = 1024
H_A, HD_A = 8, 64
A_DILATIONS = (1, 4, 16)
A_STEPS = 128
W_A = 2048
H_B, KV_B, HD_B, W_B = 8, 2, 64, 128
H_C, HD_C, N_MEM = 4, 128, 256
NUM_BUCKETS, MAX_DISTANCE = 32, 2048
EPS = 1e-6
NEG = -1e30
BLK = 128
UNROLL = 16

IN_COLS = dict(qa=(0, 512), ka=(512, 512), va=(1024, 512), za=(1536, 512),
               qb=(2048, 512), kb=(2560, 128), vb=(2688, 128), zb=(2816, 512),
               qc=(3328, 512), zc=(3840, 512), ga=(4352, 1024), gb=(5376, 1024), gc=(6400, 1024))
C_END = 7424
QKV_PARTS = ("qa", "ka", "va", "qb", "kb", "vb", "qc")
OUT_PARTS = ("za", "zb", "zc", "ga", "gb", "gc")


def _packed_offsets(parts):
    offs, at = {}, 0
    for name in parts:
        offs[name] = at
        at += IN_COLS[name][1]
    return offs, at


QKV_OFF, QKV_WIDTH = _packed_offsets(QKV_PARTS)
OUT_OFF, OUT_WIDTH = _packed_offsets(OUT_PARTS)


def _pack_columns(w, parts):
    return jnp.concatenate([w[:, IN_COLS[p][0]:IN_COLS[p][0] + IN_COLS[p][1]] for p in parts], axis=1)

VMEM_LIMIT = 56 * 1024 * 1024


def _cparams(sem):
    return pltpu.CompilerParams(dimension_semantics=sem, vmem_limit_bytes=VMEM_LIMIT)


def _const_spec(shape):
    nd = len(shape)
    return pl.BlockSpec(shape, lambda *_: (0,) * nd, pipeline_mode=pl.Buffered(1))


def _t5_bucket_np(dist):
    n = np.maximum(dist, 0)
    max_exact = NUM_BUCKETS // 2
    nf = np.maximum(n, 1).astype(np.float32)
    large = max_exact + (np.log(nf / np.float32(max_exact))
                         / np.float32(math.log(MAX_DISTANCE / max_exact))
                         * np.float32(NUM_BUCKETS - max_exact)).astype(np.int32)
    return np.where(n < max_exact, n, np.minimum(large, NUM_BUCKETS - 1)).astype(np.int32)


def _a_multiplicity(dist):
    return sum(((dist % d == 0) & (dist <= A_STEPS * d)).astype(np.int32) for d in A_DILATIONS)


def _bucket_tables():
    i = np.arange(BLK)[:, None]
    j = np.arange(2 * BLK)[None, :]
    rel = i + BLK - j
    bkt_a = np.stack([np.where((rel >= 0) & (rel <= A_STEPS), _t5_bucket_np(rel * d), -1)
                      for d in A_DILATIONS]).astype(np.int32)
    bkt_b = np.where((rel >= 0) & (rel < W_B), _t5_bucket_np(rel), -1).astype(np.int32)
    dist_a = W_A - np.arange(W_A)
    mult_a = _a_multiplicity(dist_a)
    sb_a = np.broadcast_to(np.where(mult_a > 0, _t5_bucket_np(dist_a), -1), (8, W_A)).astype(np.int32)
    dist_b = W_B - np.arange(W_B)
    sb_b = np.broadcast_to(np.where(dist_b < W_B, _t5_bucket_np(dist_b), -1), (8, W_B)).astype(np.int32)
    return bkt_a, bkt_b, sb_a, sb_b, mult_a.astype(np.float32).reshape(1, W_A)


def _rms_rows(x, g):
    return x * lax.rsqrt(jnp.mean(x * x, axis=-1, keepdims=True) + EPS) * g


def _dot(a, b):
    return jnp.dot(a, b, preferred_element_type=F32)


def _dot_nt(a, b):
    return lax.dot_general(a, b, (((1,), (1,)), ((), ())), preferred_element_type=F32)


def _headnorm64(t, g, pmat):
    sq = (t * t).astype(BF16)
    width = t.shape[1]
    step = min(width, 256)
    pp = pmat[:step, :step]
    parts = [_dot(sq[:, c:c + step], pp) for c in range(0, width, step)]
    ss = parts[0] if len(parts) == 1 else jnp.concatenate(parts, axis=1)
    return t * lax.rsqrt(ss * (1.0 / 64.0) + EPS) * g


def _headnorm128(t, g):
    parts = []
    for c in range(0, t.shape[1], 128):
        seg = t[:, c:c + 128]
        parts.append(seg * lax.rsqrt(jnp.mean(seg * seg, axis=-1, keepdims=True) + EPS))
    return jnp.concatenate(parts, axis=1) * g


def _bias_kernel(tbl_ref, sink_ref, bkta_ref, bktb_ref, sba_ref, sbb_ref,
                 ba_ref, bb_ref, oa_ref, ob_ref, s0a_ref, s0b_ref, sk_ref, *, present):
    pa, pb, psa, psb = present
    for g in range(3):
        bk = bkta_ref[g]
        for h in range(H_A):
            ba_ref[g, h] = jnp.full((BLK, 2 * BLK), NEG, F32)
        for b in pa[g]:
            hit = bk == b
            for h in range(H_A):
                ba_ref[g, h] = jnp.where(hit, tbl_ref[b, h], ba_ref[g, h])
    bk = bktb_ref[...]
    for h in range(H_B):
        bb_ref[h] = jnp.full((BLK, 2 * BLK), NEG, F32)
    for b in pb:
        hit = bk == b
        for h in range(H_B):
            bb_ref[h] = jnp.where(hit, tbl_ref[b, H_A + h], bb_ref[h])
    for (src, dst, buckets, col0) in ((sba_ref, oa_ref, psa, 0), (sbb_ref, ob_ref, psb, H_A)):
        bk = src[...]
        row = lax.broadcasted_iota(jnp.int32, bk.shape, 0)
        acc = jnp.full(bk.shape, NEG, F32)
        for b in buckets:
            hit = bk == b
            for h in range(8):
                acc = jnp.where(hit & (row == h), tbl_ref[b, col0 + h], acc)
        dst[...] = acc
    row8 = lax.broadcasted_iota(jnp.int32, (8, 128), 0)
    a0 = jnp.zeros((8, 128), F32)
    b0 = jnp.zeros((8, 128), F32)
    sk = jnp.zeros((8, 128), F32)
    for h in range(8):
        a0 = jnp.where(row8 == h, tbl_ref[0, h], a0)
        b0 = jnp.where(row8 == h, tbl_ref[0, H_A + h], b0)
        sk = jnp.where(row8 == h, sink_ref[h], sk)
    s0a_ref[...] = a0
    s0b_ref[...] = b0
    sk_ref[...] = sk


def _bias_call(rel_bias, sinks, tables):
    bkt_a, bkt_b, sb_a, sb_b = tables
    uniq = lambda a: tuple(int(b) for b in np.unique(a) if b >= 0)
    present = (tuple(uniq(bkt_a[g]) for g in range(3)), uniq(bkt_b), uniq(sb_a), uniq(sb_b))
    smem = pl.BlockSpec(memory_space=pltpu.SMEM)
    vmem = pl.BlockSpec(memory_space=pltpu.VMEM)
    return pl.pallas_call(
        functools.partial(_bias_kernel, present=present),
        in_specs=[smem, smem, vmem, vmem, vmem, vmem],
        out_specs=[vmem] * 7,
        out_shape=[jax.ShapeDtypeStruct((3, H_A, BLK, 2 * BLK), F32),
                   jax.ShapeDtypeStruct((H_B, BLK, 2 * BLK), F32),
                   jax.ShapeDtypeStruct((8, W_A), F32),
                   jax.ShapeDtypeStruct((8, W_B), F32),
                   jax.ShapeDtypeStruct((8, 128), F32),
                   jax.ShapeDtypeStruct((8, 128), F32),
                   jax.ShapeDtypeStruct((8, 128), F32)],
        compiler_params=pltpu.CompilerParams(vmem_limit_bytes=VMEM_LIMIT),
        name="bias_expand",
    )(rel_bias, sinks, jnp.asarray(bkt_a), jnp.asarray(bkt_b), jnp.asarray(sb_a), jnp.asarray(sb_b))


def _qkv_kernel(x_ref, lng_ref, w_ref, g_ref, p_ref, *refs, rider=False, tiles_per_seq=1):
    if rider:
        _ride(refs[:N_RIDER_IN], refs[-1])
        refs = refs[N_RIDER_IN:-1]
    qa_ref, ka_ref, va_ref, qb_ref, kb_ref, vb_ref, qc_ref, *cache_refs = refs
    h = _rms_rows(x_ref[...], lng_ref[...]).astype(BF16)
    pmat = p_ref[...]

    def proj(part):
        c0 = QKV_OFF[part]
        return _dot(h, w_ref[:, c0:c0 + IN_COLS[part][1]])

    qa_ref[...] = _headnorm64(proj("qa"), g_ref[0:1, :], pmat)
    ka = _headnorm64(proj("ka"), g_ref[1:2, :], pmat)
    va = proj("va")
    ka_ref[...] = ka
    va_ref[...] = va
    kb = _headnorm64(proj("kb"), g_ref[3:4, 0:128], pmat)
    vb = proj("vb")
    if cache_refs:
        kt_ref, vt_ref, kbt_ref, vbt_ref = cache_refs
        kt_ref[...] = ka.T
        vt_ref[...] = va.T

        @pl.when((pl.program_id(0) + 1) % tiles_per_seq == 0)
        def _():
            tm = kb.shape[0]
            kbt_ref[...] = kb[tm - W_B:, :].T
            vbt_ref[...] = vb[tm - W_B:, :].T
    qb_ref[...] = _headnorm64(proj("qb"), g_ref[2:3, :], pmat).astype(qb_ref.dtype)
    kb_ref[...] = kb
    vb_ref[...] = vb
    qc_ref[...] = _headnorm128(proj("qc"), g_ref[4:5, :]).astype(qc_ref.dtype)


def _qkv_call(x2, lng, w_bf, gains, pmat, cache_layout_for=None, rider=None):
    n = x2.shape[0]
    tm = min(512, n)
    steps = n // tm
    row = lambda w: pl.BlockSpec((tm, w), lambda i: (i, 0))
    widths = (512, 512, 512, 512, 128, 128, 512)
    in_specs = [row(D_MODEL), _const_spec((1, D_MODEL)), _const_spec((D_MODEL, QKV_WIDTH)),
                _const_spec((5, 512)), _const_spec((256, 256))]
    args = [x2, lng, w_bf, gains, pmat]
    out_specs = [row(w) for w in widths]
    q_dtype = F32 if cache_layout_for is None else BF16
    out_shape = [jax.ShapeDtypeStruct((n, w), q_dtype if i in (3, 6) else F32) for i, w in enumerate(widths)]
    per = 1
    if cache_layout_for is not None:
        batch, seq = cache_layout_for
        per = seq // tm
        out_specs += [pl.BlockSpec((None, 512, tm), lambda i: (i // per, 0, i % per))] * 2
        out_shape += [jax.ShapeDtypeStruct((batch, 512, seq), F32)] * 2
        out_specs += [pl.BlockSpec((None, KV_B * HD_B, W_B), lambda i: (i // per, 0, 0))] * 2
        out_shape += [jax.ShapeDtypeStruct((batch, KV_B * HD_B, W_B), F32)] * 2
    if rider is not None:
        r_in, r_out = _rider_specs(rider[1], lambda i: i)
        in_specs += r_in
        args += list(rider[0])
        out_specs.append(r_out)
        out_shape.append(jax.ShapeDtypeStruct((steps, 1, 512), F32))
    return pl.pallas_call(
        functools.partial(_qkv_kernel, rider=rider is not None, tiles_per_seq=per),
        grid=(steps,),
        in_specs=in_specs,
        out_specs=out_specs,
        out_shape=out_shape,
        compiler_params=_cparams(("arbitrary",)),
        name="qkv_proj",
    )(*args)


def _memkv_kernel(m_ref, g_ref, w_ref, gk_ref, mk_ref, mv_ref):
    h = _rms_rows(m_ref[...], g_ref[...]).astype(BF16)
    mk_ref[...] = _headnorm128(_dot(h, w_ref[:, 0:512]), gk_ref[...])
    mv_ref[...] = _dot(h, w_ref[:, 512:1024])


def _memkv_call(mem2, g, w_bf, gk):
    n = mem2.shape[0]
    tm = 512
    row = lambda w: pl.BlockSpec((tm, w), lambda i: (i, 0))
    return pl.pallas_call(
        _memkv_kernel,
        grid=(n // tm,),
        in_specs=[row(D_MODEL), _const_spec((1, D_MODEL)), _const_spec((D_MODEL, 2 * 512)),
                  _const_spec((1, 512))],
        out_specs=[row(512), row(512)],
        out_shape=[jax.ShapeDtypeStruct((n, 512), F32)] * 2,
        compiler_params=_cparams(("arbitrary",)),
        name="mem_kv_proj",
    )(mem2, g, w_bf, gk)


def _out_kernel(x_ref, oa_ref, ob_ref, oc_ref, lng_ref, w_ref, wbr_ref, wout_ref, *refs):
    if len(refs) > 1:
        _ride(refs[:N_RIDER_IN], refs[-1])
    y_ref = refs[-2] if len(refs) > 1 else refs[0]
    x = x_ref[...]
    h = _rms_rows(x, lng_ref[...]).astype(BF16)
    acc = None
    for br, (o_ref, zpart, gpart) in enumerate(((oa_ref, "za", "ga"), (ob_ref, "zb", "gb"), (oc_ref, "zc", "gc"))):
        cz, cg = OUT_OFF[zpart], OUT_OFF[gpart]
        z = _dot(h, w_ref[:, cz:cz + 512])
        y = (o_ref[...] * (z * jax.nn.sigmoid(z))).astype(BF16)
        yb = _dot(y, wbr_ref[br])
        gate = jax.nn.sigmoid(_dot(h, w_ref[:, cg:cg + D_MODEL]))
        acc = gate * yb if acc is None else acc + gate * yb
    y_ref[...] = x + _dot(acc.astype(BF16), wout_ref[...])


def _out_call(x2, oa, ob, oc, lng, w_bf, wbr_bf, wout_bf, tm=512, rider=None):
    n = x2.shape[0]
    tm = min(tm, n)
    steps = n // tm
    row = lambda w: pl.BlockSpec((tm, w), lambda i: (i, 0))
    in_specs = [row(D_MODEL), row(512), row(512), row(512), _const_spec((1, D_MODEL)),
                _const_spec((D_MODEL, OUT_WIDTH)), _const_spec((3, 512, D_MODEL)), _const_spec((D_MODEL, D_MODEL))]
    args = [x2, oa, ob, oc, lng, w_bf, wbr_bf, wout_bf]
    out_specs = [row(D_MODEL)]
    out_shape = [jax.ShapeDtypeStruct((n, D_MODEL), F32)]
    if rider is not None:
        r_in, r_out = _rider_specs(rider[1], lambda i: i)
        in_specs += r_in
        args += list(rider[0])
        out_specs.append(r_out)
        out_shape.append(jax.ShapeDtypeStruct((steps, 1, 512), F32))
    return pl.pallas_call(
        _out_kernel,
        grid=(steps,),
        in_specs=in_specs,
        out_specs=out_specs,
        out_shape=out_shape,
        compiler_params=_cparams(("arbitrary",)),
        name="gated_out",
    )(*args)


PAD = BLK
CLS_PITCH = BLK + 8


def _low_lanes(rows):
    return lax.broadcasted_iota(jnp.int32, (rows, 128), 1) < 64


def _aligned(start):
    return start if isinstance(start, int) else pl.multiple_of(start, BLK)


def _pair_tile(q, k0, k1, v0, v1, b0, b1, sinks=None):
    nk = k0.shape[0]
    low = _low_lanes(BLK)
    s = _dot_nt(q, jnp.concatenate([k0, k1], axis=0)) + jnp.concatenate([b0, b1], axis=1)
    ms = [jnp.max(s[:, hh * nk:(hh + 1) * nk], axis=-1, keepdims=True) for hh in range(2)]
    if sinks is not None:
        ms = [jnp.maximum(ms[hh], sinks[hh]) for hh in range(2)]
    p = jnp.concatenate([jnp.exp(s[:, hh * nk:(hh + 1) * nk] - ms[hh]) for hh in range(2)], axis=1).astype(BF16)
    own = _low_lanes(nk)
    ones = [jnp.where(own, 1.0, 0.0).astype(BF16), jnp.where(own, 0.0, 1.0).astype(BF16)]
    vblk = jnp.concatenate([jnp.concatenate([v0, ones[0]], axis=1),
                            jnp.concatenate([v1, ones[1]], axis=1)], axis=0)
    up = _dot(p, vblk)
    return up[:, :128], jnp.where(low, ms[0], ms[1]), up[:, 128:]


def _first_bias(bias, also=None):
    gone = lax.broadcasted_iota(jnp.int32, bias.shape, 1) < BLK
    if also is not None:
        gone = gone & also
    return jnp.where(gone, NEG, bias)


def _dil_kernel(q_ref, k_ref, v_ref, bias_ref, *refs, rider=False):
    if rider:
        _ride(refs[:N_RIDER_IN], refs[N_RIDER_IN + 1])
        refs = refs[N_RIDER_IN:N_RIDER_IN + 1] + refs[N_RIDER_IN + 2:]
    o_ref, qs_ref, kc_ref, vc_ref, c4_ref, ru_ref, rm_ref, rl_ref = refs
    scale = HD_A ** -0.5
    seq = q_ref.shape[0]
    ntile = seq // BLK
    low = _low_lanes(BLK)
    d1, d2 = A_DILATIONS[1], A_DILATIONS[2]

    def src_rows(g, t):
        d = A_DILATIONS[g]
        if d == 1:
            return pl.ds(_aligned(t * BLK), BLK)
        nblk = ntile // d
        r, n = t >> (nblk.bit_length() - 1), t & (nblk - 1)
        return pl.ds(r + n * (d * BLK), BLK, stride=d)

    zero = jnp.zeros((PAD, 128), BF16)
    stagers = []
    for g in range(3):
        for hh in range(2):
            kc_ref[g, hh, 0:PAD, :] = zero
            vc_ref[g, hh, 0:PAD, :] = zero

        def stage(t, c, g=g):
            dst = pl.ds(_aligned(t * BLK), BLK)
            dstp = pl.ds(_aligned(PAD + t * BLK), BLK)
            if g < 2:
                src = src_rows(g, t)
                q, k, v = q_ref[src, :], k_ref[src, :], v_ref[src, :]
            else:
                src = pl.ds((t & (d1 - 1)) * (seq // d1) + (t >> (d1.bit_length() - 1)), BLK, stride=d2 // d1)
                q, k, v = c4_ref[0, src, :], c4_ref[1, src, :], c4_ref[2, src, :]
            if g == 1:
                c4_ref[0, dst, :], c4_ref[1, dst, :], c4_ref[2, dst, :] = q, k, v
            qs_ref[g, dst, :] = (q * scale).astype(BF16)
            kc_ref[g, 0, dstp, :] = jnp.where(low, k, 0.0).astype(BF16)
            kc_ref[g, 1, dstp, :] = jnp.where(low, 0.0, k).astype(BF16)
            vc_ref[g, 0, dstp, :] = jnp.where(low, v, 0.0).astype(BF16)
            vc_ref[g, 1, dstp, :] = jnp.where(low, 0.0, v).astype(BF16)
            return c

        stagers.append(stage)

    def tiles(g, i):
        nblk = ntile // A_DILATIONS[g]
        res = []
        for j in range(UNROLL):
            t = i * UNROLL + j
            qrows = pl.ds(_aligned(t * BLK), BLK)
            krows = pl.ds(_aligned(t * BLK), 2 * BLK)
            bias = [bias_ref[g, hh] for hh in range(2)]
            if nblk <= UNROLL and j % nblk == 0:
                bias = [_first_bias(b) for b in bias]
            elif nblk > UNROLL and j == 0:
                bias = [_first_bias(b, i == 0) for b in bias]
            res.append(_pair_tile(qs_ref[g, qrows, :], kc_ref[g, 0, krows, :], kc_ref[g, 1, krows, :],
                                  vc_ref[g, 0, krows, :], vc_ref[g, 1, krows, :], bias[0], bias[1]))
        return res

    assert UNROLL == ntile
    for g in range(3):
        for t in range(ntile):
            stagers[g](t, 0)
        res = tiles(g, 0)
        for t in range(ntile):
            rows = pl.ds(t * CLS_PITCH, BLK) if g == 2 else src_rows(g, t)
            ru_ref[g, rows, :], rm_ref[g, rows, :], rl_ref[g, rows, :] = res[t]

    def natural_rows(ref, i):
        parts = [ref[2, pl.ds((v % 2) * 8 * CLS_PITCH + i * (BLK // d2) + v // 2, 8, stride=CLS_PITCH), :]
                 for v in range(BLK // 8)]
        return jnp.concatenate(parts, axis=0)

    def combine(i, c):
        rows = pl.ds(_aligned(i * BLK), BLK)
        ms = [rm_ref[0, rows, :], rm_ref[1, rows, :], natural_rows(rm_ref, i)]
        mx = jnp.maximum(jnp.maximum(ms[0], ms[1]), ms[2])
        ws = [jnp.exp(m - mx) for m in ms]
        num = ws[0] * ru_ref[0, rows, :] + ws[1] * ru_ref[1, rows, :] + ws[2] * natural_rows(ru_ref, i)
        den = ws[0] * rl_ref[0, rows, :] + ws[1] * rl_ref[1, rows, :] + ws[2] * natural_rows(rl_ref, i)
        o_ref[rows, :] = num / den
        return c

    lax.fori_loop(0, ntile, combine, 0, unroll=2)


def _dil_call(qa, ka, va, bias_a, batch, seq, rider=None):
    assert seq == A_DILATIONS[2] * BLK and UNROLL % (seq // (A_DILATIONS[1] * BLK)) == 0
    blk = pl.BlockSpec((None, seq, 128), lambda b, p: (b, 0, p))
    pairs = H_A // 2
    in_specs = [blk, blk, blk, pl.BlockSpec((3, 2, BLK, 2 * BLK), lambda b, p: (0, p, 0, 0))]
    args = [qa.reshape(batch, seq, 512), ka.reshape(batch, seq, 512), va.reshape(batch, seq, 512), bias_a]
    out_specs = [blk]
    out_shape = [jax.ShapeDtypeStruct((batch, seq, 512), F32)]
    if rider is not None:
        r_in, r_out = _rider_specs(rider[1], lambda b, p: b * pairs + p)
        in_specs += r_in
        args += list(rider[0])
        out_specs.append(r_out)
        out_shape.append(jax.ShapeDtypeStruct((batch * pairs, 1, 512), F32))
    return pl.pallas_call(
        functools.partial(_dil_kernel, rider=rider is not None),
        grid=(batch, pairs),
        in_specs=in_specs,
        out_specs=out_specs,
        out_shape=out_shape,
        scratch_shapes=[pltpu.VMEM((3, seq, 128), BF16), pltpu.VMEM((3, 2, PAD + seq, 128), BF16),
                        pltpu.VMEM((3, 2, PAD + seq, 128), BF16), pltpu.VMEM((3, seq, 128), F32)]
                       + [pltpu.VMEM((3, A_DILATIONS[2] * CLS_PITCH, 128), F32)] * 3,
        compiler_params=_cparams(("arbitrary", "arbitrary")),
        name="dilated_attn",
    )(*args)


def _swa_kernel(sink_ref, q_ref, k_ref, v_ref, bias_ref, o_ref, qs_ref, kc_ref, vc_ref):
    scale = HD_B ** -0.5
    seq = q_ref.shape[0]
    ntile = seq // BLK
    pairs = q_ref.shape[1] // 128
    kv = pl.program_id(1)
    low = _low_lanes(BLK)
    own_half = (lax.broadcasted_iota(jnp.int32, (BLK, 128), 1) >= 64).astype(jnp.int32) == kv

    def dup(xf):
        return jnp.where(own_half, xf, pltpu.roll(xf, 64, 1))

    zero = jnp.zeros((PAD, 128), BF16)
    for hh in range(2):
        kc_ref[hh, 0:PAD, :] = zero
        vc_ref[hh, 0:PAD, :] = zero

    def stage(t, c):
        src = pl.ds(pl.multiple_of(t * BLK, BLK), BLK)
        dstp = pl.ds(pl.multiple_of(PAD + t * BLK, BLK), BLK)
        for pp in range(pairs):
            qs_ref[pp, src, :] = (q_ref[src, 128 * pp:128 * (pp + 1)] * scale).astype(BF16)
        k = dup(k_ref[src, :])
        v = dup(v_ref[src, :])
        kc_ref[0, dstp, :] = jnp.where(low, k, 0.0).astype(BF16)
        kc_ref[1, dstp, :] = jnp.where(low, 0.0, k).astype(BF16)
        vc_ref[0, dstp, :] = jnp.where(low, v, 0.0).astype(BF16)
        vc_ref[1, dstp, :] = jnp.where(low, 0.0, v).astype(BF16)
        return c

    lax.fori_loop(0, ntile, stage, 0, unroll=4)

    for pp in range(pairs):
        head = (kv * pairs + pp) * 2
        sinks = (sink_ref[head], sink_ref[head + 1])
        sk = jnp.where(low, sinks[0], sinks[1])

        def body(i, c, pp=pp, sinks=sinks, sk=sk):
            res = []
            for j in range(UNROLL):
                t = i * UNROLL + j
                qrows = pl.ds(pl.multiple_of(t * BLK, BLK), BLK)
                krows = pl.ds(pl.multiple_of(t * BLK, BLK), 2 * BLK)
                bias = [bias_ref[2 * pp + hh] for hh in range(2)]
                if j == 0:
                    bias = [_first_bias(b, i == 0) for b in bias]
                u, m, l = _pair_tile(qs_ref[pp, qrows, :], kc_ref[0, krows, :], kc_ref[1, krows, :],
                                     vc_ref[0, krows, :], vc_ref[1, krows, :], bias[0], bias[1], sinks)
                res.append(u / (l + jnp.exp(sk - m)))
            for j in range(UNROLL):
                rows = pl.ds(pl.multiple_of((i * UNROLL + j) * BLK, BLK), BLK)
                o_ref[rows, 128 * pp:128 * (pp + 1)] = res[j]
            return c

        lax.fori_loop(0, ntile // UNROLL, body, 0)


def _swa_call(sinks, qb, kb, vb, bias_b, batch, seq):
    per_kv = H_B // KV_B
    qblk = pl.BlockSpec((None, seq, per_kv * HD_B), lambda b, kv: (b, 0, kv))
    kblk = pl.BlockSpec((None, seq, 128), lambda b, kv: (b, 0, 0))
    return pl.pallas_call(
        _swa_kernel,
        grid=(batch, KV_B),
        in_specs=[pl.BlockSpec(memory_space=pltpu.SMEM), qblk, kblk, kblk,
                  pl.BlockSpec((per_kv, BLK, 2 * BLK), lambda b, kv: (kv, 0, 0))],
        out_specs=qblk,
        out_shape=jax.ShapeDtypeStruct((batch, seq, 512), F32),
        scratch_shapes=[pltpu.VMEM((per_kv // 2, seq, 128), BF16), pltpu.VMEM((2, PAD + seq, 128), BF16),
                        pltpu.VMEM((2, PAD + seq, 128), BF16)],
        compiler_params=_cparams(("arbitrary", "arbitrary")),
        name="swa_attn",
    )(sinks, qb.reshape(batch, seq, 512), kb.reshape(batch, seq, 128), vb.reshape(batch, seq, 128), bias_b)


def _mem_attn_kernel(q_ref, mk_ref, mv_ref, o_ref):
    scale = HD_C ** -0.5
    ones = jnp.ones((N_MEM, HD_C), BF16)
    tq = 512
    for h in range(H_C):
        cs = slice(h * HD_C, (h + 1) * HD_C)
        k = mk_ref[:, cs].astype(BF16)
        v = jnp.concatenate([mv_ref[:, cs].astype(BF16), ones], axis=1)
        for r in range(0, q_ref.shape[0], tq):
            s = _dot_nt(q_ref[r:r + tq, cs].astype(BF16), k) * scale
            p = jnp.exp(s - jnp.max(s, axis=-1, keepdims=True)).astype(BF16)
            up = _dot(p, v)
            o_ref[r:r + tq, cs] = up[:, :HD_C] / up[:, HD_C:]


def _mem_attn_call(qc, mk, mv, batch, seq):
    tq = 1024
    return pl.pallas_call(
        _mem_attn_kernel,
        grid=(batch, seq // tq),
        in_specs=[pl.BlockSpec((None, tq, 512), lambda b, i: (b, i, 0)),
                  pl.BlockSpec((None, N_MEM, 512), lambda b, i: (b, 0, 0)),
                  pl.BlockSpec((None, N_MEM, 512), lambda b, i: (b, 0, 0))],
        out_specs=pl.BlockSpec((None, tq, 512), lambda b, i: (b, i, 0)),
        out_shape=jax.ShapeDtypeStruct((batch, seq, 512), F32),
        compiler_params=_cparams(("arbitrary", "arbitrary")),
        name="mem_attn",
    )(qc.reshape(batch, seq, 512), mk.reshape(batch, N_MEM, 512), mv.reshape(batch, N_MEM, 512))


def _head_rows(row, width, head_dim):
    rid = lax.broadcasted_iota(jnp.int32, (8, width), 0)
    hid = lax.broadcasted_iota(jnp.int32, (8, width), 1) // head_dim
    return rid == hid, jnp.where(rid == hid, jnp.broadcast_to(row, (8, width)), 0.0)


def _samp_a_one(q, kn, vn, kt_ref, vt_ref, bias_ref, mult_ref, b0_ref):
    scale = HD_A ** -0.5
    nt = W_A // 128
    own, qh = _head_rows(q, 512, HD_A)
    qcols = jnp.transpose(jnp.broadcast_to(q, (128, 512)))
    s_heads = []
    for h in range(H_A):
        rows = slice(h * HD_A, (h + 1) * HD_A)
        qb = qcols[rows, :]
        s_heads.append(jnp.concatenate(
            [jnp.sum(kt_ref[rows, 128 * j:128 * (j + 1)] * qb, axis=0, keepdims=True) for j in range(nt)], axis=1))
    s = jnp.concatenate(s_heads, axis=0) * scale + bias_ref[...]
    ln = jnp.sum(qh * kn, axis=-1, keepdims=True) * scale + b0_ref[:, 0:1]
    m = jnp.maximum(jnp.max(s, axis=-1, keepdims=True), ln)
    p = jnp.exp(s - m) * mult_ref[...]
    pn = len(A_DILATIONS) * jnp.exp(ln - m)
    den = jnp.sum(p, axis=-1, keepdims=True) + pn
    accs = []
    for h in range(H_A):
        rows = slice(h * HD_A, (h + 1) * HD_A)
        acc = vt_ref[rows, 0:128] * p[h:h + 1, 0:128]
        for j in range(1, nt):
            acc = acc + vt_ref[rows, 128 * j:128 * (j + 1)] * p[h:h + 1, 128 * j:128 * (j + 1)]
        accs.append(acc)
    pv = jnp.sum(jnp.transpose(jnp.concatenate(accs, axis=0)), axis=0, keepdims=True)
    lanes = lambda col: jnp.sum(jnp.where(own, col, 0.0), axis=0, keepdims=True)
    return (pv + lanes(pn) * vn) / lanes(den)


N_RIDER_IN = 8


def _rider_arrays(qa, kn, vn, cache_k, cache_v, bias_sa, mult, s0a):
    nb = qa.shape[0]
    kt = jnp.transpose(cache_k, (0, 2, 3, 1)).reshape(nb, 512, W_A)
    vt = jnp.transpose(cache_v, (0, 2, 3, 1)).reshape(nb, 512, W_A)
    return (qa.reshape(nb, 1, 512), kn.reshape(nb, 1, 512), vn.reshape(nb, 1, 512), kt, vt, bias_sa, mult, s0a)


def _rider_specs(first, step_of):
    row = pl.BlockSpec((None, 1, 512), lambda *ids: (first + step_of(*ids), 0, 0))
    cblk = pl.BlockSpec((None, 512, W_A), lambda *ids: (first + step_of(*ids), 0, 0))
    ins = [row, row, row, cblk, cblk, _const_spec((8, W_A)), _const_spec((1, W_A)), _const_spec((8, 128))]
    return ins, pl.BlockSpec((None, 1, 512), lambda *ids: (step_of(*ids), 0, 0))


def _ride(in_refs, o_ref):
    q_ref, kn_ref, vn_ref, kt_ref, vt_ref, bias_ref, mult_ref, b0_ref = in_refs
    o_ref[...] = _samp_a_one(q_ref[...], kn_ref[...], vn_ref[...], kt_ref, vt_ref, bias_ref, mult_ref, b0_ref)


def _samp_a_kernel(*refs):
    _ride(refs[:N_RIDER_IN], refs[-1])


def _samp_a_call(arrays, first, count):
    r_in, r_out = _rider_specs(first, lambda i: i)
    return pl.pallas_call(
        _samp_a_kernel,
        grid=(count,),
        in_specs=r_in,
        out_specs=r_out,
        out_shape=jax.ShapeDtypeStruct((count, 1, 512), F32),
        compiler_params=_cparams(("arbitrary",)),
        name="sample_dilated_attn",
    )(*arrays)


def _samp_b_kernel(q_ref, kn_ref, vn_ref, kt_ref, vt_ref, bias_ref, b0_ref, sk_ref, o_ref):
    scale = HD_B ** -0.5
    low = lax.broadcasted_iota(jnp.int32, (1, 128), 1) < 64
    rid = lax.broadcasted_iota(jnp.int32, (8, 128), 0)
    kv_half = (lax.broadcasted_iota(jnp.int32, (8, 128), 1) >= 64) == (rid >= 4)

    bb = q_ref.shape[0]
    zero = jnp.zeros((8, 128), F32)
    qhs = []
    for b in range(bb):
        q = q_ref[b:b + 1, :]
        rows = []
        for h in range(H_B):
            chunk = q[:, 128 * (h // 2):128 * (h // 2) + 128]
            if (h % 2) != (h // 4):
                chunk = pltpu.roll(chunk, 64, 1)
            rows.append(chunk)
        qhs.append(jnp.where(kv_half, jnp.concatenate(rows, axis=0), 0.0))
    qblk = jnp.concatenate([jnp.concatenate([qhs[b] if c == b else zero for c in range(bb)], axis=1)
                            for b in range(bb)], axis=0)
    kt = kt_ref[...].reshape(bb * 128, W_B).astype(BF16)
    vt = vt_ref[...].reshape(bb * 128, W_B).astype(BF16)
    tile8 = lambda ref: jnp.concatenate([ref[...]] * bb, axis=0)
    s = _dot(qblk.astype(BF16), kt) * scale + tile8(bias_ref)
    knew = jnp.concatenate([jnp.broadcast_to(kn_ref[b:b + 1, :], (8, 128)) for b in range(bb)], axis=0)
    vnew = jnp.concatenate([jnp.broadcast_to(vn_ref[b:b + 1, :], (8, 128)) for b in range(bb)], axis=0)
    qall = jnp.concatenate(qhs, axis=0)
    ln = jnp.sum(qall * knew, axis=-1, keepdims=True) * scale + tile8(b0_ref)[:, 0:1]
    sk = tile8(sk_ref)[:, 0:1]
    m = jnp.maximum(jnp.maximum(jnp.max(s, axis=-1, keepdims=True), ln), sk)
    p = jnp.exp(s - m)
    pn = jnp.exp(ln - m)
    den = jnp.sum(p, axis=-1, keepdims=True) + pn + jnp.exp(sk - m)
    rall = _dot_nt(p.astype(BF16), vt)
    for b in range(bb):
        r = (rall[8 * b:8 * b + 8, 128 * b:128 * b + 128] + pn[8 * b:8 * b + 8] * vnew[8 * b:8 * b + 8])
        r = jnp.where(kv_half, r / den[8 * b:8 * b + 8], 0.0)
        chunks = []
        for c2 in range(4):
            pair = []
            for h in (2 * c2, 2 * c2 + 1):
                piece = r[h:h + 1, :]
                if (h % 2) != (h // 4):
                    piece = pltpu.roll(piece, 64, 1)
                pair.append(piece)
            chunks.append(jnp.where(low, pair[0], pair[1]))
        o_ref[b:b + 1, :] = jnp.concatenate(chunks, axis=1)


def _samp_c_kernel(q_ref, mk_ref, mv_ref, mask_ref, o_ref):
    scale = HD_C ** -0.5
    bb = q_ref.shape[0]
    zero = jnp.zeros((H_C, HD_C), F32)
    qblk = jnp.concatenate([jnp.concatenate([q_ref[b] if c == b else zero for c in range(bb)], axis=1)
                            for b in range(bb)], axis=0)
    kcat = jnp.concatenate([mk_ref[b] for b in range(bb)], axis=1).astype(BF16)
    vcat = jnp.concatenate([mv_ref[b] for b in range(bb)], axis=1).astype(BF16)
    s = _dot_nt(qblk.astype(BF16), kcat) * scale + mask_ref[...]
    p = jnp.exp(s - jnp.max(s, axis=-1, keepdims=True))
    den = jnp.sum(p, axis=-1, keepdims=True)
    rall = _dot(p.astype(BF16), vcat)
    for b in range(bb):
        o_ref[b] = rall[H_C * b:H_C * (b + 1), HD_C * b:HD_C * (b + 1)] / den[H_C * b:H_C * (b + 1)]


N_SAMP_B_IN, N_SAMP_C_IN = 8, 4


def _samp_bc_kernel(*refs):
    ins, (ob_ref, oc_ref) = refs[:-2], refs[-2:]
    _samp_b_kernel(*ins[:N_SAMP_B_IN], ob_ref)
    _samp_c_kernel(*ins[N_SAMP_B_IN:], oc_ref)


def _samp_bc_call(qb, kn, vn, cache_bk, cache_bv, bias_sb, s0b, skb, qc, cache_mk, cache_mv):
    nb = qb.shape[0]
    bb = 8
    kt = jnp.transpose(cache_bk, (0, 2, 3, 1)).reshape(nb, 128, W_B)
    vt = jnp.transpose(cache_bv, (0, 2, 3, 1)).reshape(nb, 128, W_B)
    row = lambda w: pl.BlockSpec((bb, w), lambda i: (i, 0))
    bblk = pl.BlockSpec((bb, 128, W_B), lambda i: (i, 0, 0))
    rows = N_MEM * H_C
    head_of_row = np.arange(rows)[None, :] % H_C
    head_of_query = np.arange(bb * H_C)[:, None] % H_C
    mask = np.where(head_of_row == head_of_query, 0.0, NEG).astype(np.float32)
    qblk = pl.BlockSpec((bb, H_C, HD_C), lambda i: (i, 0, 0))
    cblk = pl.BlockSpec((bb, rows, HD_C), lambda i: (i, 0, 0))
    in_specs = [row(512), row(128), row(128), bblk, bblk, _const_spec((8, W_B)),
                _const_spec((8, 128)), _const_spec((8, 128)),
                qblk, cblk, cblk, _const_spec((bb * H_C, rows))]
    assert len(in_specs) == N_SAMP_B_IN + N_SAMP_C_IN
    ob, oc = pl.pallas_call(
        _samp_bc_kernel,
        grid=(nb // bb,),
        in_specs=in_specs,
        out_specs=[row(512), qblk],
        out_shape=[jax.ShapeDtypeStruct((nb, 512), F32), jax.ShapeDtypeStruct((nb, H_C, HD_C), F32)],
        compiler_params=_cparams(("arbitrary",)),
        name="sample_swa_mem_attn",
    )(qb, kn, vn, kt, vt, bias_sb, s0b, skb,
      qc.reshape(nb, H_C, HD_C), cache_mk.reshape(nb, rows, HD_C), cache_mv.reshape(nb, rows, HD_C),
      jnp.asarray(mask))
    return ob, oc.reshape(nb, 512)


def kernel(x_prompt, x_sample, mem_prompt, cache_a_k, cache_a_v, cache_b_k, cache_b_v, cache_mem_k,
           cache_mem_v, rel_bias, ln_g, w_in, gq_a, gk_a, gq_b, gk_b, gq_c, gk_c, sinks_b, mem_ln_g,
           w_mem_kv, w_br_a, w_br_b, w_br_c, w_out):
    batch, seq, _ = x_prompt.shape
    nsamp = x_sample.shape[0]
    assert ln_g.shape[0] == 1 and x_sample.shape[1] == 1
    assert (batch, seq, nsamp) == (8, 2048, 128) and w_in.shape == (1, D_MODEL, C_END)
    assert cache_a_k.shape == (1, nsamp, W_A, H_A, HD_A) and cache_b_k.shape == (1, nsamp, W_B, KV_B, HD_B)

    wqkv_bf = _pack_columns(w_in[0], QKV_PARTS).astype(BF16)
    wzg_bf = _pack_columns(w_in[0], OUT_PARTS).astype(BF16)
    wmem_bf = w_mem_kv[0].astype(BF16)
    wbr_bf = jnp.stack([w_br_a[0], w_br_b[0], w_br_c[0]]).astype(BF16)
    wout_bf = w_out[0].astype(BF16)
    lng = ln_g.reshape(1, D_MODEL)
    gains = jnp.stack([jnp.tile(gq_a[0], 8), jnp.tile(gk_a[0], 8), jnp.tile(gq_b[0], 8),
                       jnp.tile(gk_b[0], 8), jnp.tile(gq_c[0], 4)])
    gk_c4 = jnp.tile(gk_c[0], 4).reshape(1, 512)
    blockdiag = np.kron(np.eye(4, dtype=np.float32), np.ones((64, 64), np.float32))
    pmat = jnp.asarray(blockdiag, BF16)

    *tables, mult_a = _bucket_tables()
    bias_a, bias_b, bias_sa, bias_sb, s0a, s0b, skb = _bias_call(rel_bias, sinks_b.reshape(H_B), tables)

    xs2 = x_sample.reshape(nsamp, D_MODEL)
    sqa, ska, sva, sqb, skb_new, svb, sqc = _qkv_call(xs2, lng, wqkv_bf, gains, pmat)
    riders = _rider_arrays(sqa, ska, sva, cache_a_k[0], cache_a_v[0], bias_sa, jnp.asarray(mult_a), s0a)
    out_tm = 256
    n_qkv = batch * seq // 512
    n_dil = batch * (H_A // 2)
    n_out = batch * seq // out_tm
    n_alone = nsamp - n_qkv - n_dil - n_out
    assert n_alone >= 0

    xp2 = x_prompt.reshape(batch * seq, D_MODEL)
    qa, ka, va, qb, kb, vb, qc, ka_t, va_t, kb_t, vb_t, soa_1 = _qkv_call(
        xp2, lng, wqkv_bf, gains, pmat, (batch, seq), rider=(riders, 0))
    mk, mv = _memkv_call(mem_prompt.reshape(batch * N_MEM, D_MODEL), mem_ln_g.reshape(1, D_MODEL),
                         wmem_bf, gk_c4)
    oa, soa_2 = _dil_call(qa, ka, va, bias_a, batch, seq, rider=(riders, n_qkv))
    ob = _swa_call(sinks_b.reshape(H_B), qb, kb, vb, bias_b, batch, seq)
    oc = _mem_attn_call(qc, mk, mv, batch, seq)
    yp, soa_3 = _out_call(xp2, oa.reshape(-1, 512), ob.reshape(-1, 512), oc.reshape(-1, 512),
                          lng, wzg_bf, wbr_bf, wout_bf, tm=out_tm, rider=(riders, n_qkv + n_dil))
    yp = yp.reshape(batch, seq, D_MODEL)

    soa_parts = [soa_1, soa_2, soa_3]
    if n_alone:
        soa_parts.append(_samp_a_call(riders, n_qkv + n_dil + n_out, n_alone))
    soa = jnp.concatenate(soa_parts, axis=0).reshape(nsamp, 512)
    sob, soc = _samp_bc_call(sqb, skb_new, svb, cache_b_k[0], cache_b_v[0], bias_sb, s0b, skb,
                             sqc, cache_mem_k[0], cache_mem_v[0])
    ys, = _out_call(xs2, soa, sob, soc, lng, wzg_bf, wbr_bf, wout_bf)

    return (yp, ys.reshape(nsamp, 1, D_MODEL),
            jnp.transpose(ka_t.reshape(1, batch, H_A, HD_A, seq), (0, 1, 4, 2, 3)),
            jnp.transpose(va_t.reshape(1, batch, H_A, HD_A, seq), (0, 1, 4, 2, 3)),
            jnp.transpose(kb_t.reshape(1, batch, KV_B, HD_B, W_B), (0, 1, 4, 2, 3)),
            jnp.transpose(vb_t.reshape(1, batch, KV_B, HD_B, W_B), (0, 1, 4, 2, 3)),
            mk.reshape(1, batch, N_MEM, H_C, HD_C), mv.reshape(1, batch, N_MEM, H_C, HD_C),
            ska.reshape(1, nsamp, 1, H_A, HD_A), sva.reshape(1, nsamp, 1, H_A, HD_A),
            skb_new.reshape(1, nsamp, 1, KV_B, HD_B), svb.reshape(1, nsamp, 1, KV_B, HD_B))
```
